```python
import math
import jax
import jax.numpy as jnp
from jax import lax
import numpy as np

D_MODEL = 4096
BATCH = 4
SEQ = 2048
DEPTH = 2
DEC_BATCH = 8
DEC_SEQ = 8
PAST_LEN = 16384
PAGE_SIZE = 128

DH = 128
MIX_HEADS = D_MODEL // DH
H_A = MIX_HEADS // 4
HKV_A = H_A // 4
G_A = H_A // HKV_A
H_B = MIX_HEADS // 4
HKV_B = H_B // 4
G_B = H_B // HKV_B
H_C = MIX_HEADS // 8
HKV_C = H_C // 2
G_C = H_C // HKV_C
H_D = MIX_HEADS // 4
HKV_D = H_D // 4
G_D = H_D // HKV_D
D_FF = -(-(8 * D_MODEL) // (3 * 256)) * 256
CMP_LEN = 32
CMP_STRIDE = 16
CMP_HIDDEN = 256
SEL_BLOCK = 64
SEL_TOP = 16
WINDOW = 512
MOBA_BLOCK = 256
MOBA_TOP = 3
N_BUCKETS = 32
MAX_DIST = 128
H_BIAS = H_A + H_C + H_D
FOX_BIAS_INIT = 2.0
QBLK = 128
GATHER_QBLK = 16
NEG = -1e30
FORCE = 1e4

_COLS = (
    ('a_q', H_A * DH), ('a_kc', HKV_A * DH), ('a_vc', HKV_A * DH), ('a_ks', HKV_A * DH),
    ('a_vs', HKV_A * DH), ('a_kw', HKV_A * DH), ('a_vw', HKV_A * DH), ('a_g', H_A * 3),
    ('b_q', H_B * DH), ('b_k', HKV_B * DH), ('b_v', HKV_B * DH), ('b_f', H_B),
    ('c_q', H_C * 2 * DH), ('c_k', HKV_C * 2 * DH), ('c_v', HKV_C * 2 * DH),
    ('d_q', H_D * DH), ('d_k', HKV_D * DH), ('d_v', HKV_D * DH),
)
P_IN = sum(w for _, w in _COLS)

kernel_name = 'hybrid_nsa_fox_diff_moba_decode_step'


def _rms(x, g, eps=1e-6):
    xf = x.astype(jnp.float32)
    y = xf * lax.rsqrt(jnp.mean(xf * xf, axis=-1, keepdims=True) + eps)
    return (y * g.astype(jnp.float32)).astype(x.dtype)


def _split_cols(proj):
    offs = np.cumsum([w for _, w in _COLS])[:-1].tolist()
    return dict(zip([n for n, _ in _COLS], jnp.split(proj, offs, axis=-1)))


def _masked_softmax(s, valid):
    s = jnp.where(valid, s, NEG)
    m = jnp.max(s, axis=-1, keepdims=True)
    e = jnp.where(valid, jnp.exp(s - m), 0.0)
    return e / jnp.maximum(jnp.sum(e, axis=-1, keepdims=True), 1e-30)


def _t5_bucket(dist):
    n = jnp.maximum(jnp.asarray(dist), 0)
    exact = N_BUCKETS // 2
    nf = jnp.maximum(n, 1).astype(jnp.float32)
    big = exact + (jnp.log(nf / exact) / math.log(MAX_DIST / exact) * (N_BUCKETS - exact)).astype(jnp.int32)
    return jnp.where(n < exact, n, jnp.minimum(big, N_BUCKETS - 1))


def _rel_bias(tab, dist, hkv):
    b = jnp.moveaxis(tab[_t5_bucket(dist)].astype(jnp.float32), -1, -3)
    return b.reshape(b.shape[:-3] + (hkv, -1) + b.shape[-2:])


def _sdpa(q, k, v, valid, bias):
    s = jnp.einsum('...qhgd,...khd->...hgqk', q, k, preferred_element_type=jnp.float32) * (q.shape[-1] ** -0.5) + bias
    p = _masked_softmax(s, valid)
    return jnp.einsum('...hgqk,...khd->...qhgd', p.astype(v.dtype), v), p


def _query_blocks(fn, blk, args, axes):
    tq = args[0].shape[axes[0]]
    if tq <= blk or tq % blk:
        return fn(*args)
    n = tq // blk

    def split(a, ax):
        a = jnp.asarray(a)
        return jnp.moveaxis(a.reshape(a.shape[:ax] + (n, blk) + a.shape[ax + 1:]), ax, 0)

    xs = tuple(split(a, ax) for a, ax in zip(args, axes))
    ys = jnp.moveaxis(lax.map(lambda t: fn(*t), xs), 0, 1)
    return ys.reshape(ys.shape[:1] + (tq,) + ys.shape[3:])


def _gather_pages(cache, page_table):
    g = cache[page_table]
    return g.reshape((page_table.shape[0], page_table.shape[1] * cache.shape[1]) + cache.shape[2:])


def _compress(k, pos, w1, w2):
    B, T = k.shape[:2]
    n_cmp = (T - CMP_LEN) // CMP_STRIDE + 1
    idx = np.arange(n_cmp)[:, None] * CMP_STRIDE + np.arange(CMP_LEN)[None, :]
    blk = k[:, idx] + pos[None, None, :, None, :]
    blk = jnp.swapaxes(blk, 2, 3).reshape(B, n_cmp, k.shape[2], CMP_LEN * DH)
    return jax.nn.gelu(blk @ w1) @ w2


def _nsa_sel_block(qc, pc, sc, kb, vb, tab):
    B, Tb = qc.shape[:2]
    bi = jnp.arange(B)[:, None, None, None]
    hi = jnp.arange(HKV_A)[None, :, None, None]
    kg = kb[bi, hi, sc].reshape(B, HKV_A, Tb, -1, DH)
    vg = vb[bi, hi, sc].reshape(B, HKV_A, Tb, -1, DH)
    kpos = (sc[..., None] * SEL_BLOCK + jnp.arange(SEL_BLOCK)).reshape(B, HKV_A, Tb, -1)
    dist = pc[:, None] - kpos
    bias = jnp.moveaxis(tab.reshape(N_BUCKETS, HKV_A, G_A)[_t5_bucket(dist), hi].astype(jnp.float32), -1, 2)
    s = jnp.einsum('bqhgd,bhqkd->bhgqk', qc, kg, preferred_element_type=jnp.float32) * (DH ** -0.5) + bias
    p = _masked_softmax(s, (dist >= 0)[:, :, None])
    return jnp.einsum('bhgqk,bhqkd->bqhgd', p.astype(vg.dtype), vg)


def _nsa_cmp_sel(q, q_pos, kc, vc, ks, vs, cmp_pos, cmp_w1, cmp_w2, tab):
    B, Tk = kc.shape[:2]
    kcmp = _compress(kc, cmp_pos[0], cmp_w1[0], cmp_w2[0])
    vcmp = _compress(vc, cmp_pos[1], cmp_w1[1], cmp_w2[1])
    n_cmp = kcmp.shape[1]
    cmp_start = np.arange(n_cmp) * CMP_STRIDE
    dist = q_pos[:, None] - (cmp_start + CMP_LEN - 1)[None, :]
    o_cmp, p_cmp = _sdpa(q, kcmp, vcmp, dist >= 0, _rel_bias(tab, dist, HKV_A))
    n_sel = -(-Tk // SEL_BLOCK)
    jstart = np.arange(n_sel) * SEL_BLOCK
    overlap = ((cmp_start[:, None] < jstart[None, :] + SEL_BLOCK) & (cmp_start[:, None] + CMP_LEN > jstart[None, :])).astype(np.float32)
    imp = jnp.einsum('bhgqn,nj->bhqj', p_cmp, jnp.asarray(overlap))
    cur = q_pos // SEL_BLOCK
    jb = np.arange(n_sel)[None, :]
    forced = (jb == 0) | (jb == cur[:, None]) | (jb == cur[:, None] - 1)
    score = jnp.where(jb <= cur[:, None], imp + jnp.where(forced, FORCE, 0.0), NEG)
    _, sel = lax.top_k(score, min(SEL_TOP, n_sel))
    pad = ((0, 0), (0, n_sel * SEL_BLOCK - Tk), (0, 0), (0, 0))
    kb = jnp.pad(ks, pad).reshape(B, n_sel, SEL_BLOCK, HKV_A, DH).transpose(0, 3, 1, 2, 4)
    vb = jnp.pad(vs, pad).reshape(B, n_sel, SEL_BLOCK, HKV_A, DH).transpose(0, 3, 1, 2, 4)
    o_sel = _query_blocks(lambda qc, pc, scc: _nsa_sel_block(qc, pc, scc, kb, vb, tab), GATHER_QBLK, (q, q_pos, sel), (1, 0, 2))
    return o_cmp, o_sel


def _window_attn(q, q_pos, k, v, k_pos, tab):
    dist = q_pos[..., :, None] - k_pos[..., None, :]
    valid = (dist >= 0) & (dist < WINDOW) & (k_pos[..., None, :] >= 0)
    o, _ = _sdpa(q, k, v, valid[..., None, None, :, :], _rel_bias(tab, dist, HKV_A))
    return o


def _fox_block(qc, pc, cq, k, v, ck):
    kpos = jnp.arange(k.shape[1])
    valid = kpos[None, :] <= pc[:, None]
    cq = jnp.moveaxis(cq.reshape(cq.shape[:2] + (HKV_B, G_B)), 1, -1)
    cks = jnp.moveaxis(ck.reshape(ck.shape[:2] + (HKV_B, G_B)), 1, -1)
    o, _ = _sdpa(qc, k, v, valid, cq[..., :, None] - cks[..., None, :])
    return o


def _diff_block(qc, pc, k, v, lam, tab):
    kpos = jnp.arange(k.shape[1])
    dist = pc[:, None] - kpos[None, :]
    valid = dist >= 0
    bias = _rel_bias(tab, dist, HKV_C)
    sc = DH ** -0.5
    s1 = jnp.einsum('bqhgd,bkhd->bhgqk', qc[..., 0, :], k[..., 0, :], preferred_element_type=jnp.float32) * sc + bias
    s2 = jnp.einsum('bqhgd,bkhd->bhgqk', qc[..., 1, :], k[..., 1, :], preferred_element_type=jnp.float32) * sc + bias
    p = _masked_softmax(s1, valid) - lam[:, :, None, None] * _masked_softmax(s2, valid)
    return jnp.einsum('bhgqk,bkhd->bqhgd', p.astype(v.dtype), v)


def _moba_block(qc, pc, bc, oc, kb, vb, tab):
    B = qc.shape[0]
    bi = jnp.arange(B)[:, None, None, None, None]
    hi = jnp.arange(HKV_D)[None, :, None, None, None]
    gi = jnp.arange(G_D)[None, None, :, None, None]
    kg = kb[bi, hi, bc]
    vg = vb[bi, hi, bc]
    kpos = bc[..., None] * MOBA_BLOCK + jnp.arange(MOBA_BLOCK)
    dist = pc[:, None, None] - kpos
    valid = oc[..., None] & (dist >= 0)
    bias = tab.reshape(N_BUCKETS, HKV_D, G_D)[_t5_bucket(dist), hi[..., None], gi[..., None]].astype(jnp.float32)
    s = jnp.einsum('bqhgd,bhgqnrd->bhgqnr', qc, kg, preferred_element_type=jnp.float32) * (DH ** -0.5) + bias
    p = _masked_softmax(s.reshape(s.shape[:4] + (-1,)), valid.reshape(valid.shape[:4] + (-1,)))
    return jnp.einsum('bhgqk,bhgqkd->bqhgd', p.astype(vg.dtype), vg.reshape(vg.shape[:4] + (-1, DH)))


def _moba(q, q_pos, k, v, tab):
    B, Tk = k.shape[:2]
    Tq = q.shape[1]
    nb = -(-Tk // MOBA_BLOCK)
    n_full = Tk // MOBA_BLOCK
    pad = ((0, 0), (0, nb * MOBA_BLOCK - Tk), (0, 0), (0, 0))
    kb = jnp.pad(k, pad).reshape(B, nb, MOBA_BLOCK, HKV_D, DH).transpose(0, 3, 1, 2, 4)
    vb = jnp.pad(v, pad).reshape(B, nb, MOBA_BLOCK, HKV_D, DH).transpose(0, 3, 1, 2, 4)
    own_blk = q_pos // MOBA_BLOCK
    own = jnp.broadcast_to(jnp.asarray(own_blk, jnp.int32)[:, None], (B, HKV_D, G_D, Tq, 1))
    if n_full > 0:
        kmean = jnp.mean(kb[:, :, :n_full].astype(jnp.float32), axis=3)
        gate = jnp.einsum('bqhgd,bhnd->bhgqn', q.astype(jnp.float32), kmean)
        past_ok = np.arange(n_full)[None, :] < own_blk[:, None]
        score = jnp.where(past_ok, gate, NEG)
        top_s, top_i = lax.top_k(score, min(MOBA_TOP, n_full))
        blocks = jnp.concatenate([top_i.astype(jnp.int32), own], axis=-1)
        ok = jnp.concatenate([top_s > NEG / 2, jnp.ones(own.shape, bool)], axis=-1)
    else:
        blocks, ok = own, jnp.ones(own.shape, bool)
    return _query_blocks(lambda qc, pc, bc, oc: _moba_block(qc, pc, bc, oc, kb, vb, tab), GATHER_QBLK, (q, q_pos, blocks, ok), (1, 0, 3, 3))


def _layer(x, c, layer, past, p, tab):
    B, T, _ = x.shape
    q_pos = (0 if past is None else PAST_LEN) + np.arange(T)
    wbuf = min(WINDOW, PAST_LEN)
    mod = jax.nn.silu(c) @ p['ada_w'] + p['ada_b']
    sh1, sc1, gt1, sh2, sc2, gt2 = [m[:, None, :] for m in jnp.split(mod, 6, axis=-1)]
    h = _rms(x, p['norm_attn']) * (1 + sc1) + sh1
    cols = _split_cols(h @ p['w_in'])
    tab_a, tab_c, tab_d = tab[:, :H_A], tab[:, H_A:H_A + H_C], tab[:, H_A + H_C:]
    new = {}

    q_a = cols['a_q'].reshape(B, T, HKV_A, G_A, DH)
    kva = lambda n: cols[n].reshape(B, T, HKV_A, DH)
    k_w, v_w = kva('a_kw'), kva('a_vw')
    new['nsa'] = jnp.stack([kva('a_kc'), kva('a_vc'), kva('a_ks'), kva('a_vs')], axis=2)
    full = new['nsa'] if past is None else jnp.concatenate([past['nsa'], new['nsa']], axis=1)
    o_cmp, o_sel = _nsa_cmp_sel(q_a, q_pos, full[:, :, 0], full[:, :, 1], full[:, :, 2], full[:, :, 3],
                                p['cmp_pos'], p['cmp_w1'], p['cmp_w2'], tab_a)
    if past is None:
        nbq = T // QBLK
        kpad = jnp.pad(k_w, ((0, 0), (WINDOW, 0), (0, 0), (0, 0)))
        vpad = jnp.pad(v_w, ((0, 0), (WINDOW, 0), (0, 0), (0, 0)))
        idx = np.arange(nbq)[:, None] * QBLK + np.arange(WINDOW + QBLK)[None, :]
        o_win = _window_attn(q_a.reshape(B, nbq, QBLK, HKV_A, G_A, DH), q_pos.reshape(nbq, QBLK),
                             kpad[:, idx], vpad[:, idx], idx - WINDOW, tab_a).reshape(B, T, HKV_A, G_A, DH)
    else:
        kpad = jnp.concatenate([past['win'][:, :, 0], k_w], axis=1)
        vpad = jnp.concatenate([past['win'][:, :, 1], v_w], axis=1)
        o_win = _window_attn(q_a, q_pos, kpad, vpad, PAST_LEN - wbuf + np.arange(wbuf + T), tab_a)
    new['win'] = jnp.stack([kpad[:, -wbuf:], vpad[:, -wbuf:]], axis=2)
    g_a = jax.nn.sigmoid(cols['a_g'].reshape(B, T, HKV_A, G_A, 3))
    o_a = g_a[..., 0:1] * o_cmp + g_a[..., 1:2] * o_sel + g_a[..., 2:3] * o_win

    q_b = cols['b_q'].reshape(B, T, HKV_B, G_B, DH)
    new['fox'] = jnp.stack([cols['b_k'].reshape(B, T, HKV_B, DH), cols['b_v'].reshape(B, T, HKV_B, DH)], axis=2)
    new['logf'] = jax.nn.log_sigmoid(cols['b_f'].astype(jnp.float32) + p['fox_fbias'].astype(jnp.float32)).astype(x.dtype)
    lf = new['logf'].astype(jnp.float32)
    if past is None:
        full = new['fox']
    else:
        full = jnp.concatenate([past['fox'], new['fox']], axis=1)
        lf = jnp.concatenate([past['logf'].astype(jnp.float32), lf], axis=1)
    cum = jnp.cumsum(lf, axis=1)
    k_b, v_b = full[:, :, 0], full[:, :, 1]
    o_b = _query_blocks(lambda qc, pc, cc: _fox_block(qc, pc, cc, k_b, v_b, cum), QBLK, (q_b, q_pos, cum[:, -T:]), (1, 0, 1))

    lam_init = 0.8 - 0.6 * math.exp(-0.3 * layer)
    lp = p['diff_lambda'].astype(jnp.float32)
    lam = (jnp.exp(jnp.sum(lp[:, 0] * lp[:, 1], -1)) - jnp.exp(jnp.sum(lp[:, 2] * lp[:, 3], -1)) + lam_init).reshape(HKV_C, G_C)
    q_c = cols['c_q'].reshape(B, T, HKV_C, G_C, 2, DH)
    new['diff'] = jnp.stack([cols['c_k'].reshape(B, T, HKV_C, 2 * DH), cols['c_v'].reshape(B, T, HKV_C, 2 * DH)], axis=2)
    full = new['diff'] if past is None else jnp.concatenate([past['diff'], new['diff']], axis=1)
    k_c = full[:, :, 0].reshape(full.shape[:2] + (HKV_C, 2, DH))
    v_c = full[:, :, 1]
    o_c = _query_blocks(lambda qc, pc: _diff_block(qc, pc, k_c, v_c, lam, tab_c), QBLK, (q_c, q_pos), (1, 0))

    q_d = cols['d_q'].reshape(B, T, HKV_D, G_D, DH)
    new['moba'] = jnp.stack([cols['d_k'].reshape(B, T, HKV_D, DH), cols['d_v'].reshape(B, T, HKV_D, DH)], axis=2)
    full = new['moba'] if past is None else jnp.concatenate([past['moba'], new['moba']], axis=1)
    o_d = _moba(q_d, q_pos, full[:, :, 0], full[:, :, 1], tab_d)

    mg = p['mix_gain']
    e1, e2, e3 = H_A * DH, H_A * DH + H_B * DH, H_A * DH + H_B * DH + H_C * 2 * DH
    o = jnp.concatenate([
        _rms(o_a.reshape(B, T, H_A, DH), mg[:e1].reshape(H_A, DH)).reshape(B, T, -1),
        _rms(o_b.reshape(B, T, H_B, DH), mg[e1:e2].reshape(H_B, DH)).reshape(B, T, -1),
        _rms(o_c.reshape(B, T, H_C, 2 * DH), mg[e2:e3].reshape(H_C, 2 * DH)).reshape(B, T, -1) * (1 - lam_init),
        _rms(o_d.reshape(B, T, H_D, DH), mg[e3:].reshape(H_D, DH)).reshape(B, T, -1)], axis=-1)
    x = x + gt1 * (o @ p['w_out'])

    h2 = _rms(x, p['norm_ffn']) * (1 + sc2) + sh2
    x = x + gt2 * ((jax.nn.silu(h2 @ p['w_gate']) * (h2 @ p['w_up'])) @ p['w_down'])
    return x, new


def setup_inputs(seed: int = 0) -> dict:
    key = jax.random.key(seed)
    ks = iter(jax.random.split(key, 32))
    nrm = lambda shape, scale: jax.random.normal(next(ks), shape, jnp.float32) * scale
    n_pages = PAST_LEN // PAGE_SIZE
    used = DEC_BATCH * n_pages
    n_phys = used + max(1, used // 4)
    wbuf = min(WINDOW, PAST_LEN)
    page_table = jax.random.permutation(next(ks), n_phys)[:used].reshape(DEC_BATCH, n_pages).astype(jnp.int32)
    return {
        'x_prompt': nrm((BATCH, SEQ, D_MODEL), 1.0),
        'x_sample': nrm((DEC_BATCH, DEC_SEQ, D_MODEL), 1.0),
        'cache_nsa': nrm((DEPTH, n_phys, PAGE_SIZE, 4, HKV_A, DH), 1.0),
        'state_nsa_win': nrm((DEPTH, DEC_BATCH, wbuf, 2, HKV_A, DH), 1.0),
        'cache_fox': nrm((DEPTH, n_phys, PAGE_SIZE, 2, HKV_B, DH), 1.0),
        'cache_fox_logf': jax.nn.log_sigmoid(FOX_BIAS_INIT + nrm((DEPTH, n_phys, PAGE_SIZE, H_B), 1.0)),
        'cache_diff': nrm((DEPTH, n_phys, PAGE_SIZE, 2, HKV_C, 2 * DH), 1.0),
        'cache_moba': nrm((DEPTH, n_phys, PAGE_SIZE, 2, HKV_D, DH), 1.0),
        'page_table': page_table,
        'c_prompt': nrm((BATCH, D_MODEL), 1.0),
        'c_sample': nrm((DEC_BATCH, D_MODEL), 1.0),
        't5_table': nrm((N_BUCKETS, H_BIAS), 0.5),
        'ada_w': nrm((DEPTH, D_MODEL, 6 * D_MODEL), 0.5 * D_MODEL ** -0.5),
        'ada_b': nrm((DEPTH, 6 * D_MODEL), 0.02),
        'norm_attn': 1.0 + nrm((DEPTH, D_MODEL), 0.02),
        'norm_ffn': 1.0 + nrm((DEPTH, D_MODEL), 0.02),
        'w_in': nrm((DEPTH, D_MODEL, P_IN), D_MODEL ** -0.5),
        'w_out': nrm((DEPTH, D_MODEL, D_MODEL), D_MODEL ** -0.5),
        'mix_gain': 1.0 + nrm((DEPTH, D_MODEL), 0.02),
        'nsa_cmp_pos': nrm((DEPTH, 2, CMP_LEN, DH), 0.1),
        'nsa_cmp_w1': nrm((DEPTH, 2, CMP_LEN * DH, CMP_HIDDEN), (CMP_LEN * DH) ** -0.5),
        'nsa_cmp_w2': nrm((DEPTH, 2, CMP_HIDDEN, DH), CMP_HIDDEN ** -0.5),
        'fox_fbias': FOX_BIAS_INIT + nrm((DEPTH, H_B), 0.5),
        'diff_lambda': nrm((DEPTH, H_C, 4, DH), 0.1),
        'w_gate': nrm((DEPTH, D_MODEL, D_FF), D_MODEL ** -0.5),
        'w_up': nrm((DEPTH, D_MODEL, D_FF), D_MODEL ** -0.5),
        'w_down': nrm((DEPTH, D_FF, D_MODEL), D_FF ** -0.5),
        'final_norm': 1.0 + nrm((D_MODEL,), 0.02),
    }


def reference(x_prompt, x_sample, cache_nsa, state_nsa_win, cache_fox, cache_fox_logf, cache_diff, cache_moba,
              page_table, c_prompt, c_sample, t5_table, ada_w, ada_b, norm_attn, norm_ffn, w_in, w_out, mix_gain,
              nsa_cmp_pos, nsa_cmp_w1, nsa_cmp_w2, fox_fbias, diff_lambda, w_gate, w_up, w_down, final_norm):
    names = ('nsa', 'win', 'fox', 'logf', 'diff', 'moba')
    new_p = {n: [] for n in names}
    new_s = {n: [] for n in names}
    y_p, y_s = x_prompt, x_sample
    for l in range(DEPTH):
        prm = {'ada_w': ada_w[l], 'ada_b': ada_b[l], 'norm_attn': norm_attn[l], 'norm_ffn': norm_ffn[l],
               'w_in': w_in[l], 'w_out': w_out[l], 'mix_gain': mix_gain[l], 'cmp_pos': nsa_cmp_pos[l],
               'cmp_w1': nsa_cmp_w1[l], 'cmp_w2': nsa_cmp_w2[l], 'fox_fbias': fox_fbias[l],
               'diff_lambda': diff_lambda[l], 'w_gate': w_gate[l], 'w_up': w_up[l], 'w_down': w_down[l]}
        past = {'nsa': _gather_pages(cache_nsa[l], page_table), 'win': state_nsa_win[l],
                'fox': _gather_pages(cache_fox[l], page_table), 'logf': _gather_pages(cache_fox_logf[l], page_table),
                'diff': _gather_pages(cache_diff[l], page_table), 'moba': _gather_pages(cache_moba[l], page_table)}
        y_p, st_p = _layer(y_p, c_prompt, l, None, prm, t5_table)
        y_s, st_s = _layer(y_s, c_sample, l, past, prm, t5_table)
        for n in names:
            new_p[n].append(st_p[n])
            new_s[n].append(st_s[n])
    y_p = _rms(y_p, final_norm)
    y_s = _rms(y_s, final_norm)
    nsa_p, nsa_s = jnp.stack(new_p['nsa']), jnp.stack(new_s['nsa'])
    win_p, win_s = jnp.stack(new_p['win']), jnp.stack(new_s['win'])
    fox_p, fox_s = jnp.stack(new_p['fox']), jnp.stack(new_s['fox'])
    logf_p, logf_s = jnp.stack(new_p['logf']), jnp.stack(new_s['logf'])
    diff_p, diff_s = jnp.stack(new_p['diff']), jnp.stack(new_s['diff'])
    moba_p, moba_s = jnp.stack(new_p['moba']), jnp.stack(new_s['moba'])
    return (y_p, y_s, nsa_p, nsa_s, win_p, win_s, fox_p, fox_s, logf_p, logf_s, diff_p, diff_s, moba_p, moba_s)
```

```python
import functools
import math

import jax
import jax.numpy as jnp
import numpy as np
from jax import lax
from jax.experimental import pallas as pl
from jax.experimental.pallas import tpu as pltpu

DH = 128
CMP_LEN = 32
CMP_STRIDE = 16
SEL_BLOCK = 64
SEL_TOP = 16
WINDOW = 512
MOBA_BLOCK = 256
MOBA_TOP = 3
N_BUCKETS = 32
MAX_DIST = 128
NEG = -1e30
FORCE = 1e4
RMS_EPS = 1e-6
BELOW_NEG = -3e38

LANE = 128
V7X_VMEM_LIMIT_BYTES = 60000 * 1024

U_QA, U_QB, U_QD, U_QC = 0, 8, 16, 24
U_NSA, U_WIN, U_FOX, U_DIFF, U_MOBA, U_AG, U_BF, U_TOTAL = 32, 40, 44, 48, 56, 60, 61, 62

BF16 = jnp.bfloat16
F32 = jnp.float32


def _cparams(*sem):
    return pltpu.CompilerParams(dimension_semantics=sem, vmem_limit_bytes=V7X_VMEM_LIMIT_BYTES)


def _silu(x):
    return x * jax.nn.sigmoid(x)


def _div_pow2(x, d):
    assert d & (d - 1) == 0
    return lax.shift_right_logical(x, jnp.int32(d.bit_length() - 1))


def _mm_body(*refs, nk, tk, k_rem, silu_in, dual, has_bias, has_res):
    it = iter(refs)
    x_ref, w_ref = next(it), next(it)
    w2_ref = next(it) if dual else None
    b_ref = next(it) if has_bias else None
    r_ref, g_ref = (next(it), next(it)) if has_res else (None, None)
    o_ref, acc = next(it), next(it)
    acc2 = next(it) if dual else None
    k = pl.program_id(2)

    @pl.when(k == 0)
    def _():
        acc[...] = jnp.zeros_like(acc)
        if dual:
            acc2[...] = jnp.zeros_like(acc2)

    xv = x_ref[...]
    if silu_in:
        xv = _silu(xv.astype(F32))
    xv = xv.astype(BF16)
    wv = w_ref[...].astype(BF16)
    w2v = w2_ref[...].astype(BF16) if dual else None
    if k_rem:
        lim = jnp.where(k == nk - 1, k_rem, tk)
        xv = jnp.where(lax.broadcasted_iota(jnp.int32, xv.shape, 1) < lim, xv, jnp.zeros_like(xv))
        rows = lax.broadcasted_iota(jnp.int32, wv.shape, 0) < lim
        wv = jnp.where(rows, wv, jnp.zeros_like(wv))
        if dual:
            w2v = jnp.where(rows, w2v, jnp.zeros_like(w2v))
    acc[...] += jnp.dot(xv, wv, preferred_element_type=F32)
    if dual:
        acc2[...] += jnp.dot(xv, w2v, preferred_element_type=F32)

    @pl.when(k == nk - 1)
    def _():
        r = acc[...]
        if dual:
            r = _silu(r) * acc2[...]
        if has_bias:
            r = r + b_ref[...]
        if has_res:
            r = r_ref[...] + g_ref[...] * r
        o_ref[...] = r.astype(o_ref.dtype)


def _matmul(x, w, layer, *, tm, tn, tk, w2=None, bias=None, res=None, gate=None, rows_per_batch=None,
            silu_in=False, out_dtype=F32):
    M, K = x.shape
    N = w.shape[-1]
    assert M % tm == 0
    nk = pl.cdiv(K, tk)
    k_rem = K % tk
    grid = (M // tm, pl.cdiv(N, tn), nk)
    if layer is None:
        w_spec = pl.BlockSpec((tk, tn), lambda m, n, k: (k, n))
    else:
        w_spec = pl.BlockSpec((None, tk, tn), lambda m, n, k: (layer, k, n))
    in_specs = [pl.BlockSpec((tm, tk), lambda m, n, k: (m, k)), w_spec]
    args = [x, w]
    if w2 is not None:
        in_specs.append(w_spec)
        args.append(w2)
    if bias is not None:
        in_specs.append(pl.BlockSpec((1, tn), lambda m, n, k: (0, n)))
        args.append(bias)
    if res is not None:
        in_specs.append(pl.BlockSpec((tm, tn), lambda m, n, k: (m, n)))
        args.append(res)
        if gate.ndim == 2:
            in_specs.append(pl.BlockSpec((tm, tn), lambda m, n, k: (m, n)))
        else:
            assert rows_per_batch % tm == 0
            per = rows_per_batch // tm
            in_specs.append(pl.BlockSpec((None, 1, tn), lambda m, n, k: (m // per, 0, n)))
        args.append(gate)
    scratch = [pltpu.VMEM((tm, tn), F32)] * (2 if w2 is not None else 1)
    body = functools.partial(_mm_body, nk=nk, tk=tk, k_rem=k_rem, silu_in=silu_in, dual=w2 is not None,
                             has_bias=bias is not None, has_res=res is not None)
    return pl.pallas_call(
        body, grid=grid, in_specs=in_specs,
        out_specs=pl.BlockSpec((tm, tn), lambda m, n, k: (m, n)),
        out_shape=jax.ShapeDtypeStruct((M, N), out_dtype),
        scratch_shapes=scratch,
        compiler_params=_cparams("parallel", "parallel", "arbitrary"),
    )(*args)


def _norm_body(*refs, modulated):
    if modulated:
        x_ref, g_ref, sc_ref, sh_ref, o_ref = refs
    else:
        x_ref, g_ref, o_ref = refs
    x = x_ref[...]
    y = x * lax.rsqrt(jnp.mean(x * x, axis=-1, keepdims=True) + RMS_EPS) * g_ref[...]
    if modulated:
        y = y * (1.0 + sc_ref[...]) + sh_ref[...]
    o_ref[...] = y.astype(o_ref.dtype)


def _norm(x, g, layer, *, tm, sc=None, sh=None, rows_per_batch=None, out_dtype=F32):
    M, D = x.shape
    modulated = sc is not None
    if layer is None:
        g_spec = pl.BlockSpec((1, D), lambda m: (0, 0))
    else:
        g_spec = pl.BlockSpec((None, 1, D), lambda m: (layer, 0, 0))
    in_specs = [pl.BlockSpec((tm, D), lambda m: (m, 0)), g_spec]
    args = [x, g]
    if modulated:
        assert rows_per_batch % tm == 0
        per = rows_per_batch // tm
        mod_spec = pl.BlockSpec((None, 1, D), lambda m: (m // per, 0, 0))
        in_specs += [mod_spec, mod_spec]
        args += [sc, sh]
    return pl.pallas_call(
        functools.partial(_norm_body, modulated=modulated), grid=(M // tm,), in_specs=in_specs,
        out_specs=pl.BlockSpec((tm, D), lambda m: (m, 0)),
        out_shape=jax.ShapeDtypeStruct((M, D), out_dtype),
        compiler_params=_cparams("parallel"),
    )(*args)


def _gather_body(pt_ref, cache_ref, new_ref, o_ref, *, n_pages):
    j = pl.program_id(1)

    @pl.when(j < n_pages)
    def _():
        o_ref[...] = cache_ref[...]

    @pl.when(j == n_pages)
    def _():
        o_ref[...] = new_ref[...]


def _gather_pages(cache, layer, page_table, new_page):
    _, _, page, W = cache.shape
    B, n_pages = page_table.shape
    grid_spec = pltpu.PrefetchScalarGridSpec(
        num_scalar_prefetch=1, grid=(B, n_pages + 1),
        in_specs=[
            pl.BlockSpec((None, None, page, W), lambda b, j, pt: (layer, pt[b, jnp.minimum(j, n_pages - 1)], 0, 0)),
            pl.BlockSpec((None, page, W), lambda b, j, pt: (b, 0, 0)),
        ],
        out_specs=pl.BlockSpec((None, page, W), lambda b, j, pt: (b, j, 0)),
    )
    return pl.pallas_call(
        functools.partial(_gather_body, n_pages=n_pages), grid_spec=grid_spec,
        out_shape=jax.ShapeDtypeStruct((B, (n_pages + 1) * page, W), cache.dtype),
        compiler_params=_cparams("parallel", "arbitrary"),
    )(page_table, cache, new_page)


CUM_CHUNK = 256


def _cum_body(*refs, n_past, n_new, nh):
    if n_past:
        past_ref, raw_ref, fb_ref, lf_ref, cum_ref = refs
    else:
        raw_ref, fb_ref, lf_ref, cum_ref = refs
    z = raw_ref[...][:, :nh] + fb_ref[...]
    lf = -(jnp.maximum(-z, 0.0) + jnp.log1p(jnp.exp(-jnp.abs(z))))
    lf_ref[...] = lf

    def tri(n):
        return (lax.broadcasted_iota(jnp.int32, (n, n), 0) >= lax.broadcasted_iota(jnp.int32, (n, n), 1)).astype(F32)

    def scan_rows(src, dst_off, n, carry):
        c = min(CUM_CHUNK, n)
        assert n % c == 0
        t = tri(c)

        def step(i, carry):
            r0 = pl.multiple_of(i * c, c)
            blk = jnp.dot(t, src(r0, c), preferred_element_type=F32, precision=lax.Precision.HIGHEST) + carry
            cum_ref[pl.ds(dst_off + r0, c), :] = blk
            return blk[c - 1:c, :]

        return lax.fori_loop(0, n // c, step, carry)

    carry = jnp.zeros((1, nh), F32)
    if n_past:
        carry = scan_rows(lambda r0, c: past_ref[pl.ds(r0, c), :], 0, n_past, carry)
    lf_ref[...] = lf
    scan_rows(lambda r0, c: lf_ref[pl.ds(r0, c), :], n_past, n_new, carry)


def _fox_cum(raw_f, fbias, past_lf):
    B, Tn, _ = raw_f.shape
    nh = fbias.shape[-1]
    n_past = 0 if past_lf is None else past_lf.shape[1]
    in_specs, args = [], []
    if n_past:
        in_specs.append(pl.BlockSpec((None, n_past, nh), lambda b: (b, 0, 0)))
        args.append(past_lf)
    in_specs += [pl.BlockSpec((None, Tn, LANE), lambda b: (b, 0, 0)), pl.BlockSpec((1, nh), lambda b: (0, 0))]
    args += [raw_f, fbias]
    return pl.pallas_call(
        functools.partial(_cum_body, n_past=n_past, n_new=Tn, nh=nh), grid=(B,), in_specs=in_specs,
        out_specs=[pl.BlockSpec((None, Tn, nh), lambda b: (b, 0, 0)),
                   pl.BlockSpec((None, n_past + Tn, nh), lambda b: (b, 0, 0))],
        out_shape=[jax.ShapeDtypeStruct((B, Tn, nh), F32), jax.ShapeDtypeStruct((B, n_past + Tn, nh), F32)],
        compiler_params=_cparams("parallel"),
    )(*args)


def _gelu_tanh(x):
    return 0.5 * x * (1.0 + jnp.tanh(math.sqrt(2.0 / math.pi) * (x + 0.044715 * (x * x * x))))


def _compress_body(x_ref, pos_ref, w1_ref, w2_ref, o_ref, *, n):
    hidden = w1_ref.shape[-1]
    acc_lo = jnp.zeros((n, hidden), F32)
    acc_hi = jnp.zeros((n, hidden), F32)
    for rho in range(CMP_STRIDE):
        xr = x_ref[pl.ds(rho, n, stride=CMP_STRIDE), :]
        lo = (xr + pos_ref[rho:rho + 1, :]).astype(BF16)
        hi = (xr + pos_ref[rho + CMP_STRIDE:rho + CMP_STRIDE + 1, :]).astype(BF16)
        w_lo = w1_ref[rho * DH:(rho + 1) * DH, :].astype(BF16)
        w_hi = w1_ref[(rho + CMP_STRIDE) * DH:(rho + CMP_STRIDE + 1) * DH, :].astype(BF16)
        acc_lo += jnp.dot(lo, w_lo, preferred_element_type=F32)
        acc_hi += jnp.dot(hi, w_hi, preferred_element_type=F32)
    hid = acc_lo + pltpu.roll(acc_hi, n - 1, 0)
    o_ref[...] = jnp.dot(_gelu_tanh(hid).astype(BF16), w2_ref[...].astype(BF16), preferred_element_type=F32)


def _compress(kv, u_k, u_v, hkv, n, layer, pos, w1, w2):
    B = kv.shape[0]
    hidden = w1.shape[-1]

    def unit(j, h):
        return u_k + h + j * (u_v - u_k)

    return pl.pallas_call(
        functools.partial(_compress_body, n=n), grid=(B, 2, hkv),
        in_specs=[
            pl.BlockSpec((None, n * CMP_STRIDE, DH), lambda b, j, h: (b, 0, unit(j, h))),
            pl.BlockSpec((None, None, CMP_LEN, DH), lambda b, j, h: (layer, j, 0, 0)),
            pl.BlockSpec((None, None, CMP_LEN * DH, hidden), lambda b, j, h: (layer, j, 0, 0)),
            pl.BlockSpec((None, None, hidden, DH), lambda b, j, h: (layer, j, 0, 0)),
        ],
        out_specs=pl.BlockSpec((None, None, None, n, DH), lambda b, j, h: (b, j, h, 0, 0)),
        out_shape=jax.ShapeDtypeStruct((B, 2, hkv, n, DH), F32),
        compiler_params=_cparams("parallel", "arbitrary", "arbitrary"),
    )(kv, pos, w1, w2)


def _take_top(score, lane_ids, count):
    sel = jnp.zeros(score.shape, F32)
    taken = []
    big = jnp.int32(score.shape[-1])
    for _ in range(count):
        m = jnp.max(score, axis=-1, keepdims=True)
        idx = jnp.min(jnp.where(score == m, lane_ids, big), axis=-1, keepdims=True)
        hit = lane_ids == idx
        sel = jnp.where(hit, 1.0, sel)
        score = jnp.where(hit, BELOW_NEG, score)
        taken.append((hit, m))
    return sel, taken


def _cmp_body(q_ref, kc_ref, vc_ref, bias_ref, ov_ref, o_ref, sel_ref, *, G, tq, q_off, n_sel, top):
    qi = pl.program_id(2)
    kc = kc_ref[...]
    vc = vc_ref[...].astype(BF16)
    scale = DH ** -0.5
    psum = None
    outs = []
    for g in range(G):
        qg = q_ref[:, g * DH:(g + 1) * DH]
        s = lax.dot_general(qg, kc, (((1,), (1,)), ((), ())), preferred_element_type=F32,
                            precision=lax.Precision.HIGHEST) * scale + bias_ref[g]
        m = jnp.max(s, axis=-1, keepdims=True)
        e = jnp.where(s > NEG / 2, jnp.exp(s - m), 0.0)
        p = e / jnp.maximum(jnp.sum(e, axis=-1, keepdims=True), 1e-30)
        outs.append(jnp.dot(p.astype(BF16), vc, preferred_element_type=F32))
        psum = p if psum is None else psum + p
    o_ref[...] = jnp.concatenate(outs, axis=1)
    imp = jnp.dot(psum, ov_ref[...], preferred_element_type=F32, precision=lax.Precision.HIGHEST)
    shape = imp.shape
    jb = lax.broadcasted_iota(jnp.int32, shape, 1)
    qpos = q_off + qi * tq + lax.broadcasted_iota(jnp.int32, shape, 0)
    cur = _div_pow2(qpos, SEL_BLOCK)
    forced = (jb == 0) | (jb == cur) | (jb == cur - 1)
    score = jnp.where(jb <= cur, imp + jnp.where(forced, FORCE, 0.0), NEG)
    score = jnp.where(jb < n_sel, score, BELOW_NEG)
    sel, _ = _take_top(score, jb, top)
    sel_ref[...] = jnp.where(jb <= cur, sel, 0.0)


def _nsa_cmp(q, u_q, cmp_kv, bias, overlap, *, hkv, G, tq, q_off, n_sel):
    B, Tq, _ = q.shape
    n = cmp_kv.shape[3]
    nsp = overlap.shape[1]
    top = min(SEL_TOP, n_sel)
    body = functools.partial(_cmp_body, G=G, tq=tq, q_off=q_off, n_sel=n_sel, top=top)
    return pl.pallas_call(
        body, grid=(B, hkv, Tq // tq),
        in_specs=[
            pl.BlockSpec((None, tq, G * DH), lambda b, h, i: (b, i, u_q // G + h)),
            pl.BlockSpec((None, None, None, n, DH), lambda b, h, i: (b, 0, h, 0, 0)),
            pl.BlockSpec((None, None, None, n, DH), lambda b, h, i: (b, 1, h, 0, 0)),
            pl.BlockSpec((G, tq, n), lambda b, h, i: (h, i, 0)),
            pl.BlockSpec((n, nsp), lambda b, h, i: (0, 0)),
        ],
        out_specs=[pl.BlockSpec((None, tq, G * DH), lambda b, h, i: (b, i, h)),
                   pl.BlockSpec((None, None, tq, nsp), lambda b, h, i: (b, h, i, 0))],
        out_shape=[jax.ShapeDtypeStruct((B, Tq, hkv * G * DH), F32),
                   jax.ShapeDtypeStruct((B, hkv, Tq, nsp), F32)],
        compiler_params=_cparams("parallel", "parallel", "arbitrary"),
    )(q, cmp_kv, cmp_kv, bias, overlap)


def _moba_gate_body(q_ref, k_ref, m_ref, *, G, n_full, nbp, q_off, top):
    Tq = q_ref.shape[0]
    shape = (Tq, nbp)
    jb = lax.broadcasted_iota(jnp.int32, shape, 1)
    own = _div_pow2(q_off + lax.broadcasted_iota(jnp.int32, shape, 0), MOBA_BLOCK)
    if n_full > 0:
        kmean = jnp.sum(k_ref[...].reshape(n_full, MOBA_BLOCK, DH), axis=1) * (1.0 / MOBA_BLOCK)
        if nbp > n_full:
            kmean = jnp.concatenate([kmean, jnp.zeros((nbp - n_full, DH), F32)], axis=0)
    for g in range(G):
        mask = jnp.where(jb == own, 1.0, 0.0)
        if n_full > 0:
            gate = lax.dot_general(q_ref[:, g * DH:(g + 1) * DH], kmean, (((1,), (1,)), ((), ())),
                                   preferred_element_type=F32, precision=lax.Precision.HIGHEST)
            score = jnp.where((jb < own) & (jb < n_full), gate, NEG)
            score = jnp.where(jb < n_full, score, BELOW_NEG)
            _, taken = _take_top(score, jb, top)
            for hit, val in taken:
                mask = jnp.where(hit & (val > NEG / 2), 1.0, mask)
        m_ref[g] = mask


def _moba_gate(q, u_q, k, u_k, *, hkv, G, Tk, q_off, nbp):
    B, Tq, _ = q.shape
    n_full = Tk // MOBA_BLOCK
    top = min(MOBA_TOP, n_full)
    body = functools.partial(_moba_gate_body, G=G, n_full=n_full, nbp=nbp, q_off=q_off, top=top)
    return pl.pallas_call(
        body, grid=(B, hkv),
        in_specs=[pl.BlockSpec((None, Tq, G * DH), lambda b, h: (b, 0, u_q // G + h)),
                  pl.BlockSpec((None, max(n_full, 1) * MOBA_BLOCK, DH), lambda b, h: (b, 0, u_k + h))],
        out_specs=pl.BlockSpec((None, None, G, Tq, nbp), lambda b, h: (b, h, 0, 0, 0)),
        out_shape=jax.ShapeDtypeStruct((B, hkv, G, Tq, nbp), F32),
        compiler_params=_cparams("parallel", "arbitrary"),
    )(q, k)


def _flash_body(*refs, G, Gb, Gm, tq, tk, dv, mode, nsteps, fox, mask_blk):
    it = iter(refs)
    q_ref, k_ref, v_ref, bias_ref = next(it), next(it), next(it), next(it)
    cq_ref, ck_ref = (next(it), next(it)) if fox else (None, None)
    mask_ref = next(it) if mask_blk else None
    o_ref, m_sc, l_sc, acc_sc = next(it), next(it), next(it), next(it)
    qi, step = pl.program_id(2), pl.program_id(3)
    if mode == "band":
        ki = qi - (nsteps - 1) + step
        active = ki >= 0
    elif mode == "causal":
        ki = step
        active = ki <= qi
    else:
        ki = step
        active = step >= 0

    @pl.when(step == 0)
    def _():
        m_sc[...] = jnp.full(m_sc.shape, NEG, F32)
        l_sc[...] = jnp.zeros_like(l_sc)
        acc_sc[...] = jnp.zeros_like(acc_sc)

    @pl.when(active)
    def _():
        k = k_ref[...].astype(BF16)
        v = v_ref[...].astype(BF16)
        q = jnp.concatenate([q_ref[:, g * DH:(g + 1) * DH] for g in range(G)], axis=0).astype(BF16)
        s = lax.dot_general(q, k, (((1,), (1,)), ((), ())), preferred_element_type=F32) * (DH ** -0.5)
        s = s.reshape(G, tq, tk) + bias_ref[...]
        if fox:
            s = s + (cq_ref[...] - ck_ref[...])
        if mask_blk:
            nbp = mask_ref.shape[-1]
            kpos = ki * tk + lax.broadcasted_iota(jnp.int32, (nbp, tk), 1)
            lo = lax.broadcasted_iota(jnp.int32, (nbp, tk), 0) * mask_blk
            expand = jnp.where((kpos >= lo) & (kpos < lo + mask_blk), 1.0, 0.0).astype(BF16)
            picked = jnp.dot(mask_ref[...].reshape(Gm * tq, nbp).astype(BF16), expand, preferred_element_type=F32)
            s = jnp.where(picked.reshape(Gm, tq, tk) > 0.5, s, NEG)
        m_old = m_sc[...]
        m_new = jnp.maximum(m_old, jnp.max(s, axis=-1, keepdims=True))
        p = jnp.where(s > NEG / 2, jnp.exp(s - m_new), 0.0)
        alpha = jnp.exp(m_old - m_new)
        l_sc[...] = alpha * l_sc[...] + jnp.sum(p, axis=-1, keepdims=True)
        pv = jnp.dot(p.reshape(G * tq, tk).astype(BF16), v, preferred_element_type=F32)
        acc_sc[...] = alpha * acc_sc[...] + pv.reshape(G, tq, dv)
        m_sc[...] = m_new

    @pl.when(step == nsteps - 1)
    def _():
        o = acc_sc[...] / jnp.maximum(l_sc[...], 1e-30)
        for g in range(G):
            o_ref[:, g * dv:(g + 1) * dv] = o[g]


def _flash(q, u_q, k, u_k, v, u_v, bias, *, hkv, G, tq, tk, dv, mode, nsteps, fox=None, mask=None, mask_blk=0,
           q_stride=1, k_stride=1):
    B, Tq, _ = q.shape
    nq = Tq // tq
    Hb, NB = bias.shape[:2]
    Gb = G if Hb > 1 else 1
    vu = dv // DH

    if mode == "causal":
        kidx = lambda i, s: jnp.minimum(s, i)
        bidx = lambda i, s: jnp.clip(i - s, 0, NB - 1)
    elif mode == "band":
        kidx = lambda i, s: jnp.maximum(i - (nsteps - 1) + s, 0)
        bidx = lambda i, s: jnp.clip(nsteps - 1 - s, 0, NB - 1)
    else:
        kidx = lambda i, s: s
        bidx = lambda i, s: s

    in_specs = [
        pl.BlockSpec((None, tq, G * DH), lambda b, h, i, s: (b, i, u_q // G + h * q_stride)),
        pl.BlockSpec((None, tk, DH), lambda b, h, i, s: (b, kidx(i, s), u_k + h * k_stride)),
        pl.BlockSpec((None, tk, dv), lambda b, h, i, s: (b, kidx(i, s), u_v // vu + h)),
        pl.BlockSpec((Gb, None, tq, tk), lambda b, h, i, s: (h if Hb > 1 else 0, bidx(i, s), 0, 0)),
    ]
    args = [q, k, v, bias]
    if fox is not None:
        in_specs += [pl.BlockSpec((None, G, tq, 1), lambda b, h, i, s: (b, h, i, 0)),
                     pl.BlockSpec((None, G, 1, tk), lambda b, h, i, s: (b, h, 0, kidx(i, s)))]
        args += list(fox)
    Gm = 0
    if mask is not None:
        Gm, nbp = mask.shape[2], mask.shape[4]
        in_specs.append(pl.BlockSpec((None, None, Gm, tq, nbp), lambda b, h, i, s: (b, h, 0, i, 0)))
        args.append(mask)
    body = functools.partial(_flash_body, G=G, Gb=Gb, Gm=Gm, tq=tq, tk=tk, dv=dv, mode=mode, nsteps=nsteps,
                             fox=fox is not None, mask_blk=mask_blk if mask is not None else 0)
    return pl.pallas_call(
        body, grid=(B, hkv, nq, nsteps), in_specs=in_specs,
        out_specs=pl.BlockSpec((None, tq, G * dv), lambda b, h, i, s: (b, i, h)),
        out_shape=jax.ShapeDtypeStruct((B, Tq, hkv * G * dv), F32),
        scratch_shapes=[pltpu.VMEM((G, tq, 1), F32), pltpu.VMEM((G, tq, 1), F32), pltpu.VMEM((G, tq, dv), F32)],
        compiler_params=_cparams("parallel", "parallel", "parallel", "arbitrary"),
    )(*args)


def _head_rms(x, gain):
    return x * lax.rsqrt(jnp.mean(x * x, axis=-1, keepdims=True) + RMS_EPS) * gain


def _combine_body(cmp_ref, sel_ref, win_ref, ag_ref, b_ref, c1_ref, c2_ref, d_ref, mg_ref, lp_ref, o_ref, *,
                  h_a, h_b, h_c, h_d, lam_init):
    gates = jax.nn.sigmoid(ag_ref[...])
    col = 0
    for h in range(h_a):
        sl = slice(h * DH, (h + 1) * DH)
        o = (gates[:, 3 * h:3 * h + 1] * cmp_ref[:, sl] + gates[:, 3 * h + 1:3 * h + 2] * sel_ref[:, sl]
             + gates[:, 3 * h + 2:3 * h + 3] * win_ref[:, sl])
        o_ref[:, col:col + DH] = _head_rms(o, mg_ref[:, col:col + DH]).astype(o_ref.dtype)
        col += DH
    for h in range(h_b):
        o_ref[:, col:col + DH] = _head_rms(b_ref[:, h * DH:(h + 1) * DH], mg_ref[:, col:col + DH]).astype(o_ref.dtype)
        col += DH
    for h in range(h_c):
        lp = lp_ref[h]
        lam = (jnp.exp(jnp.sum(lp[0:1] * lp[1:2], axis=-1, keepdims=True))
               - jnp.exp(jnp.sum(lp[2:3] * lp[3:4], axis=-1, keepdims=True)) + lam_init)
        sl = slice(h * 2 * DH, (h + 1) * 2 * DH)
        o = c1_ref[:, sl] - lam * c2_ref[:, sl]
        o_ref[:, col:col + 2 * DH] = (_head_rms(o, mg_ref[:, col:col + 2 * DH]) * (1.0 - lam_init)).astype(o_ref.dtype)
        col += 2 * DH
    for h in range(h_d):
        o_ref[:, col:col + DH] = _head_rms(d_ref[:, h * DH:(h + 1) * DH], mg_ref[:, col:col + DH]).astype(o_ref.dtype)
        col += DH


def _combine(o_cmp, o_sel, o_win, proj, o_b, o_c1, o_c2, o_d, mix_gain, diff_lambda, layer, *, tm, lam_init):
    M = o_cmp.shape[0]
    D = mix_gain.shape[-1]
    h_a, h_b, h_d = o_cmp.shape[1] // DH, o_b.shape[1] // DH, o_d.shape[1] // DH
    h_c = o_c1.shape[1] // (2 * DH)
    row = lambda w: pl.BlockSpec((tm, w), lambda m: (m, 0))
    body = functools.partial(_combine_body, h_a=h_a, h_b=h_b, h_c=h_c, h_d=h_d, lam_init=lam_init)
    return pl.pallas_call(
        body, grid=(M // tm,),
        in_specs=[row(o_cmp.shape[1]), row(o_sel.shape[1]), row(o_win.shape[1]),
                  pl.BlockSpec((tm, LANE), lambda m: (m, U_AG)),
                  row(o_b.shape[1]), row(o_c1.shape[1]), row(o_c2.shape[1]), row(o_d.shape[1]),
                  pl.BlockSpec((None, 1, D), lambda m: (layer, 0, 0)),
                  pl.BlockSpec((None,) + diff_lambda.shape[1:], lambda m: (layer, 0, 0, 0))],
        out_specs=pl.BlockSpec((tm, D), lambda m: (m, 0)),
        out_shape=jax.ShapeDtypeStruct((M, D), BF16),
        compiler_params=_cparams("parallel"),
    )(o_cmp, o_sel, o_win, proj, o_b, o_c1, o_c2, o_d, mix_gain, diff_lambda)


def _t5_bucket_np(dist):
    n = np.maximum(dist, 0)
    exact = N_BUCKETS // 2
    nf = np.maximum(n, 1).astype(np.float32)
    big = exact + (np.log(nf / np.float32(exact)) / np.float32(math.log(MAX_DIST / exact))
                   * np.float32(N_BUCKETS - exact)).astype(np.int32)
    return np.where(n < exact, n, np.minimum(big, N_BUCKETS - 1)).astype(np.int32)


def _bias_from_dist(tab_t, dist, valid):
    if tab_t is None:
        return jnp.asarray(np.where(valid, 0.0, NEG).astype(np.float32))[None]
    idx = _t5_bucket_np(dist).reshape(-1)
    b = jnp.take(tab_t, jnp.asarray(idx), axis=1).reshape((tab_t.shape[0],) + dist.shape)
    return jnp.where(jnp.asarray(valid)[None], b, NEG)


def _prompt_bias(tab_t, t, n_var, window=None):
    i = np.arange(t)[:, None]
    j = np.arange(t)[None, :]
    dist = np.stack([d * t + i - j for d in range(n_var)])
    valid = dist >= 0
    if window is not None:
        valid &= dist < window
    return _bias_from_dist(tab_t, dist, valid)


def _decode_bias(tab_t, tq, kpos, q_off, tk, window=None):
    dist = (q_off + np.arange(tq))[:, None] - kpos[None, :]
    valid = (dist >= 0) & (kpos[None, :] >= 0)
    if window is not None:
        valid &= dist < window
    b = _bias_from_dist(tab_t, dist, valid)
    nk = kpos.shape[0] // tk
    return b.reshape(b.shape[0], tq, nk, tk).transpose(0, 2, 1, 3)


def _cmp_bias(tab_t, q_pos, n):
    end = np.arange(n) * CMP_STRIDE + CMP_LEN - 1
    dist = q_pos[:, None] - end[None, :]
    valid = (dist >= 0) & (np.arange(n)[None, :] < n - 1)
    return _bias_from_dist(tab_t, dist, valid)


def _overlap_np(n, n_sel, n_sel_pad):
    cs = np.arange(n)[:, None] * CMP_STRIDE
    js = np.arange(n_sel_pad)[None, :] * SEL_BLOCK
    ov = (cs < js + SEL_BLOCK) & (cs + CMP_LEN > js) & (np.arange(n)[:, None] < n - 1) & (np.arange(n_sel_pad)[None, :] < n_sel)
    return ov.astype(np.float32)


def _round_up(x, m):
    return -(-x // m) * m


def _reorder_w_in(w, dims):
    h_a, hkv_a, h_b, hkv_b, h_c, hkv_c, h_d, hkv_d = dims
    widths = [h_a * DH] + [hkv_a * DH] * 6 + [h_a * 3, h_b * DH, hkv_b * DH, hkv_b * DH, h_b,
                                             h_c * 2 * DH, hkv_c * 2 * DH, hkv_c * 2 * DH, h_d * DH, hkv_d * DH, hkv_d * DH]
    offs = np.concatenate([[0], np.cumsum(widths)])
    seg = lambda i: w[:, offs[i]:offs[i + 1]]
    D = w.shape[0]
    g_c = h_c // hkv_c
    c_q = seg(12).reshape(D, hkv_c, g_c, 2, DH).transpose(0, 1, 3, 2, 4).reshape(D, -1)
    pad = lambda a: jnp.pad(a, ((0, 0), (0, LANE - a.shape[1])))
    return jnp.concatenate([seg(0), seg(8), seg(15), c_q, seg(1), seg(2), seg(3), seg(4), seg(5), seg(6),
                            seg(9), seg(10), seg(13), seg(14), seg(16), seg(17), pad(seg(7)), pad(seg(11))], axis=1)


def _layer(x, B, T, mod, layer, past, prm, tabs, dims, tiles):
    h_a, hkv_a, h_b, hkv_b, h_c, hkv_c, h_d, hkv_d = dims
    g_a, g_b, g_c, g_d = h_a // hkv_a, h_b // hkv_b, h_c // hkv_c, h_d // hkv_d
    D = x.shape[1]
    M = B * T
    sh1, sc1, gt1, sh2, sc2, gt2 = [m[:, None, :] for m in jnp.split(mod, 6, axis=-1)]
    tm_n = tiles["norm_tm"]
    h = _norm(x, prm["norm_attn"], layer, tm=tm_n, sc=sc1, sh=sh1, rows_per_batch=T, out_dtype=BF16)
    proj = _matmul(h, prm["w_in_r"], None, **tiles["w_in"]).reshape(B, T, U_TOTAL * LANE)
    new = {
        "nsa": proj[:, :, U_NSA * LANE:U_WIN * LANE], "fox": proj[:, :, U_FOX * LANE:U_DIFF * LANE],
        "diff": proj[:, :, U_DIFF * LANE:U_MOBA * LANE], "moba": proj[:, :, U_MOBA * LANE:U_AG * LANE],
    }
    kw_new = proj[:, :, U_WIN * LANE:U_FOX * LANE]
    tq, tk = tiles["tq"], tiles["tk"]

    if past is None:
        q_off, Tk = 0, T
        nsa_kv, u_nsa = proj, U_NSA
        fox_kv, u_fox = proj, U_FOX
        diff_kv, u_diff = proj, U_DIFF
        moba_kv, u_moba = proj, U_MOBA
        win_kv, u_win = proj, U_WIN
        assert T >= tiles["wbuf"]
        new["win"] = kw_new[:, T - tiles["wbuf"]:]
        lf_new, cum = _fox_cum(proj[:, :, U_BF * LANE:], prm["fox_fbias"], None)
        mode, nsteps = "causal", T // tk
        band_steps = tabs["win"].shape[1]
        n_cmp_pad = T // CMP_STRIDE
        tk_pad = T
    else:
        q_off = past["len"]
        Tk = q_off + T
        page = past["page"]
        newpage = lambda a: jnp.pad(a, ((0, 0), (0, page - T), (0, 0)))
        nsa_kv, u_nsa = _gather_pages(past["nsa"], layer, past["pt"], newpage(new["nsa"])), 0
        fox_kv, u_fox = _gather_pages(past["fox"], layer, past["pt"], newpage(new["fox"])), 0
        diff_kv, u_diff = _gather_pages(past["diff"], layer, past["pt"], newpage(new["diff"])), 0
        moba_kv, u_moba = _gather_pages(past["moba"], layer, past["pt"], newpage(new["moba"])), 0
        past_lf = _gather_pages(past["logf"], layer, past["pt"], jnp.zeros((B, page, h_b), F32))[:, :q_off]
        win_all = jnp.concatenate([past["win"][layer], kw_new], axis=1)
        wbuf = past["win"].shape[2]
        new["win"] = win_all[:, -wbuf:]
        tk_win = tabs["win"].shape[-1]
        win_kv, u_win = jnp.pad(win_all, ((0, 0), (0, tk_win - win_all.shape[1]), (0, 0))), 0
        lf_new, cum = _fox_cum(proj[:, :, U_BF * LANE:], prm["fox_fbias"], past_lf)
        mode, nsteps = "full", nsa_kv.shape[1] // tk
        band_steps = 1
        n_cmp_pad = q_off // CMP_STRIDE
        tk_pad = nsa_kv.shape[1]
    new["logf"] = lf_new

    cmp_kv = _compress(nsa_kv, u_nsa, u_nsa + hkv_a, hkv_a, n_cmp_pad, layer,
                       prm["cmp_pos"], prm["cmp_w1"], prm["cmp_w2"])
    n_sel = -(-Tk // SEL_BLOCK)
    o_cmp, sel_mask = _nsa_cmp(proj, U_QA, cmp_kv, tabs["cmp"], tabs["overlap"], hkv=hkv_a, G=g_a,
                               tq=min(tiles["cmp_tq"], T), q_off=q_off, n_sel=n_sel)
    o_sel = _flash(proj, U_QA, nsa_kv, u_nsa + 2 * hkv_a, nsa_kv, u_nsa + 3 * hkv_a, tabs["a"], hkv=hkv_a, G=g_a,
                   tq=tq, tk=tk, dv=DH, mode=mode, nsteps=nsteps, mask=sel_mask[:, :, None], mask_blk=SEL_BLOCK)
    if past is None:
        o_win = _flash(proj, U_QA, win_kv, u_win, win_kv, u_win + hkv_a, tabs["win"], hkv=hkv_a, G=g_a,
                       tq=tabs["win"].shape[2], tk=tabs["win"].shape[3], dv=DH, mode="band", nsteps=band_steps)
    else:
        o_win = _flash(proj, U_QA, win_kv, u_win, win_kv, u_win + hkv_a, tabs["win"], hkv=hkv_a, G=g_a,
                       tq=tq, tk=tabs["win"].shape[3], dv=DH, mode="full", nsteps=1)

    cum_t = cum.transpose(0, 2, 1)
    cq = cum_t[:, :, Tk - T:, None]
    ck = jnp.pad(cum_t, ((0, 0), (0, 0), (0, tk_pad - Tk)))[:, :, None, :]
    o_b = _flash(proj, U_QB, fox_kv, u_fox, fox_kv, u_fox + hkv_b, tabs["mask"], hkv=hkv_b, G=g_b,
                 tq=tq, tk=tk, dv=DH, mode=mode, nsteps=nsteps, fox=(cq, ck))

    o_c = [_flash(proj, U_QC + half * g_c, diff_kv, u_diff + half, diff_kv, u_diff + 2 * hkv_c, tabs["c"],
                  hkv=hkv_c, G=g_c, tq=tq, tk=tk, dv=2 * DH, mode=mode, nsteps=nsteps, q_stride=2, k_stride=2)
           for half in range(2)]

    nb = -(-Tk // MOBA_BLOCK)
    moba_mask = _moba_gate(proj, U_QD, moba_kv, u_moba, hkv=hkv_d, G=g_d, Tk=Tk, q_off=q_off,
                           nbp=_round_up(nb, LANE))
    o_d = _flash(proj, U_QD, moba_kv, u_moba, moba_kv, u_moba + hkv_d, tabs["d"], hkv=hkv_d, G=g_d,
                 tq=tq, tk=tk, dv=DH, mode=mode, nsteps=nsteps, mask=moba_mask, mask_blk=MOBA_BLOCK)

    lam_init = 0.8 - 0.6 * math.exp(-0.3 * layer)
    flat = lambda a: a.reshape(M, a.shape[-1])
    o = _combine(flat(o_cmp), flat(o_sel), flat(o_win), flat(proj), flat(o_b), flat(o_c[0]), flat(o_c[1]),
                 flat(o_d), prm["mix_gain"], prm["diff_lambda"], layer, tm=tiles["comb_tm"], lam_init=lam_init)

    if T % tiles["w_out"]["tm"] == 0:
        gates = dict(gate1=gt1, gate2=gt2, rows_per_batch=T)
    else:
        gates = dict(gate1=jnp.repeat(gt1[:, 0], T, axis=0), gate2=jnp.repeat(gt2[:, 0], T, axis=0), rows_per_batch=None)
    x = _matmul(o, prm["w_out"], layer, res=x, gate=gates["gate1"], rows_per_batch=gates["rows_per_batch"],
                **tiles["w_out"])
    h2 = _norm(x, prm["norm_ffn"], layer, tm=tm_n, sc=sc2, sh=sh2, rows_per_batch=T, out_dtype=BF16)
    act = _matmul(h2, prm["w_gate"], layer, w2=prm["w_up"], out_dtype=BF16, **tiles["w_ff"])
    x = _matmul(act, prm["w_down"], layer, res=x, gate=gates["gate2"], rows_per_batch=gates["rows_per_batch"],
                **tiles["w_down"])
    return x, new


def _group_tables(t5_table, dims, T, q_off, tiles, tk_pad, win_len):
    h_a, hkv_a, h_b, hkv_b, h_c, hkv_c, h_d, hkv_d = dims
    tab_t = t5_table.astype(F32).T
    tab_a, tab_c, tab_d = tab_t[:h_a], tab_t[h_a:h_a + h_c], tab_t[h_a + h_c:]
    tq, tk = tiles["tq"], tiles["tk"]
    Tk = q_off + T
    n = (Tk - CMP_LEN) // CMP_STRIDE + 2
    n_sel = -(-Tk // SEL_BLOCK)
    tabs = {"overlap": jnp.asarray(_overlap_np(n, n_sel, _round_up(n_sel, LANE))),
            "cmp": _cmp_bias(tab_a, q_off + np.arange(T), n)}
    if q_off == 0:
        assert tq == tk
        tw = tiles["win_t"]
        nband = (WINDOW - 1 + tw - 1) // tw + 1
        tabs.update(a=_prompt_bias(tab_a, tq, 3), c=_prompt_bias(tab_c, tq, 3), d=_prompt_bias(tab_d, tq, 3),
                    mask=_prompt_bias(None, tq, 3), win=_prompt_bias(tab_a, tw, min(nband, T // tw), WINDOW))
    else:
        kpos = np.arange(tk_pad)
        dec = lambda tab: _decode_bias(tab, T, kpos, q_off, tk)
        kpos_w = q_off - (win_len - T) + np.arange(win_len)
        tkw = _round_up(win_len, 8)
        kpos_w = np.concatenate([kpos_w, np.full(tkw - win_len, -1)])
        tabs.update(a=dec(tab_a), c=dec(tab_c), d=dec(tab_d), mask=dec(None),
                    win=_decode_bias(tab_a, T, kpos_w, q_off, tkw, WINDOW))
    return tabs


PROMPT_TILES = dict(
    tq=512, tk=512, win_t=256, cmp_tq=256, norm_tm=256, comb_tm=256,
    w_in=dict(tm=1024, tn=512, tk=4096), w_out=dict(tm=1024, tn=512, tk=4096),
    w_ff=dict(tm=1024, tn=256, tk=4096), w_down=dict(tm=2048, tn=512, tk=1024),
)


def _sample_tiles(M, T, tk_pad):
    tk = tk_pad
    for div in (2, 3, 4, 6):
        if tk_pad % div == 0 and (tk_pad // div) % LANE == 0 and tk_pad // div >= 4096:
            tk = tk_pad // div
    return dict(
        tq=T, tk=tk, cmp_tq=T, norm_tm=T, comb_tm=M,
        w_in=dict(tm=M, tn=512, tk=4096), w_out=dict(tm=M, tn=512, tk=4096),
        w_ff=dict(tm=M, tn=256, tk=4096), w_down=dict(tm=M, tn=1024, tk=1024),
    )


def kernel(x_prompt, x_sample, cache_nsa, state_nsa_win, cache_fox, cache_fox_logf, cache_diff, cache_moba,
           page_table, c_prompt, c_sample, t5_table, ada_w, ada_b, norm_attn, norm_ffn, w_in, w_out, mix_gain,
           nsa_cmp_pos, nsa_cmp_w1, nsa_cmp_w2, fox_fbias, diff_lambda, w_gate, w_up, w_down, final_norm):
    depth = w_in.shape[0]
    B, T, D = x_prompt.shape
    Bs, Ts, _ = x_sample.shape
    hkv_a, hkv_b, hkv_c, hkv_d = cache_nsa.shape[4], cache_fox.shape[4], cache_diff.shape[4], cache_moba.shape[4]
    h_b, h_c = fox_fbias.shape[1], diff_lambda.shape[1]
    h_a = (D // DH - h_b - 2 * h_c) // 2
    h_d = h_a
    dims = (h_a, hkv_a, h_b, hkv_b, h_c, hkv_c, h_d, hkv_d)
    page = cache_nsa.shape[2]
    n_pages = page_table.shape[1]
    past_len = n_pages * page
    tk_pad = past_len + page
    win_len = state_nsa_win.shape[2] + Ts

    p_tiles = dict(PROMPT_TILES, wbuf=state_nsa_win.shape[2])
    s_tiles = _sample_tiles(Bs * Ts, Ts, tk_pad)
    tabs_p = _group_tables(t5_table, dims, T, 0, p_tiles, T, 0)
    tabs_s = _group_tables(t5_table, dims, Ts, past_len, s_tiles, tk_pad, win_len)

    flat_cache = lambda c: c.reshape(c.shape[0], c.shape[1], c.shape[2], -1)
    past = dict(nsa=flat_cache(cache_nsa), fox=flat_cache(cache_fox), diff=flat_cache(cache_diff),
                moba=flat_cache(cache_moba),
                logf=cache_fox_logf,
                win=state_nsa_win.reshape(depth, Bs, state_nsa_win.shape[2], -1),
                pt=page_table, len=past_len, page=page)

    n_c = _round_up(B + Bs, 8)
    c_all = jnp.pad(jnp.concatenate([c_prompt, c_sample], axis=0), ((0, n_c - B - Bs), (0, 0)))
    xp, xs = x_prompt.reshape(B * T, D), x_sample.reshape(Bs * Ts, D)
    names = ("nsa", "win", "fox", "logf", "diff", "moba")
    st_p = {n: [] for n in names}
    st_s = {n: [] for n in names}
    r3 = lambda a: a.reshape(a.shape[0], 1, a.shape[1])
    for l in range(depth):
        mod = _matmul(c_all, ada_w, l, bias=ada_b[l][None], silu_in=True, tm=n_c, tn=512, tk=D)
        prm = dict(norm_attn=r3(norm_attn), norm_ffn=r3(norm_ffn), w_in_r=_reorder_w_in(w_in[l], dims), w_out=w_out,
                   mix_gain=r3(mix_gain), cmp_pos=nsa_cmp_pos, cmp_w1=nsa_cmp_w1, cmp_w2=nsa_cmp_w2,
                   fox_fbias=fox_fbias[l][None], diff_lambda=diff_lambda, w_gate=w_gate, w_up=w_up, w_down=w_down)
        xp, new_p = _layer(xp, B, T, mod[:B], l, None, prm, tabs_p, dims, p_tiles)
        xs, new_s = _layer(xs, Bs, Ts, mod[B:B + Bs], l, past, prm, tabs_s, dims, s_tiles)
        for n in names:
            st_p[n].append(new_p[n])
            st_s[n].append(new_s[n])
    y_p = _norm(xp, final_norm[None], None, tm=p_tiles["norm_tm"]).reshape(B, T, D)
    y_s = _norm(xs, final_norm[None], None, tm=s_tiles["norm_tm"]).reshape(Bs, Ts, D)

    def stack(st, name, tail):
        a = jnp.stack(st[name])
        return a.reshape(a.shape[:3] + tail)

    kv = lambda n, hk, w: (n, hk, w)
    shapes = {"nsa": kv(4, hkv_a, DH), "win": kv(2, hkv_a, DH), "fox": kv(2, hkv_b, DH), "logf": (h_b,),
              "diff": kv(2, hkv_c, 2 * DH), "moba": kv(2, hkv_d, DH)}
    out = [y_p, y_s]
    for n in names:
        out += [stack(st_p, n, shapes[n]), stack(st_s, n, shapes[n])]
    return tuple(out)
```

```python
import functools
import math

import jax
import jax.numpy as jnp
import numpy as np
from jax import lax
from jax.experimental import pallas as pl
from jax.experimental.pallas import tpu as pltpu

DH = 128
CMP_LEN = 32
CMP_STRIDE = 16
SEL_BLOCK = 64
SEL_TOP = 16
WINDOW = 512
MOBA_BLOCK = 256
MOBA_TOP = 3
N_BUCKETS = 32
MAX_DIST = 128
NEG = -1e30
FORCE = 1e4
RMS_EPS = 1e-6
BELOW_NEG = -3e38

LANE = 128
V7X_VMEM_LIMIT_BYTES = 60000 * 1024

U_QA, U_QB, U_QD, U_QC = 0, 8, 16, 24
U_NSA, U_WIN, U_FOX, U_DIFF, U_MOBA, U_AG, U_BF, U_TOTAL = 32, 40, 44, 48, 56, 60, 61, 62

BF16 = jnp.bfloat16
F32 = jnp.float32


def _cparams(*sem):
    return pltpu.CompilerParams(dimension_semantics=sem, vmem_limit_bytes=V7X_VMEM_LIMIT_BYTES)


def _silu(x):
    return x * jax.nn.sigmoid(x)


def _div_pow2(x, d):
    assert d & (d - 1) == 0
    return lax.shift_right_logical(x, jnp.int32(d.bit_length() - 1))


def _lane_src(arr, unit0, stride=1, jstride=0):
    def make_spec(rows, width, to_brhj):
        def index_map(*g):
            b, r, h, j = to_brhj(*g)
            return (b, r, (unit0 * DH) // width + h * stride + j * jstride)
        return pl.BlockSpec((None, rows, width), index_map)
    return arr, make_spec


def _mm_body(*refs, nk, tk, k_rem, silu_in, dual, has_bias, has_res):
    it = iter(refs)
    x_ref, w_ref = next(it), next(it)
    w2_ref = next(it) if dual else None
    b_ref = next(it) if has_bias else None
    r_ref, g_ref = (next(it), next(it)) if has_res else (None, None)
    o_ref, acc = next(it), next(it)
    acc2 = next(it) if dual else None
    k = pl.program_id(2)

    @pl.when(k == 0)
    def _():
        acc[...] = jnp.zeros_like(acc)
        if dual:
            acc2[...] = jnp.zeros_like(acc2)

    xv = x_ref[...]
    if silu_in:
        xv = _silu(xv.astype(F32))
    xv = xv.astype(BF16)
    wv = w_ref[...].astype(BF16)
    w2v = w2_ref[...].astype(BF16) if dual else None
    if k_rem:
        lim = jnp.where(k == nk - 1, k_rem, tk)
        xv = jnp.where(lax.broadcasted_iota(jnp.int32, xv.shape, 1) < lim, xv, jnp.zeros_like(xv))
        rows = lax.broadcasted_iota(jnp.int32, wv.shape, 0) < lim
        wv = jnp.where(rows, wv, jnp.zeros_like(wv))
        if dual:
            w2v = jnp.where(rows, w2v, jnp.zeros_like(w2v))
    acc[...] += jnp.dot(xv, wv, preferred_element_type=F32)
    if dual:
        acc2[...] += jnp.dot(xv, w2v, preferred_element_type=F32)

    @pl.when(k == nk - 1)
    def _():
        r = acc[...]
        if dual:
            r = _silu(r) * acc2[...]
        if has_bias:
            r = r + b_ref[...]
        if has_res:
            r = r_ref[...] + g_ref[...] * r
        o_ref[...] = r.astype(o_ref.dtype)


def _matmul(x, w, layer, *, tm, tn, tk, w2=None, bias=None, res=None, gate=None, rows_per_batch=None,
            silu_in=False, out_dtype=F32):
    M, K = x.shape
    N = w.shape[-1]
    assert M % tm == 0
    nk = pl.cdiv(K, tk)
    k_rem = K % tk
    grid = (M // tm, pl.cdiv(N, tn), nk)
    if layer is None:
        w_spec = pl.BlockSpec((tk, tn), lambda m, n, k: (k, n))
    else:
        w_spec = pl.BlockSpec((None, tk, tn), lambda m, n, k: (layer, k, n))
    in_specs = [pl.BlockSpec((tm, tk), lambda m, n, k: (m, k)), w_spec]
    args = [x, w]
    if w2 is not None:
        in_specs.append(w_spec)
        args.append(w2)
    if bias is not None:
        in_specs.append(pl.BlockSpec((1, tn), lambda m, n, k: (0, n)))
        args.append(bias)
    if res is not None:
        in_specs.append(pl.BlockSpec((tm, tn), lambda m, n, k: (m, n)))
        args.append(res)
        if gate.ndim == 2:
            in_specs.append(pl.BlockSpec((tm, tn), lambda m, n, k: (m, n)))
        else:
            assert rows_per_batch % tm == 0
            per = rows_per_batch // tm
            in_specs.append(pl.BlockSpec((None, 1, tn), lambda m, n, k: (m // per, 0, n)))
        args.append(gate)
    scratch = [pltpu.VMEM((tm, tn), F32)] * (2 if w2 is not None else 1)
    body = functools.partial(_mm_body, nk=nk, tk=tk, k_rem=k_rem, silu_in=silu_in, dual=w2 is not None,
                             has_bias=bias is not None, has_res=res is not None)
    return pl.pallas_call(
        body, grid=grid, in_specs=in_specs,
        out_specs=pl.BlockSpec((tm, tn), lambda m, n, k: (m, n)),
        out_shape=jax.ShapeDtypeStruct((M, N), out_dtype),
        scratch_shapes=scratch,
        compiler_params=_cparams("parallel", "parallel", "arbitrary"),
    )(*args)


def _norm_body(*refs, modulated):
    if modulated:
        x_ref, g_ref, sc_ref, sh_ref, o_ref = refs
    else:
        x_ref, g_ref, o_ref = refs
    x = x_ref[...]
    y = x * lax.rsqrt(jnp.mean(x * x, axis=-1, keepdims=True) + RMS_EPS) * g_ref[...]
    if modulated:
        y = y * (1.0 + sc_ref[...]) + sh_ref[...]
    o_ref[...] = y.astype(o_ref.dtype)


def _norm(x, g, layer, *, tm, sc=None, sh=None, rows_per_batch=None, out_dtype=F32):
    M, D = x.shape
    modulated = sc is not None
    if layer is None:
        g_spec = pl.BlockSpec((1, D), lambda m: (0, 0))
    else:
        g_spec = pl.BlockSpec((None, 1, D), lambda m: (layer, 0, 0))
    in_specs = [pl.BlockSpec((tm, D), lambda m: (m, 0)), g_spec]
    args = [x, g]
    if modulated:
        assert rows_per_batch % tm == 0
        per = rows_per_batch // tm
        mod_spec = pl.BlockSpec((None, 1, D), lambda m: (m // per, 0, 0))
        in_specs += [mod_spec, mod_spec]
        args += [sc, sh]
    return pl.pallas_call(
        functools.partial(_norm_body, modulated=modulated), grid=(M // tm,), in_specs=in_specs,
        out_specs=pl.BlockSpec((tm, D), lambda m: (m, 0)),
        out_shape=jax.ShapeDtypeStruct((M, D), out_dtype),
        compiler_params=_cparams("parallel"),
    )(*args)


def _gather_body(pt_ref, cache_ref, new_ref, o_ref, sem, *, layer, n_pages, page):
    b = pl.program_id(0)

    def page_copy(j):
        return pltpu.make_async_copy(cache_ref.at[layer, pt_ref[b, j]], o_ref.at[b, pl.ds(j * page, page)], sem)

    new_copy = pltpu.make_async_copy(new_ref.at[b], o_ref.at[b, pl.ds(n_pages * page, page)], sem)

    def start(j, carry):
        page_copy(j).start()
        return carry

    def wait(j, carry):
        page_copy(j).wait()
        return carry

    lax.fori_loop(0, n_pages, start, 0)
    new_copy.start()
    lax.fori_loop(0, n_pages, wait, 0)
    new_copy.wait()


def _gather_pages(cache, layer, page_table, new_page):
    page = cache.shape[2]
    B, n_pages = page_table.shape
    grid_spec = pltpu.PrefetchScalarGridSpec(
        num_scalar_prefetch=1, grid=(B,),
        in_specs=[pl.BlockSpec(memory_space=pl.ANY), pl.BlockSpec(memory_space=pl.ANY)],
        out_specs=pl.BlockSpec(memory_space=pl.ANY),
        scratch_shapes=[pltpu.SemaphoreType.DMA(())],
    )
    return pl.pallas_call(
        functools.partial(_gather_body, layer=layer, n_pages=n_pages, page=page), grid_spec=grid_spec,
        out_shape=jax.ShapeDtypeStruct((B, (n_pages + 1) * page) + cache.shape[3:], cache.dtype),
        compiler_params=_cparams("arbitrary"),
    )(page_table, cache, new_page)


CUM_CHUNK = 256


def _cum_body(*refs, n_past, n_new, nh):
    if n_past:
        past_ref, raw_ref, fb_ref, lf_ref, cum_ref = refs
    else:
        raw_ref, fb_ref, lf_ref, cum_ref = refs
    z = raw_ref[...][:, :nh] + fb_ref[...]
    lf = -(jnp.maximum(-z, 0.0) + jnp.log1p(jnp.exp(-jnp.abs(z))))
    lf_ref[...] = lf

    def tri(n):
        return (lax.broadcasted_iota(jnp.int32, (n, n), 0) >= lax.broadcasted_iota(jnp.int32, (n, n), 1)).astype(F32)

    def scan_rows(src, dst_off, n, carry):
        c = min(CUM_CHUNK, n)
        assert n % c == 0
        t = tri(c)

        def step(i, carry):
            r0 = pl.multiple_of(i * c, c)
            blk = jnp.dot(t, src(r0, c), preferred_element_type=F32, precision=lax.Precision.HIGHEST) + carry
            cum_ref[pl.ds(dst_off + r0, c), :] = blk
            return blk[c - 1:c, :]

        return lax.fori_loop(0, n // c, step, carry)

    carry = jnp.zeros((1, nh), F32)
    if n_past:
        carry = scan_rows(lambda r0, c: past_ref[pl.ds(r0, c), :], 0, n_past, carry)
    lf_ref[...] = lf
    scan_rows(lambda r0, c: lf_ref[pl.ds(r0, c), :], n_past, n_new, carry)


def _fox_cum(raw_f, fbias, past_lf):
    B, Tn, _ = raw_f.shape
    nh = fbias.shape[-1]
    n_past = 0 if past_lf is None else past_lf.shape[1]
    in_specs, args = [], []
    if n_past:
        in_specs.append(pl.BlockSpec((None, n_past, nh), lambda b: (b, 0, 0)))
        args.append(past_lf)
    in_specs += [pl.BlockSpec((None, Tn, LANE), lambda b: (b, 0, 0)), pl.BlockSpec((1, nh), lambda b: (0, 0))]
    args += [raw_f, fbias]
    return pl.pallas_call(
        functools.partial(_cum_body, n_past=n_past, n_new=Tn, nh=nh), grid=(B,), in_specs=in_specs,
        out_specs=[pl.BlockSpec((None, Tn, nh), lambda b: (b, 0, 0)),
                   pl.BlockSpec((None, n_past + Tn, nh), lambda b: (b, 0, 0))],
        out_shape=[jax.ShapeDtypeStruct((B, Tn, nh), F32), jax.ShapeDtypeStruct((B, n_past + Tn, nh), F32)],
        compiler_params=_cparams("parallel"),
    )(*args)


def _gelu_tanh(x):
    return 0.5 * x * (1.0 + jnp.tanh(math.sqrt(2.0 / math.pi) * (x + 0.044715 * (x * x * x))))


def _compress_body(x_ref, pos_ref, w1_ref, w2_ref, o_ref, *, n):
    hidden = w1_ref.shape[-1]
    acc_lo = jnp.zeros((n, hidden), F32)
    acc_hi = jnp.zeros((n, hidden), F32)
    for rho in range(CMP_STRIDE):
        xr = x_ref[pl.ds(rho, n, stride=CMP_STRIDE), :]
        lo = (xr + pos_ref[rho:rho + 1, :]).astype(BF16)
        hi = (xr + pos_ref[rho + CMP_STRIDE:rho + CMP_STRIDE + 1, :]).astype(BF16)
        w_lo = w1_ref[rho * DH:(rho + 1) * DH, :].astype(BF16)
        w_hi = w1_ref[(rho + CMP_STRIDE) * DH:(rho + CMP_STRIDE + 1) * DH, :].astype(BF16)
        acc_lo += jnp.dot(lo, w_lo, preferred_element_type=F32)
        acc_hi += jnp.dot(hi, w_hi, preferred_element_type=F32)
    hid = acc_lo + pltpu.roll(acc_hi, n - 1, 0)
    o_ref[...] = jnp.dot(_gelu_tanh(hid).astype(BF16), w2_ref[...].astype(BF16), preferred_element_type=F32)


def _compress(src, hkv, n, layer, pos, w1, w2):
    kv, make_spec = src
    B = kv.shape[0]
    hidden = w1.shape[-1]
    return pl.pallas_call(
        functools.partial(_compress_body, n=n), grid=(B, 2, hkv),
        in_specs=[
            make_spec(n * CMP_STRIDE, DH, lambda b, j, h: (b, 0, h, j)),
            pl.BlockSpec((None, None, CMP_LEN, DH), lambda b, j, h: (layer, j, 0, 0)),
            pl.BlockSpec((None, None, CMP_LEN * DH, hidden), lambda b, j, h: (layer, j, 0, 0)),
            pl.BlockSpec((None, None, hidden, DH), lambda b, j, h: (layer, j, 0, 0)),
        ],
        out_specs=pl.BlockSpec((None, None, None, n, DH), lambda b, j, h: (b, j, h, 0, 0)),
        out_shape=jax.ShapeDtypeStruct((B, 2, hkv, n, DH), F32),
        compiler_params=_cparams("parallel", "arbitrary", "arbitrary"),
    )(kv, pos, w1, w2)


def _take_top(score, lane_ids, count):
    sel = jnp.zeros(score.shape, F32)
    taken = []
    big = jnp.int32(score.shape[-1])
    for _ in range(count):
        m = jnp.max(score, axis=-1, keepdims=True)
        idx = jnp.min(jnp.where(score == m, lane_ids, big), axis=-1, keepdims=True)
        hit = lane_ids == idx
        sel = jnp.where(hit, 1.0, sel)
        score = jnp.where(hit, BELOW_NEG, score)
        taken.append((hit, m))
    return sel, taken


def _cmp_body(q_ref, kc_ref, vc_ref, bias_ref, ov_ref, o_ref, sel_ref, *, G, tq, q_off, n_sel, top):
    qi = pl.program_id(2)
    kc = kc_ref[...]
    vc = vc_ref[...].astype(BF16)
    scale = DH ** -0.5
    psum = None
    outs = []
    for g in range(G):
        qg = q_ref[:, g * DH:(g + 1) * DH]
        s = lax.dot_general(qg, kc, (((1,), (1,)), ((), ())), preferred_element_type=F32,
                            precision=lax.Precision.HIGHEST) * scale + bias_ref[g]
        m = jnp.max(s, axis=-1, keepdims=True)
        e = jnp.where(s > NEG / 2, jnp.exp(s - m), 0.0)
        p = e / jnp.maximum(jnp.sum(e, axis=-1, keepdims=True), 1e-30)
        outs.append(jnp.dot(p.astype(BF16), vc, preferred_element_type=F32))
        psum = p if psum is None else psum + p
    o_ref[...] = jnp.concatenate(outs, axis=1)
    imp = jnp.dot(psum, ov_ref[...], preferred_element_type=F32, precision=lax.Precision.HIGHEST)
    shape = imp.shape
    jb = lax.broadcasted_iota(jnp.int32, shape, 1)
    qpos = q_off + qi * tq + lax.broadcasted_iota(jnp.int32, shape, 0)
    cur = _div_pow2(qpos, SEL_BLOCK)
    forced = (jb == 0) | (jb == cur) | (jb == cur - 1)
    score = jnp.where(jb <= cur, imp + jnp.where(forced, FORCE, 0.0), NEG)
    score = jnp.where(jb < n_sel, score, BELOW_NEG)
    sel, _ = _take_top(score, jb, top)
    sel_ref[...] = jnp.where((jb <= cur) & (sel > 0.5), 0.0, NEG)


def _nsa_cmp(q, u_q, cmp_kv, bias, overlap, *, hkv, G, tq, q_off, n_sel):
    B, Tq, _ = q.shape
    n = cmp_kv.shape[3]
    nsp = overlap.shape[1]
    top = min(SEL_TOP, n_sel)
    body = functools.partial(_cmp_body, G=G, tq=tq, q_off=q_off, n_sel=n_sel, top=top)
    return pl.pallas_call(
        body, grid=(B, hkv, Tq // tq),
        in_specs=[
            pl.BlockSpec((None, tq, G * DH), lambda b, h, i: (b, i, u_q // G + h)),
            pl.BlockSpec((None, None, None, n, DH), lambda b, h, i: (b, 0, h, 0, 0)),
            pl.BlockSpec((None, None, None, n, DH), lambda b, h, i: (b, 1, h, 0, 0)),
            pl.BlockSpec((G, tq, n), lambda b, h, i: (h, i, 0)),
            pl.BlockSpec((n, nsp), lambda b, h, i: (0, 0)),
        ],
        out_specs=[pl.BlockSpec((None, tq, G * DH), lambda b, h, i: (b, i, h)),
                   pl.BlockSpec((None, None, tq, nsp), lambda b, h, i: (b, h, i, 0))],
        out_shape=[jax.ShapeDtypeStruct((B, Tq, hkv * G * DH), F32),
                   jax.ShapeDtypeStruct((B, hkv, Tq, nsp), F32)],
        compiler_params=_cparams("parallel", "parallel", "arbitrary"),
    )(q, cmp_kv, cmp_kv, bias, overlap)


def _moba_gate_body(q_ref, k_ref, m_ref, *, G, n_full, nbp, q_off, top):
    Tq = q_ref.shape[0]
    shape = (Tq, nbp)
    jb = lax.broadcasted_iota(jnp.int32, shape, 1)
    own = _div_pow2(q_off + lax.broadcasted_iota(jnp.int32, shape, 0), MOBA_BLOCK)
    if n_full > 0:
        kmean = jnp.sum(k_ref[...].reshape(n_full, MOBA_BLOCK, DH), axis=1) * (1.0 / MOBA_BLOCK)
        if nbp > n_full:
            kmean = jnp.concatenate([kmean, jnp.zeros((nbp - n_full, DH), F32)], axis=0)
    for g in range(G):
        mask = jnp.where(jb == own, 0.0, NEG)
        if n_full > 0:
            gate = lax.dot_general(q_ref[:, g * DH:(g + 1) * DH], kmean, (((1,), (1,)), ((), ())),
                                   preferred_element_type=F32, precision=lax.Precision.HIGHEST)
            score = jnp.where((jb < own) & (jb < n_full), gate, NEG)
            score = jnp.where(jb < n_full, score, BELOW_NEG)
            _, taken = _take_top(score, jb, top)
            for hit, val in taken:
                mask = jnp.where(hit & (val > NEG / 2), 0.0, mask)
        m_ref[g] = mask


def _moba_gate(q, u_q, ksrc, *, hkv, G, Tk, q_off, nbp):
    k, k_spec = ksrc
    B, Tq, _ = q.shape
    n_full = Tk // MOBA_BLOCK
    top = min(MOBA_TOP, n_full)
    body = functools.partial(_moba_gate_body, G=G, n_full=n_full, nbp=nbp, q_off=q_off, top=top)
    return pl.pallas_call(
        body, grid=(B, hkv),
        in_specs=[pl.BlockSpec((None, Tq, G * DH), lambda b, h: (b, 0, u_q // G + h)),
                  k_spec(max(n_full, 1) * MOBA_BLOCK, DH, lambda b, h: (b, 0, h, 0))],
        out_specs=pl.BlockSpec((None, None, G, Tq, nbp), lambda b, h: (b, h, 0, 0, 0)),
        out_shape=jax.ShapeDtypeStruct((B, hkv, G, Tq, nbp), F32),
        compiler_params=_cparams("parallel", "arbitrary"),
    )(q, k)


def _flash_body(*refs, G, Gm, tq, tk, dv, mode, nsteps, fox, mask_blk):
    it = iter(refs)
    q_ref, k_ref, v_ref, bias_ref = next(it), next(it), next(it), next(it)
    cq_ref, ck_ref = (next(it), next(it)) if fox else (None, None)
    mask_ref = next(it) if mask_blk else None
    o_ref, m_sc, acc_sc = next(it), next(it), next(it)
    qi, step = pl.program_id(2), pl.program_id(3)
    if mode == "band":
        ki = qi - (nsteps - 1) + step
        active = ki >= 0
    elif mode == "causal":
        ki = step
        active = ki <= qi
    else:
        ki = step
        active = step >= 0

    @pl.when(step == 0)
    def _():
        m_sc[...] = jnp.full(m_sc.shape, NEG, F32)
        acc_sc[...] = jnp.zeros_like(acc_sc)

    @pl.when(active)
    def _():
        k = k_ref[...].astype(BF16)
        v = jnp.concatenate([v_ref[...].astype(BF16), jnp.ones((tk, LANE), BF16)], axis=1)
        q = jnp.concatenate([q_ref[:, g * DH:(g + 1) * DH] for g in range(G)], axis=0)
        q = (q * (DH ** -0.5)).astype(BF16)
        s = lax.dot_general(q, k, (((1,), (1,)), ((), ())), preferred_element_type=F32)
        s = s.reshape(G, tq, tk) + bias_ref[...]
        if fox:
            s = s + (cq_ref[...] - ck_ref[...])
        if mask_blk:
            nbp = mask_ref.shape[-1]
            kpos = ki * tk + lax.broadcasted_iota(jnp.int32, (nbp, tk), 1)
            lo = lax.broadcasted_iota(jnp.int32, (nbp, tk), 0) * mask_blk
            expand = jnp.where((kpos >= lo) & (kpos < lo + mask_blk), 1.0, 0.0).astype(BF16)
            picked = jnp.dot(mask_ref[...].reshape(Gm * tq, nbp).astype(BF16), expand, preferred_element_type=F32)
            s = s + picked.reshape(Gm, tq, tk)
        m_old = m_sc[...]
        m_new = jnp.maximum(m_old, jnp.max(s, axis=-1, keepdims=True))
        m_use = jnp.where(m_new < NEG / 2, 0.0, m_new)
        p = jnp.exp((s - m_use).astype(BF16))
        alpha = jnp.exp(m_old - m_use)
        pv = jnp.dot(p.reshape(G * tq, tk), v, preferred_element_type=F32)
        acc_sc[...] = alpha * acc_sc[...] + pv.reshape(G, tq, dv + LANE)
        m_sc[...] = m_new

    @pl.when(step == nsteps - 1)
    def _():
        acc = acc_sc[...]
        o = acc[:, :, :dv] / jnp.maximum(acc[:, :, dv:dv + 1], 1e-30)
        for g in range(G):
            o_ref[:, g * dv:(g + 1) * dv] = o[g]


def _flash(q, u_q, ksrc, vsrc, bias, *, hkv, G, tq, tk, dv, mode, nsteps, fox=None, mask=None, mask_blk=0,
           q_stride=1):
    B, Tq, _ = q.shape
    nq = Tq // tq
    Hb, NB = bias.shape[:2]
    Gb = G if Hb > 1 else 1

    if mode == "causal":
        kidx = lambda i, s: jnp.minimum(s, i)
        bidx = lambda i, s: jnp.clip(i - s, 0, NB - 1)
    elif mode == "band":
        kidx = lambda i, s: jnp.maximum(i - (nsteps - 1) + s, 0)
        bidx = lambda i, s: jnp.clip(nsteps - 1 - s, 0, NB - 1)
    else:
        kidx = lambda i, s: s
        bidx = lambda i, s: s

    kv_index = lambda b, h, i, s: (b, kidx(i, s), h, 0)
    in_specs = [
        pl.BlockSpec((None, tq, G * DH), lambda b, h, i, s: (b, i, u_q // G + h * q_stride)),
        ksrc[1](tk, DH, kv_index),
        vsrc[1](tk, dv, kv_index),
        pl.BlockSpec((Gb, None, tq, tk), lambda b, h, i, s: (h if Hb > 1 else 0, bidx(i, s), 0, 0)),
    ]
    args = [q, ksrc[0], vsrc[0], bias]
    if fox is not None:
        in_specs += [pl.BlockSpec((None, G, tq, 1), lambda b, h, i, s: (b, h, i, 0)),
                     pl.BlockSpec((None, G, 1, tk), lambda b, h, i, s: (b, h, 0, kidx(i, s)))]
        args += list(fox)
    Gm = 0
    if mask is not None:
        Gm, nbp = mask.shape[2], mask.shape[4]
        in_specs.append(pl.BlockSpec((None, None, Gm, tq, nbp), lambda b, h, i, s: (b, h, 0, i, 0)))
        args.append(mask)
    body = functools.partial(_flash_body, G=G, Gm=Gm, tq=tq, tk=tk, dv=dv, mode=mode, nsteps=nsteps,
                             fox=fox is not None, mask_blk=mask_blk if mask is not None else 0)
    return pl.pallas_call(
        body, grid=(B, hkv, nq, nsteps), in_specs=in_specs,
        out_specs=pl.BlockSpec((None, tq, G * dv), lambda b, h, i, s: (b, i, h)),
        out_shape=jax.ShapeDtypeStruct((B, Tq, hkv * G * dv), F32),
        scratch_shapes=[pltpu.VMEM((G, tq, 1), F32), pltpu.VMEM((G, tq, dv + LANE), F32)],
        compiler_params=_cparams("parallel", "parallel", "parallel", "arbitrary"),
    )(*args)


def _head_rms(x, gain):
    return x * lax.rsqrt(jnp.mean(x * x, axis=-1, keepdims=True) + RMS_EPS) * gain


def _combine_body(cmp_ref, sel_ref, win_ref, ag_ref, b_ref, c1_ref, c2_ref, d_ref, mg_ref, lp_ref, o_ref, *,
                  h_a, h_b, h_c, h_d, lam_init):
    gates = jax.nn.sigmoid(ag_ref[...])
    col = 0
    for h in range(h_a):
        sl = slice(h * DH, (h + 1) * DH)
        o = (gates[:, 3 * h:3 * h + 1] * cmp_ref[:, sl] + gates[:, 3 * h + 1:3 * h + 2] * sel_ref[:, sl]
             + gates[:, 3 * h + 2:3 * h + 3] * win_ref[:, sl])
        o_ref[:, col:col + DH] = _head_rms(o, mg_ref[:, col:col + DH]).astype(o_ref.dtype)
        col += DH
    for h in range(h_b):
        o_ref[:, col:col + DH] = _head_rms(b_ref[:, h * DH:(h + 1) * DH], mg_ref[:, col:col + DH]).astype(o_ref.dtype)
        col += DH
    for h in range(h_c):
        lp = lp_ref[h]
        lam = (jnp.exp(jnp.sum(lp[0:1] * lp[1:2], axis=-1, keepdims=True))
               - jnp.exp(jnp.sum(lp[2:3] * lp[3:4], axis=-1, keepdims=True)) + lam_init)
        sl = slice(h * 2 * DH, (h + 1) * 2 * DH)
        o = c1_ref[:, sl] - lam * c2_ref[:, sl]
        o_ref[:, col:col + 2 * DH] = (_head_rms(o, mg_ref[:, col:col + 2 * DH]) * (1.0 - lam_init)).astype(o_ref.dtype)
        col += 2 * DH
    for h in range(h_d):
        o_ref[:, col:col + DH] = _head_rms(d_ref[:, h * DH:(h + 1) * DH], mg_ref[:, col:col + DH]).astype(o_ref.dtype)
        col += DH


def _combine(o_cmp, o_sel, o_win, proj, o_b, o_c1, o_c2, o_d, mix_gain, diff_lambda, layer, *, tm, lam_init):
    M = o_cmp.shape[0]
    D = mix_gain.shape[-1]
    h_a, h_b, h_d = o_cmp.shape[1] // DH, o_b.shape[1] // DH, o_d.shape[1] // DH
    h_c = o_c1.shape[1] // (2 * DH)
    row = lambda w: pl.BlockSpec((tm, w), lambda m: (m, 0))
    body = functools.partial(_combine_body, h_a=h_a, h_b=h_b, h_c=h_c, h_d=h_d, lam_init=lam_init)
    return pl.pallas_call(
        body, grid=(M // tm,),
        in_specs=[row(o_cmp.shape[1]), row(o_sel.shape[1]), row(o_win.shape[1]),
                  pl.BlockSpec((tm, LANE), lambda m: (m, U_AG)),
                  row(o_b.shape[1]), row(o_c1.shape[1]), row(o_c2.shape[1]), row(o_d.shape[1]),
                  pl.BlockSpec((None, 1, D), lambda m: (layer, 0, 0)),
                  pl.BlockSpec((None,) + diff_lambda.shape[1:], lambda m: (layer, 0, 0, 0))],
        out_specs=pl.BlockSpec((tm, D), lambda m: (m, 0)),
        out_shape=jax.ShapeDtypeStruct((M, D), BF16),
        compiler_params=_cparams("parallel"),
    )(o_cmp, o_sel, o_win, proj, o_b, o_c1, o_c2, o_d, mix_gain, diff_lambda)


def _t5_bucket_np(dist):
    n = np.maximum(dist, 0)
    exact = N_BUCKETS // 2
    nf = np.maximum(n, 1).astype(np.float32)
    big = exact + (np.log(nf / np.float32(exact)) / np.float32(math.log(MAX_DIST / exact))
                   * np.float32(N_BUCKETS - exact)).astype(np.int32)
    return np.where(n < exact, n, np.minimum(big, N_BUCKETS - 1)).astype(np.int32)


def _bias_from_dist(tab_t, dist, valid):
    if tab_t is None:
        return jnp.asarray(np.where(valid, 0.0, NEG).astype(np.float32))[None]
    idx = _t5_bucket_np(dist).reshape(-1)
    b = jnp.take(tab_t, jnp.asarray(idx), axis=1).reshape((tab_t.shape[0],) + dist.shape)
    return jnp.where(jnp.asarray(valid)[None], b, NEG)


def _dist_table(tab_t):
    if tab_t is None:
        return jnp.zeros((1, MAX_DIST + 1), F32)
    return jnp.take(tab_t, jnp.asarray(_t5_bucket_np(np.arange(MAX_DIST + 1))), axis=1)


def _bias_range(bd, lo, hi, window=None):
    H, nd = bd.shape
    far = 10 ** 9
    w = far if window is None else window
    assert w >= nd - 1
    parts = []
    for seg_lo, seg_hi, kind in ((-far, 0, "neg"), (0, nd - 1, "tab"), (nd - 1, w, "far"), (w, far, "neg")):
        a, b = max(lo, seg_lo), min(hi, seg_hi)
        if b <= a:
            continue
        if kind == "neg":
            parts.append(jnp.full((H, b - a), NEG, F32))
        elif kind == "tab":
            parts.append(bd[:, a:b])
        else:
            parts.append(jnp.broadcast_to(bd[:, nd - 1:], (H, b - a)))
    return jnp.concatenate(parts, axis=1)


def _prompt_bias(bd, t, n_var, window=None):
    H = bd.shape[0]
    tiles = []
    for v in range(n_var):
        vec = jnp.concatenate([_bias_range(bd, v * t - t, v * t + 1, window)[:, ::-1],
                               _bias_range(bd, v * t + 1, v * t + t, window)[:, ::-1]], axis=1)
        flat = jnp.broadcast_to(vec[:, None, :], (H, t, 2 * t)).reshape(H, 2 * t * t)
        tiles.append(flat[:, :t * (2 * t - 1)].reshape(H, t, 2 * t - 1)[:, :, :t])
    return jnp.stack(tiles, axis=1)


def _decode_bias(bd, tq, q_rel, n_keys, tk, window=None):
    H = bd.shape[0]
    vec = _bias_range(bd, q_rel - n_keys + 1, q_rel + tq, window)[:, ::-1]
    rows = jnp.stack([vec[:, tq - 1 - i:tq - 1 - i + n_keys] for i in range(tq)], axis=1)
    return rows.reshape(H, tq, n_keys // tk, tk).transpose(0, 2, 1, 3)


def _cmp_bias_prompt(bd, T, n):
    H = bd.shape[0]
    L = T + CMP_STRIDE * n
    vec = jnp.concatenate([_bias_range(bd, 1 - CMP_LEN, T + 1 - CMP_LEN), jnp.full((H, L - T), NEG, F32)], axis=1)
    flat = jnp.broadcast_to(vec[:, None, :], (H, n, L)).reshape(H, n * L)
    rows = flat[:, :n * (L - CMP_STRIDE)].reshape(H, n, L - CMP_STRIDE)[:, :, :T]
    rows = jnp.where(jnp.asarray(np.arange(n) < n - 1)[None, :, None], rows, NEG)
    return rows.transpose(0, 2, 1)


def _cmp_bias(tab_t, q_pos, n):
    end = np.arange(n) * CMP_STRIDE + CMP_LEN - 1
    dist = q_pos[:, None] - end[None, :]
    valid = (dist >= 0) & (np.arange(n)[None, :] < n - 1)
    return _bias_from_dist(tab_t, dist, valid)


def _overlap_np(n, n_sel, n_sel_pad):
    cs = np.arange(n)[:, None] * CMP_STRIDE
    js = np.arange(n_sel_pad)[None, :] * SEL_BLOCK
    ov = (cs < js + SEL_BLOCK) & (cs + CMP_LEN > js) & (np.arange(n)[:, None] < n - 1) & (np.arange(n_sel_pad)[None, :] < n_sel)
    return ov.astype(np.float32)


def _round_up(x, m):
    return -(-x // m) * m


def _reorder_w_in(w, dims):
    h_a, hkv_a, h_b, hkv_b, h_c, hkv_c, h_d, hkv_d = dims
    widths = [h_a * DH] + [hkv_a * DH] * 6 + [h_a * 3, h_b * DH, hkv_b * DH, hkv_b * DH, h_b,
                                             h_c * 2 * DH, hkv_c * 2 * DH, hkv_c * 2 * DH, h_d * DH, hkv_d * DH, hkv_d * DH]
    offs = np.concatenate([[0], np.cumsum(widths)])
    seg = lambda i: w[:, offs[i]:offs[i + 1]]
    D = w.shape[0]
    g_c = h_c // hkv_c
    c_q = seg(12).reshape(D, hkv_c, g_c, 2, DH).transpose(0, 1, 3, 2, 4).reshape(D, -1)
    pad = lambda a: jnp.pad(a, ((0, 0), (0, LANE - a.shape[1])))
    return jnp.concatenate([seg(0), seg(8), seg(15), c_q, seg(1), seg(2), seg(3), seg(4), seg(5), seg(6),
                            seg(9), seg(10), seg(13), seg(14), seg(16), seg(17), pad(seg(7)), pad(seg(11))], axis=1)


def _layer(x, B, T, mod, layer, past, prm, tabs, dims, tiles):
    h_a, hkv_a, h_b, hkv_b, h_c, hkv_c, h_d, hkv_d = dims
    g_a, g_b, g_c, g_d = h_a // hkv_a, h_b // hkv_b, h_c // hkv_c, h_d // hkv_d
    D = x.shape[1]
    M = B * T
    sh1, sc1, gt1, sh2, sc2, gt2 = [m[:, None, :] for m in jnp.split(mod, 6, axis=-1)]
    tm_n = tiles["norm_tm"]
    h = _norm(x, prm["norm_attn"], layer, tm=tm_n, sc=sc1, sh=sh1, rows_per_batch=T, out_dtype=BF16)
    proj = _matmul(h, prm["w_in_r"], None, **tiles["w_in"]).reshape(B, T, U_TOTAL * LANE)
    new = {
        "nsa": proj[:, :, U_NSA * LANE:U_WIN * LANE], "fox": proj[:, :, U_FOX * LANE:U_DIFF * LANE],
        "diff": proj[:, :, U_DIFF * LANE:U_MOBA * LANE], "moba": proj[:, :, U_MOBA * LANE:U_AG * LANE],
    }
    kw_new = proj[:, :, U_WIN * LANE:U_FOX * LANE]
    tq, tk = tiles["tq"], tiles["tk"]

    if past is None:
        q_off, Tk = 0, T
        src = dict(
            cmp=_lane_src(proj, U_NSA, jstride=hkv_a),
            sel_k=_lane_src(proj, U_NSA + 2 * hkv_a), sel_v=_lane_src(proj, U_NSA + 3 * hkv_a),
            win_k=_lane_src(proj, U_WIN), win_v=_lane_src(proj, U_WIN + hkv_a),
            fox_k=_lane_src(proj, U_FOX), fox_v=_lane_src(proj, U_FOX + hkv_b),
            diff_k=[_lane_src(proj, U_DIFF + half, stride=2) for half in range(2)],
            diff_v=_lane_src(proj, U_DIFF + 2 * hkv_c),
            moba_k=_lane_src(proj, U_MOBA), moba_v=_lane_src(proj, U_MOBA + hkv_d))
        assert T >= tiles["wbuf"]
        new["win"] = kw_new[:, T - tiles["wbuf"]:]
        lf_new, cum = _fox_cum(proj[:, :, U_BF * LANE:], prm["fox_fbias"], None)
        mode, nsteps = "causal", T // tk
        band_steps = tabs["win"].shape[1]
        n_cmp_pad = T // CMP_STRIDE
        tk_pad = T
    else:
        q_off = past["len"]
        Tk = q_off + T
        page = past["page"]

        def full(name):
            new_page = jnp.pad(new[name], ((0, 0), (0, page - T), (0, 0)))
            return _gather_pages(past[name], layer, past["pt"], new_page)

        nsa_kv, fox_kv, diff_kv, moba_kv = full("nsa"), full("fox"), full("diff"), full("moba")
        past_lf = _gather_pages(past["logf"], layer, past["pt"], jnp.zeros((B, page, h_b), F32))[:, :q_off]
        win_kv = jnp.concatenate([past["win"][layer], kw_new], axis=1)
        wbuf = past["win"].shape[2]
        new["win"] = win_kv[:, -wbuf:]
        src = dict(
            cmp=_lane_src(nsa_kv, 0, jstride=hkv_a),
            sel_k=_lane_src(nsa_kv, 2 * hkv_a), sel_v=_lane_src(nsa_kv, 3 * hkv_a),
            win_k=_lane_src(win_kv, 0), win_v=_lane_src(win_kv, hkv_a),
            fox_k=_lane_src(fox_kv, 0), fox_v=_lane_src(fox_kv, hkv_b),
            diff_k=[_lane_src(diff_kv, half, stride=2) for half in range(2)], diff_v=_lane_src(diff_kv, 2 * hkv_c),
            moba_k=_lane_src(moba_kv, 0), moba_v=_lane_src(moba_kv, hkv_d))
        lf_new, cum = _fox_cum(proj[:, :, U_BF * LANE:], prm["fox_fbias"], past_lf)
        mode, nsteps = "full", nsa_kv.shape[1] // tk
        band_steps = 1
        n_cmp_pad = q_off // CMP_STRIDE
        tk_pad = nsa_kv.shape[1]
    new["logf"] = lf_new

    cmp_kv = _compress(src["cmp"], hkv_a, n_cmp_pad, layer, prm["cmp_pos"], prm["cmp_w1"], prm["cmp_w2"])
    n_sel = -(-Tk // SEL_BLOCK)
    o_cmp, sel_mask = _nsa_cmp(proj, U_QA, cmp_kv, tabs["cmp"], tabs["overlap"], hkv=hkv_a, G=g_a,
                               tq=min(tiles["cmp_tq"], T), q_off=q_off, n_sel=n_sel)
    o_sel = _flash(proj, U_QA, src["sel_k"], src["sel_v"], tabs["a"], hkv=hkv_a, G=g_a,
                   tq=tq, tk=tk, dv=DH, mode=mode, nsteps=nsteps, mask=sel_mask[:, :, None], mask_blk=SEL_BLOCK)
    if past is None:
        o_win = _flash(proj, U_QA, src["win_k"], src["win_v"], tabs["win"], hkv=hkv_a, G=g_a,
                       tq=tabs["win"].shape[2], tk=tabs["win"].shape[3], dv=DH, mode="band", nsteps=band_steps)
    else:
        o_win = _flash(proj, U_QA, src["win_k"], src["win_v"], tabs["win"], hkv=hkv_a, G=g_a,
                       tq=tq, tk=tabs["win"].shape[3], dv=DH, mode="full", nsteps=1)

    cum_t = cum.transpose(0, 2, 1)
    cq = cum_t[:, :, Tk - T:, None]
    ck = jnp.pad(cum_t, ((0, 0), (0, 0), (0, tk_pad - Tk)))[:, :, None, :]
    o_b = _flash(proj, U_QB, src["fox_k"], src["fox_v"], tabs["mask"], hkv=hkv_b, G=g_b,
                 tq=tq, tk=tk, dv=DH, mode=mode, nsteps=nsteps, fox=(cq, ck))

    o_c = [_flash(proj, U_QC + half * g_c, src["diff_k"][half], src["diff_v"], tabs["c"],
                  hkv=hkv_c, G=g_c, tq=tq, tk=tk, dv=2 * DH, mode=mode, nsteps=nsteps, q_stride=2)
           for half in range(2)]

    nb = -(-Tk // MOBA_BLOCK)
    moba_mask = _moba_gate(proj, U_QD, src["moba_k"], hkv=hkv_d, G=g_d, Tk=Tk, q_off=q_off,
                           nbp=_round_up(nb, LANE))
    o_d = _flash(proj, U_QD, src["moba_k"], src["moba_v"], tabs["d"], hkv=hkv_d, G=g_d,
                 tq=tq, tk=tk, dv=DH, mode=mode, nsteps=nsteps, mask=moba_mask, mask_blk=MOBA_BLOCK)

    lam_init = 0.8 - 0.6 * math.exp(-0.3 * layer)
    flat = lambda a: a.reshape(M, a.shape[-1])
    o = _combine(flat(o_cmp), flat(o_sel), flat(o_win), flat(proj), flat(o_b), flat(o_c[0]), flat(o_c[1]),
                 flat(o_d), prm["mix_gain"], prm["diff_lambda"], layer, tm=tiles["comb_tm"], lam_init=lam_init)

    if T % tiles["w_out"]["tm"] == 0:
        gates = dict(gate1=gt1, gate2=gt2, rows_per_batch=T)
    else:
        gates = dict(gate1=jnp.repeat(gt1[:, 0], T, axis=0), gate2=jnp.repeat(gt2[:, 0], T, axis=0), rows_per_batch=None)
    x = _matmul(o, prm["w_out"], layer, res=x, gate=gates["gate1"], rows_per_batch=gates["rows_per_batch"],
                **tiles["w_out"])
    h2 = _norm(x, prm["norm_ffn"], layer, tm=tm_n, sc=sc2, sh=sh2, rows_per_batch=T, out_dtype=BF16)
    act = _matmul(h2, prm["w_gate"], layer, w2=prm["w_up"], out_dtype=BF16, **tiles["w_ff"])
    x = _matmul(act, prm["w_down"], layer, res=x, gate=gates["gate2"], rows_per_batch=gates["rows_per_batch"],
                **tiles["w_down"])
    return x, new


def _group_tables(t5_table, dims, T, q_off, tiles, tk_pad, win_len):
    h_a, hkv_a, h_b, hkv_b, h_c, hkv_c, h_d, hkv_d = dims
    tab_t = t5_table.astype(F32).T
    tab_a, tab_c, tab_d = tab_t[:h_a], tab_t[h_a:h_a + h_c], tab_t[h_a + h_c:]
    tq, tk = tiles["tq"], tiles["tk"]
    Tk = q_off + T
    n = (Tk - CMP_LEN) // CMP_STRIDE + 2
    n_sel = -(-Tk // SEL_BLOCK)
    tabs = {"overlap": jnp.asarray(_overlap_np(n, n_sel, _round_up(n_sel, LANE)))}
    bd_a, bd_c, bd_d, bd_0 = _dist_table(tab_a), _dist_table(tab_c), _dist_table(tab_d), _dist_table(None)
    if q_off == 0:
        assert tq == tk
        tw = tiles["win_t"]
        nband = (WINDOW - 1 + tw - 1) // tw + 1
        tabs.update(cmp=_cmp_bias_prompt(bd_a, T, n),
                    a=_prompt_bias(bd_a, tq, 3), c=_prompt_bias(bd_c, tq, 3), d=_prompt_bias(bd_d, tq, 3),
                    mask=_prompt_bias(bd_0, tq, 3), win=_prompt_bias(bd_a, tw, min(nband, T // tw), WINDOW))
    else:
        assert win_len % 8 == 0 and q_off >= win_len - T
        dec = lambda bd: _decode_bias(bd, T, q_off, tk_pad, tk)
        tabs.update(cmp=_cmp_bias(tab_a, q_off + np.arange(T), n),
                    a=dec(bd_a), c=dec(bd_c), d=dec(bd_d), mask=dec(bd_0),
                    win=_decode_bias(bd_a, T, win_len - T, win_len, win_len, WINDOW))
    return tabs


PROMPT_TILES = dict(
    tq=512, tk=512, win_t=512, cmp_tq=512, norm_tm=256, comb_tm=256,
    w_in=dict(tm=1024, tn=512, tk=4096), w_out=dict(tm=1024, tn=512, tk=4096),
    w_ff=dict(tm=1024, tn=256, tk=4096), w_down=dict(tm=2048, tn=512, tk=1024),
)


def _sample_tiles(M, T, tk_pad):
    tk = tk_pad
    for div in (2, 3, 4, 6):
        if tk_pad % div == 0 and (tk_pad // div) % LANE == 0 and tk_pad // div >= 4096:
            tk = tk_pad // div
    return dict(
        tq=T, tk=tk, cmp_tq=T, norm_tm=T, comb_tm=M,
        w_in=dict(tm=M, tn=512, tk=4096), w_out=dict(tm=M, tn=512, tk=4096),
        w_ff=dict(tm=M, tn=256, tk=4096), w_down=dict(tm=M, tn=1024, tk=1024),
    )


def kernel(x_prompt, x_sample, cache_nsa, state_nsa_win, cache_fox, cache_fox_logf, cache_diff, cache_moba,
           page_table, c_prompt, c_sample, t5_table, ada_w, ada_b, norm_attn, norm_ffn, w_in, w_out, mix_gain,
           nsa_cmp_pos, nsa_cmp_w1, nsa_cmp_w2, fox_fbias, diff_lambda, w_gate, w_up, w_down, final_norm):
    depth = w_in.shape[0]
    B, T, D = x_prompt.shape
    Bs, Ts, _ = x_sample.shape
    hkv_a, hkv_b, hkv_c, hkv_d = cache_nsa.shape[4], cache_fox.shape[4], cache_diff.shape[4], cache_moba.shape[4]
    h_b, h_c = fox_fbias.shape[1], diff_lambda.shape[1]
    h_a = (D // DH - h_b - 2 * h_c) // 2
    h_d = h_a
    dims = (h_a, hkv_a, h_b, hkv_b, h_c, hkv_c, h_d, hkv_d)
    page = cache_nsa.shape[2]
    n_pages = page_table.shape[1]
    past_len = n_pages * page
    tk_pad = past_len + page
    win_len = state_nsa_win.shape[2] + Ts

    p_tiles = dict(PROMPT_TILES, wbuf=state_nsa_win.shape[2])
    s_tiles = _sample_tiles(Bs * Ts, Ts, tk_pad)
    tabs_p = _group_tables(t5_table, dims, T, 0, p_tiles, T, 0)
    tabs_s = _group_tables(t5_table, dims, Ts, past_len, s_tiles, tk_pad, win_len)

    flat_cache = lambda c: c.reshape(c.shape[0], c.shape[1], c.shape[2], -1)
    past = dict(nsa=flat_cache(cache_nsa), fox=flat_cache(cache_fox), diff=flat_cache(cache_diff),
                moba=flat_cache(cache_moba), logf=cache_fox_logf,
                win=state_nsa_win.reshape(depth, Bs, state_nsa_win.shape[2], -1),
                pt=page_table, len=past_len, page=page)

    n_c = _round_up(B + Bs, 8)
    c_all = jnp.pad(jnp.concatenate([c_prompt, c_sample], axis=0), ((0, n_c - B - Bs), (0, 0)))
    xp, xs = x_prompt.reshape(B * T, D), x_sample.reshape(Bs * Ts, D)
    names = ("nsa", "win", "fox", "logf", "diff", "moba")
    st_p = {n: [] for n in names}
    st_s = {n: [] for n in names}
    r3 = lambda a: a.reshape(a.shape[0], 1, a.shape[1])
    for l in range(depth):
        mod = _matmul(c_all, ada_w, l, bias=ada_b[l][None], silu_in=True, tm=n_c, tn=512, tk=D)
        prm = dict(norm_attn=r3(norm_attn), norm_ffn=r3(norm_ffn), w_in_r=_reorder_w_in(w_in[l], dims), w_out=w_out,
                   mix_gain=r3(mix_gain), cmp_pos=nsa_cmp_pos, cmp_w1=nsa_cmp_w1, cmp_w2=nsa_cmp_w2,
                   fox_fbias=fox_fbias[l][None], diff_lambda=diff_lambda, w_gate=w_gate, w_up=w_up, w_down=w_down)
        xp, new_p = _layer(xp, B, T, mod[:B], l, None, prm, tabs_p, dims, p_tiles)
        xs, new_s = _layer(xs, Bs, Ts, mod[B:B + Bs], l, past, prm, tabs_s, dims, s_tiles)
        for n in names:
            st_p[n].append(new_p[n])
            st_s[n].append(new_s[n])
    y_p = _norm(xp, final_norm[None], None, tm=p_tiles["norm_tm"]).reshape(B, T, D)
    y_s = _norm(xs, final_norm[None], None, tm=s_tiles["norm_tm"]).reshape(Bs, Ts, D)

    def stack(st, name, tail):
        a = jnp.stack(st[name])
        return a.reshape(a.shape[:3] + tail)

    kv = lambda n, hk, w: (n, hk, w)
    shapes = {"nsa": kv(4, hkv_a, DH), "win": kv(2, hkv_a, DH), "fox": kv(2, hkv_b, DH), "logf": (h_b,),
              "diff": kv(2, hkv_c, 2 * DH), "moba": kv(2, hkv_d, DH)}
    out = [y_p, y_s]
    for n in names:
        out += [stack(st_p, n, shapes[n]), stack(st_s, n, shapes[n])]
    return tuple(out)
```

```python
import functools
import math

import jax
import jax.numpy as jnp
import numpy as np
from jax import lax
from jax.experimental import pallas as pl
from jax.experimental.pallas import tpu as pltpu

DH = 128
CMP_LEN = 32
CMP_STRIDE = 16
SEL_BLOCK = 64
SEL_TOP = 16
WINDOW = 512
MOBA_BLOCK = 256
MOBA_TOP = 3
N_BUCKETS = 32
MAX_DIST = 128
NEG = -1e30
FORCE = 1e4
RMS_EPS = 1e-6
BELOW_NEG = -3e38

LANE = 128
V7X_VMEM_LIMIT_BYTES = 60000 * 1024

U_QA, U_QB, U_QD, U_QC = 0, 8, 16, 24
U_NSA, U_WIN, U_FOX, U_DIFF, U_MOBA, U_AG, U_BF, U_TOTAL = 32, 40, 44, 48, 56, 60, 61, 62

BF16 = jnp.bfloat16
F32 = jnp.float32


def _cparams(*sem):
    return pltpu.CompilerParams(dimension_semantics=sem, vmem_limit_bytes=V7X_VMEM_LIMIT_BYTES)


def _silu(x):
    return x * jax.nn.sigmoid(x)


def _div_pow2(x, d):
    assert d & (d - 1) == 0
    return lax.shift_right_logical(x, jnp.int32(d.bit_length() - 1))


def _lane_src(arr, unit0, stride=1, jstride=0):
    def make_spec(rows, width, to_brhj):
        def index_map(*g):
            b, r, h, j = to_brhj(*g)
            return (b, r, (unit0 * DH) // width + h * stride + j * jstride)
        return pl.BlockSpec((None, rows, width), index_map)
    return arr, make_spec


def _mm_body(*refs, nk, tk, k_rem, silu_in, dual, has_bias, has_res):
    it = iter(refs)
    x_ref, w_ref = next(it), next(it)
    w2_ref = next(it) if dual else None
    b_ref = next(it) if has_bias else None
    r_ref, g_ref = (next(it), next(it)) if has_res else (None, None)
    o_ref, acc = next(it), next(it)
    acc2 = next(it) if dual else None
    k = pl.program_id(2)

    @pl.when(k == 0)
    def _():
        acc[...] = jnp.zeros_like(acc)
        if dual:
            acc2[...] = jnp.zeros_like(acc2)

    def accumulate(overhang):
        xv = x_ref[...]
        if silu_in:
            xv = _silu(xv.astype(F32))
        xv = xv.astype(BF16)
        wv = w_ref[...].astype(BF16)
        w2v = w2_ref[...].astype(BF16) if dual else None
        if overhang:
            xv = jnp.where(lax.broadcasted_iota(jnp.int32, xv.shape, 1) < k_rem, xv, jnp.zeros_like(xv))
            rows = lax.broadcasted_iota(jnp.int32, wv.shape, 0) < k_rem
            wv = jnp.where(rows, wv, jnp.zeros_like(wv))
            if dual:
                w2v = jnp.where(rows, w2v, jnp.zeros_like(w2v))
        acc[...] += jnp.dot(xv, wv, preferred_element_type=F32)
        if dual:
            acc2[...] += jnp.dot(xv, w2v, preferred_element_type=F32)

    if k_rem:
        pl.when(k < nk - 1)(lambda: accumulate(False))
        pl.when(k == nk - 1)(lambda: accumulate(True))
    else:
        accumulate(False)

    @pl.when(k == nk - 1)
    def _():
        r = acc[...]
        if dual:
            r = _silu(r) * acc2[...]
        if has_bias:
            r = r + b_ref[...]
        if has_res:
            r = r_ref[...] + g_ref[...] * r
        o_ref[...] = r.astype(o_ref.dtype)


def _matmul(x, w, layer, *, tm, tn, tk, w2=None, bias=None, res=None, gate=None, rows_per_batch=None,
            silu_in=False, out_dtype=F32):
    M, K = x.shape
    N = w.shape[-1]
    assert M % tm == 0
    nk = pl.cdiv(K, tk)
    k_rem = K % tk
    grid = (M // tm, pl.cdiv(N, tn), nk)
    if layer is None:
        w_spec = pl.BlockSpec((tk, tn), lambda m, n, k: (k, n))
    else:
        w_spec = pl.BlockSpec((None, tk, tn), lambda m, n, k: (layer, k, n))
    in_specs = [pl.BlockSpec((tm, tk), lambda m, n, k: (m, k)), w_spec]
    args = [x, w]
    if w2 is not None:
        in_specs.append(w_spec)
        args.append(w2)
    if bias is not None:
        in_specs.append(pl.BlockSpec((1, tn), lambda m, n, k: (0, n)))
        args.append(bias)
    if res is not None:
        in_specs.append(pl.BlockSpec((tm, tn), lambda m, n, k: (m, n)))
        args.append(res)
        if gate.ndim == 2:
            in_specs.append(pl.BlockSpec((tm, tn), lambda m, n, k: (m, n)))
        else:
            assert rows_per_batch % tm == 0
            per = rows_per_batch // tm
            in_specs.append(pl.BlockSpec((None, 1, tn), lambda m, n, k: (m // per, 0, n)))
        args.append(gate)
    scratch = [pltpu.VMEM((tm, tn), F32)] * (2 if w2 is not None else 1)
    body = functools.partial(_mm_body, nk=nk, tk=tk, k_rem=k_rem, silu_in=silu_in, dual=w2 is not None,
                             has_bias=bias is not None, has_res=res is not None)
    return pl.pallas_call(
        body, grid=grid, in_specs=in_specs,
        out_specs=pl.BlockSpec((tm, tn), lambda m, n, k: (m, n)),
        out_shape=jax.ShapeDtypeStruct((M, N), out_dtype),
        scratch_shapes=scratch,
        compiler_params=_cparams("parallel", "parallel", "arbitrary"),
    )(*args)


def _norm_body(*refs, modulated):
    if modulated:
        x_ref, g_ref, sc_ref, sh_ref, o_ref = refs
    else:
        x_ref, g_ref, o_ref = refs
    x = x_ref[...]
    y = x * lax.rsqrt(jnp.mean(x * x, axis=-1, keepdims=True) + RMS_EPS) * g_ref[...]
    if modulated:
        y = y * (1.0 + sc_ref[...]) + sh_ref[...]
    o_ref[...] = y.astype(o_ref.dtype)


def _norm(x, g, layer, *, tm, sc=None, sh=None, rows_per_batch=None, out_dtype=F32):
    M, D = x.shape
    modulated = sc is not None
    if layer is None:
        g_spec = pl.BlockSpec((1, D), lambda m: (0, 0))
    else:
        g_spec = pl.BlockSpec((None, 1, D), lambda m: (layer, 0, 0))
    in_specs = [pl.BlockSpec((tm, D), lambda m: (m, 0)), g_spec]
    args = [x, g]
    if modulated:
        assert rows_per_batch % tm == 0
        per = rows_per_batch // tm
        mod_spec = pl.BlockSpec((None, 1, D), lambda m: (m // per, 0, 0))
        in_specs += [mod_spec, mod_spec]
        args += [sc, sh]
    return pl.pallas_call(
        functools.partial(_norm_body, modulated=modulated), grid=(M // tm,), in_specs=in_specs,
        out_specs=pl.BlockSpec((tm, D), lambda m: (m, 0)),
        out_shape=jax.ShapeDtypeStruct((M, D), out_dtype),
        compiler_params=_cparams("parallel"),
    )(*args)


GATHER_PAGES_PER_STEP = 8


def _gather_body(pt_ref, *refs, n_groups, per, page, slots):
    cache_refs, new_ref, o_ref = refs[:per], refs[per], refs[per + 1]
    j = pl.program_id(1)

    @pl.when(j < n_groups)
    def _():
        for p in range(per):
            if slots is None:
                o_ref[p * page:(p + 1) * page] = cache_refs[p][...]
            else:
                for s, src_slot in enumerate(slots):
                    o_ref[p * page:(p + 1) * page, s * LANE:(s + 1) * LANE] = (
                        cache_refs[p][pl.ds(src_slot, page, stride=len(slots)), :])

    @pl.when(j == n_groups)
    def _():
        o_ref[0:page] = new_ref[...]
        if per > 1:
            o_ref[page:per * page] = jnp.zeros(((per - 1) * page,) + o_ref.shape[1:], o_ref.dtype)


def _gather_pages(cache, layer, page_table, new_page, slots=None):
    page, W = new_page.shape[1:]
    B, n_pages = page_table.shape
    per = math.gcd(GATHER_PAGES_PER_STEP, n_pages)
    n_groups = n_pages // per
    assert cache.shape[2:] == ((page, W) if slots is None else (page * len(slots), LANE))

    def page_spec(p):
        return pl.BlockSpec((None, None) + cache.shape[2:],
                            lambda b, j, pt: (layer, pt[b, jnp.minimum(j, n_groups - 1) * per + p], 0, 0))

    grid_spec = pltpu.PrefetchScalarGridSpec(
        num_scalar_prefetch=1, grid=(B, n_groups + 1),
        in_specs=[page_spec(p) for p in range(per)] + [pl.BlockSpec((None, page, W), lambda b, j, pt: (b, 0, 0))],
        out_specs=pl.BlockSpec((None, per * page, W), lambda b, j, pt: (b, j, 0)),
    )
    return pl.pallas_call(
        functools.partial(_gather_body, n_groups=n_groups, per=per, page=page, slots=slots), grid_spec=grid_spec,
        out_shape=jax.ShapeDtypeStruct((B, (n_pages + 1) * page, W), cache.dtype),
        compiler_params=_cparams("parallel", "arbitrary"),
    )(page_table, *([cache] * per), new_page)


CUM_CHUNK = 256


def _cum_body(*refs, n_past, n_new, nh):
    if n_past:
        past_ref, raw_ref, fb_ref, lf_ref, cum_ref = refs
    else:
        raw_ref, fb_ref, lf_ref, cum_ref = refs
    z = raw_ref[...][:, :nh] + fb_ref[...]
    lf = -(jnp.maximum(-z, 0.0) + jnp.log1p(jnp.exp(-jnp.abs(z))))
    lf_ref[...] = lf

    def tri(n):
        return (lax.broadcasted_iota(jnp.int32, (n, n), 0) >= lax.broadcasted_iota(jnp.int32, (n, n), 1)).astype(F32)

    def scan_rows(src, dst_off, n, carry):
        c = min(CUM_CHUNK, n)
        assert n % c == 0
        t = tri(c)

        def step(i, carry):
            r0 = pl.multiple_of(i * c, c)
            blk = jnp.dot(t, src(r0, c), preferred_element_type=F32, precision=lax.Precision.HIGHEST) + carry
            cum_ref[pl.ds(dst_off + r0, c), :] = blk
            return blk[c - 1:c, :]

        return lax.fori_loop(0, n // c, step, carry)

    carry = jnp.zeros((1, nh), F32)
    if n_past:
        carry = scan_rows(lambda r0, c: past_ref[pl.ds(r0, c), :], 0, n_past, carry)
    lf_ref[...] = lf
    scan_rows(lambda r0, c: lf_ref[pl.ds(r0, c), :], n_past, n_new, carry)


def _fox_cum(raw_f, fbias, past_lf):
    B, Tn, _ = raw_f.shape
    nh = fbias.shape[-1]
    n_past = 0 if past_lf is None else past_lf.shape[1]
    in_specs, args = [], []
    if n_past:
        in_specs.append(pl.BlockSpec((None, n_past, nh), lambda b: (b, 0, 0)))
        args.append(past_lf)
    in_specs += [pl.BlockSpec((None, Tn, LANE), lambda b: (b, 0, 0)), pl.BlockSpec((1, nh), lambda b: (0, 0))]
    args += [raw_f, fbias]
    return pl.pallas_call(
        functools.partial(_cum_body, n_past=n_past, n_new=Tn, nh=nh), grid=(B,), in_specs=in_specs,
        out_specs=[pl.BlockSpec((None, Tn, nh), lambda b: (b, 0, 0)),
                   pl.BlockSpec((None, n_past + Tn, nh), lambda b: (b, 0, 0))],
        out_shape=[jax.ShapeDtypeStruct((B, Tn, nh), F32), jax.ShapeDtypeStruct((B, n_past + Tn, nh), F32)],
        compiler_params=_cparams("parallel"),
    )(*args)


def _gelu_tanh(x):
    return 0.5 * x * (1.0 + jnp.tanh(math.sqrt(2.0 / math.pi) * (x + 0.044715 * (x * x * x))))


def _compress_body(x_ref, pos_ref, w1_ref, w2_ref, o_ref, *, n):
    hidden = w1_ref.shape[-1]
    acc_lo = jnp.zeros((n, hidden), F32)
    acc_hi = jnp.zeros((n, hidden), F32)
    for rho in range(CMP_STRIDE):
        xr = x_ref[pl.ds(rho, n, stride=CMP_STRIDE), :]
        lo = (xr + pos_ref[rho:rho + 1, :]).astype(BF16)
        hi = (xr + pos_ref[rho + CMP_STRIDE:rho + CMP_STRIDE + 1, :]).astype(BF16)
        w_lo = w1_ref[rho * DH:(rho + 1) * DH, :].astype(BF16)
        w_hi = w1_ref[(rho + CMP_STRIDE) * DH:(rho + CMP_STRIDE + 1) * DH, :].astype(BF16)
        acc_lo += jnp.dot(lo, w_lo, preferred_element_type=F32)
        acc_hi += jnp.dot(hi, w_hi, preferred_element_type=F32)
    hid = acc_lo + pltpu.roll(acc_hi, n - 1, 0)
    o_ref[...] = jnp.dot(_gelu_tanh(hid).astype(BF16), w2_ref[...].astype(BF16), preferred_element_type=F32)


def _compress(src, hkv, n, layer, pos, w1, w2):
    kv, make_spec = src
    B = kv.shape[0]
    hidden = w1.shape[-1]
    return pl.pallas_call(
        functools.partial(_compress_body, n=n), grid=(B, 2, hkv),
        in_specs=[
            make_spec(n * CMP_STRIDE, DH, lambda b, j, h: (b, 0, h, j)),
            pl.BlockSpec((None, None, CMP_LEN, DH), lambda b, j, h: (layer, j, 0, 0)),
            pl.BlockSpec((None, None, CMP_LEN * DH, hidden), lambda b, j, h: (layer, j, 0, 0)),
            pl.BlockSpec((None, None, hidden, DH), lambda b, j, h: (layer, j, 0, 0)),
        ],
        out_specs=pl.BlockSpec((None, None, None, n, DH), lambda b, j, h: (b, j, h, 0, 0)),
        out_shape=jax.ShapeDtypeStruct((B, 2, hkv, n, DH), F32),
        compiler_params=_cparams("parallel", "arbitrary", "arbitrary"),
    )(kv, pos, w1, w2)


def _take_top(score, lane_ids, count):
    sel = jnp.zeros(score.shape, F32)
    taken = []
    big = jnp.int32(score.shape[-1])
    for _ in range(count):
        m = jnp.max(score, axis=-1, keepdims=True)
        idx = jnp.min(jnp.where(score == m, lane_ids, big), axis=-1, keepdims=True)
        hit = lane_ids == idx
        sel = jnp.where(hit, 1.0, sel)
        score = jnp.where(hit, BELOW_NEG, score)
        taken.append((hit, m))
    return sel, taken


def _cmp_body(q_ref, kc_ref, vc_ref, bias_ref, ov_ref, o_ref, sel_ref, *, G, tq, q_off, n_sel, top):
    qi = pl.program_id(2)
    kc = kc_ref[...]
    vc = vc_ref[...].astype(BF16)
    scale = DH ** -0.5
    psum = None
    outs = []
    for g in range(G):
        qg = q_ref[:, g * DH:(g + 1) * DH]
        s = lax.dot_general(qg, kc, (((1,), (1,)), ((), ())), preferred_element_type=F32,
                            precision=lax.Precision.HIGHEST) * scale + bias_ref[g]
        m = jnp.max(s, axis=-1, keepdims=True)
        e = jnp.where(s > NEG / 2, jnp.exp(s - m), 0.0)
        p = e / jnp.maximum(jnp.sum(e, axis=-1, keepdims=True), 1e-30)
        outs.append(jnp.dot(p.astype(BF16), vc, preferred_element_type=F32))
        psum = p if psum is None else psum + p
    o_ref[...] = jnp.concatenate(outs, axis=1)
    imp = jnp.dot(psum, ov_ref[...], preferred_element_type=F32, precision=lax.Precision.HIGHEST)
    shape = imp.shape
    jb = lax.broadcasted_iota(jnp.int32, shape, 1)
    qpos = q_off + qi * tq + lax.broadcasted_iota(jnp.int32, shape, 0)
    cur = _div_pow2(qpos, SEL_BLOCK)
    forced = (jb == 0) | (jb == cur) | (jb == cur - 1)
    score = jnp.where(jb <= cur, imp + jnp.where(forced, FORCE, 0.0), NEG)
    score = jnp.where(jb < n_sel, score, BELOW_NEG)
    sel, _ = _take_top(score, jb, top)
    sel_ref[...] = jnp.where((jb <= cur) & (sel > 0.5), 0.0, NEG)


def _nsa_cmp(q, u_q, cmp_kv, bias, overlap, *, hkv, G, tq, q_off, n_sel):
    B, Tq, _ = q.shape
    n = cmp_kv.shape[3]
    nsp = overlap.shape[1]
    top = min(SEL_TOP, n_sel)
    body = functools.partial(_cmp_body, G=G, tq=tq, q_off=q_off, n_sel=n_sel, top=top)
    return pl.pallas_call(
        body, grid=(B, hkv, Tq // tq),
        in_specs=[
            pl.BlockSpec((None, tq, G * DH), lambda b, h, i: (b, i, u_q // G + h)),
            pl.BlockSpec((None, None, None, n, DH), lambda b, h, i: (b, 0, h, 0, 0)),
            pl.BlockSpec((None, None, None, n, DH), lambda b, h, i: (b, 1, h, 0, 0)),
            pl.BlockSpec((G, tq, n), lambda b, h, i: (h, i, 0)),
            pl.BlockSpec((n, nsp), lambda b, h, i: (0, 0)),
        ],
        out_specs=[pl.BlockSpec((None, tq, G * DH), lambda b, h, i: (b, i, h)),
                   pl.BlockSpec((None, None, tq, nsp), lambda b, h, i: (b, h, i, 0))],
        out_shape=[jax.ShapeDtypeStruct((B, Tq, hkv * G * DH), F32),
                   jax.ShapeDtypeStruct((B, hkv, Tq, nsp), F32)],
        compiler_params=_cparams("parallel", "parallel", "arbitrary"),
    )(q, cmp_kv, cmp_kv, bias, overlap)


def _moba_gate_body(q_ref, k_ref, m_ref, *, G, n_full, nbp, q_off, top):
    Tq = q_ref.shape[0]
    shape = (Tq, nbp)
    jb = lax.broadcasted_iota(jnp.int32, shape, 1)
    own = _div_pow2(q_off + lax.broadcasted_iota(jnp.int32, shape, 0), MOBA_BLOCK)
    if n_full > 0:
        kmean = jnp.sum(k_ref[...].reshape(n_full, MOBA_BLOCK, DH), axis=1) * (1.0 / MOBA_BLOCK)
        if nbp > n_full:
            kmean = jnp.concatenate([kmean, jnp.zeros((nbp - n_full, DH), F32)], axis=0)
    for g in range(G):
        mask = jnp.where(jb == own, 0.0, NEG)
        if n_full > 0:
            gate = lax.dot_general(q_ref[:, g * DH:(g + 1) * DH], kmean, (((1,), (1,)), ((), ())),
                                   preferred_element_type=F32, precision=lax.Precision.HIGHEST)
            score = jnp.where((jb < own) & (jb < n_full), gate, NEG)
            score = jnp.where(jb < n_full, score, BELOW_NEG)
            _, taken = _take_top(score, jb, top)
            for hit, val in taken:
                mask = jnp.where(hit & (val > NEG / 2), 0.0, mask)
        m_ref[g] = mask


def _moba_gate(q, u_q, ksrc, *, hkv, G, Tk, q_off, nbp):
    k, k_spec = ksrc
    B, Tq, _ = q.shape
    n_full = Tk // MOBA_BLOCK
    top = min(MOBA_TOP, n_full)
    body = functools.partial(_moba_gate_body, G=G, n_full=n_full, nbp=nbp, q_off=q_off, top=top)
    return pl.pallas_call(
        body, grid=(B, hkv),
        in_specs=[pl.BlockSpec((None, Tq, G * DH), lambda b, h: (b, 0, u_q // G + h)),
                  k_spec(max(n_full, 1) * MOBA_BLOCK, DH, lambda b, h: (b, 0, h, 0))],
        out_specs=pl.BlockSpec((None, None, G, Tq, nbp), lambda b, h: (b, h, 0, 0, 0)),
        out_shape=jax.ShapeDtypeStruct((B, hkv, G, Tq, nbp), F32),
        compiler_params=_cparams("parallel", "arbitrary"),
    )(q, k)


def _flash_body(*refs, G, Gm, tq, tk, dv, mode, nsteps, fox, mask_blk):
    it = iter(refs)
    q_ref, k_ref, v_ref, bias_ref = next(it), next(it), next(it), next(it)
    cq_ref, ck_ref = (next(it), next(it)) if fox else (None, None)
    mask_ref = next(it) if mask_blk else None
    o_ref, m_sc, acc_sc = next(it), next(it), next(it)
    qi, step = pl.program_id(2), pl.program_id(3)
    if mode == "band":
        ki = qi - (nsteps - 1) + step
        active = ki >= 0
    elif mode == "causal":
        ki = step
        active = ki <= qi
    else:
        ki = step
        active = step >= 0

    @pl.when(step == 0)
    def _():
        m_sc[...] = jnp.full(m_sc.shape, NEG, F32)
        acc_sc[...] = jnp.zeros_like(acc_sc)

    @pl.when(active)
    def _():
        k = k_ref[...].astype(BF16)
        v = jnp.concatenate([v_ref[...].astype(BF16), jnp.ones((tk, LANE), BF16)], axis=1)
        q = jnp.concatenate([q_ref[:, g * DH:(g + 1) * DH] for g in range(G)], axis=0)
        q = (q * (DH ** -0.5)).astype(BF16)
        s = lax.dot_general(q, k, (((1,), (1,)), ((), ())), preferred_element_type=F32)
        s = s.reshape(G, tq, tk) + bias_ref[...]
        if fox:
            s = s + (cq_ref[...] - ck_ref[...])
        if mask_blk:
            nbp = mask_ref.shape[-1]
            kpos = ki * tk + lax.broadcasted_iota(jnp.int32, (nbp, tk), 1)
            lo = lax.broadcasted_iota(jnp.int32, (nbp, tk), 0) * mask_blk
            expand = jnp.where((kpos >= lo) & (kpos < lo + mask_blk), 1.0, 0.0).astype(BF16)
            picked = jnp.dot(mask_ref[...].reshape(Gm * tq, nbp).astype(BF16), expand, preferred_element_type=F32)
            s = s + picked.reshape(Gm, tq, tk)
        m_old = m_sc[...]
        m_new = jnp.maximum(m_old, jnp.max(s, axis=-1, keepdims=True))
        m_use = jnp.where(m_new < NEG / 2, 0.0, m_new)
        p = jnp.exp((s - m_use).astype(BF16))
        alpha = jnp.exp(m_old - m_use)
        pv = jnp.dot(p.reshape(G * tq, tk), v, preferred_element_type=F32)
        acc_sc[...] = alpha * acc_sc[...] + pv.reshape(G, tq, dv + LANE)
        m_sc[...] = m_new

    @pl.when(step == nsteps - 1)
    def _():
        acc = acc_sc[...]
        o = acc[:, :, :dv] / jnp.maximum(acc[:, :, dv:dv + 1], 1e-30)
        for g in range(G):
            o_ref[:, g * dv:(g + 1) * dv] = o[g]


def _flash(q, u_q, ksrc, vsrc, bias, *, hkv, G, tq, tk, dv, mode, nsteps, fox=None, mask=None, mask_blk=0,
           q_stride=1):
    B, Tq, _ = q.shape
    nq = Tq // tq
    bias, bias_h0, bias_shared = bias
    NB = bias.shape[1]
    Gb = 1 if bias_shared else G
    assert bias_h0 % Gb == 0

    if mode == "causal":
        kidx = lambda i, s: jnp.minimum(s, i)
        bidx = lambda i, s: jnp.clip(i - s, 0, NB - 1)
    elif mode == "band":
        kidx = lambda i, s: jnp.maximum(i - (nsteps - 1) + s, 0)
        bidx = lambda i, s: jnp.clip(nsteps - 1 - s, 0, NB - 1)
    else:
        kidx = lambda i, s: s
        bidx = lambda i, s: s

    kv_index = lambda b, h, i, s: (b, kidx(i, s), h, 0)
    in_specs = [
        pl.BlockSpec((None, tq, G * DH), lambda b, h, i, s: (b, i, u_q // G + h * q_stride)),
        ksrc[1](tk, DH, kv_index),
        vsrc[1](tk, dv, kv_index),
        pl.BlockSpec((Gb, None, tq, tk),
                     lambda b, h, i, s: (bias_h0 // Gb + (0 if bias_shared else h), bidx(i, s), 0, 0)),
    ]
    args = [q, ksrc[0], vsrc[0], bias]
    if fox is not None:
        in_specs += [pl.BlockSpec((None, G, tq, 1), lambda b, h, i, s: (b, h, i, 0)),
                     pl.BlockSpec((None, G, 1, tk), lambda b, h, i, s: (b, h, 0, kidx(i, s)))]
        args += list(fox)
    Gm = 0
    if mask is not None:
        Gm, nbp = mask.shape[2], mask.shape[4]
        in_specs.append(pl.BlockSpec((None, None, Gm, tq, nbp), lambda b, h, i, s: (b, h, 0, i, 0)))
        args.append(mask)
    body = functools.partial(_flash_body, G=G, Gm=Gm, tq=tq, tk=tk, dv=dv, mode=mode, nsteps=nsteps,
                             fox=fox is not None, mask_blk=mask_blk if mask is not None else 0)
    return pl.pallas_call(
        body, grid=(B, hkv, nq, nsteps), in_specs=in_specs,
        out_specs=pl.BlockSpec((None, tq, G * dv), lambda b, h, i, s: (b, i, h)),
        out_shape=jax.ShapeDtypeStruct((B, Tq, hkv * G * dv), F32),
        scratch_shapes=[pltpu.VMEM((G, tq, 1), F32), pltpu.VMEM((G, tq, dv + LANE), F32)],
        compiler_params=_cparams("parallel", "parallel", "parallel", "arbitrary"),
    )(*args)


def _head_rms(x, gain):
    return x * lax.rsqrt(jnp.mean(x * x, axis=-1, keepdims=True) + RMS_EPS) * gain


def _combine_body(cmp_ref, sel_ref, win_ref, ag_ref, b_ref, c1_ref, c2_ref, d_ref, mg_ref, lp_ref, o_ref, *,
                  h_a, h_b, h_c, h_d, lam_init):
    gates = jax.nn.sigmoid(ag_ref[...])
    col = 0
    for h in range(h_a):
        sl = slice(h * DH, (h + 1) * DH)
        o = (gates[:, 3 * h:3 * h + 1] * cmp_ref[:, sl] + gates[:, 3 * h + 1:3 * h + 2] * sel_ref[:, sl]
             + gates[:, 3 * h + 2:3 * h + 3] * win_ref[:, sl])
        o_ref[:, col:col + DH] = _head_rms(o, mg_ref[:, col:col + DH]).astype(o_ref.dtype)
        col += DH
    for h in range(h_b):
        o_ref[:, col:col + DH] = _head_rms(b_ref[:, h * DH:(h + 1) * DH], mg_ref[:, col:col + DH]).astype(o_ref.dtype)
        col += DH
    for h in range(h_c):
        lp = lp_ref[h]
        lam = (jnp.exp(jnp.sum(lp[0:1] * lp[1:2], axis=-1, keepdims=True))
               - jnp.exp(jnp.sum(lp[2:3] * lp[3:4], axis=-1, keepdims=True)) + lam_init)
        sl = slice(h * 2 * DH, (h + 1) * 2 * DH)
        o = c1_ref[:, sl] - lam * c2_ref[:, sl]
        o_ref[:, col:col + 2 * DH] = (_head_rms(o, mg_ref[:, col:col + 2 * DH]) * (1.0 - lam_init)).astype(o_ref.dtype)
        col += 2 * DH
    for h in range(h_d):
        o_ref[:, col:col + DH] = _head_rms(d_ref[:, h * DH:(h + 1) * DH], mg_ref[:, col:col + DH]).astype(o_ref.dtype)
        col += DH


def _combine(o_cmp, o_sel, o_win, proj, o_b, o_c1, o_c2, o_d, mix_gain, diff_lambda, layer, *, tm, lam_init):
    M = o_cmp.shape[0]
    D = mix_gain.shape[-1]
    h_a, h_b, h_d = o_cmp.shape[1] // DH, o_b.shape[1] // DH, o_d.shape[1] // DH
    h_c = o_c1.shape[1] // (2 * DH)
    row = lambda w: pl.BlockSpec((tm, w), lambda m: (m, 0))
    body = functools.partial(_combine_body, h_a=h_a, h_b=h_b, h_c=h_c, h_d=h_d, lam_init=lam_init)
    return pl.pallas_call(
        body, grid=(M // tm,),
        in_specs=[row(o_cmp.shape[1]), row(o_sel.shape[1]), row(o_win.shape[1]),
                  pl.BlockSpec((tm, LANE), lambda m: (m, U_AG)),
                  row(o_b.shape[1]), row(o_c1.shape[1]), row(o_c2.shape[1]), row(o_d.shape[1]),
                  pl.BlockSpec((None, 1, D), lambda m: (layer, 0, 0)),
                  pl.BlockSpec((None,) + diff_lambda.shape[1:], lambda m: (layer, 0, 0, 0))],
        out_specs=pl.BlockSpec((tm, D), lambda m: (m, 0)),
        out_shape=jax.ShapeDtypeStruct((M, D), BF16),
        compiler_params=_cparams("parallel"),
    )(o_cmp, o_sel, o_win, proj, o_b, o_c1, o_c2, o_d, mix_gain, diff_lambda)


def _t5_bucket_np(dist):
    n = np.maximum(dist, 0)
    exact = N_BUCKETS // 2
    nf = np.maximum(n, 1).astype(np.float32)
    big = exact + (np.log(nf / np.float32(exact)) / np.float32(math.log(MAX_DIST / exact))
                   * np.float32(N_BUCKETS - exact)).astype(np.int32)
    return np.where(n < exact, n, np.minimum(big, N_BUCKETS - 1)).astype(np.int32)


def _bias_from_dist(tab_t, dist, valid):
    if tab_t is None:
        return jnp.asarray(np.where(valid, 0.0, NEG).astype(np.float32))[None]
    idx = _t5_bucket_np(dist).reshape(-1)
    b = jnp.take(tab_t, jnp.asarray(idx), axis=1).reshape((tab_t.shape[0],) + dist.shape)
    return jnp.where(jnp.asarray(valid)[None], b, NEG)


def _dist_table(tab_t):
    if tab_t is None:
        return jnp.zeros((1, MAX_DIST + 1), F32)
    return jnp.take(tab_t, jnp.asarray(_t5_bucket_np(np.arange(MAX_DIST + 1))), axis=1)


def _bias_range(bd, lo, hi, window=None):
    H, nd = bd.shape
    far = 10 ** 9
    w = far if window is None else window
    assert w >= nd - 1
    parts = []
    for seg_lo, seg_hi, kind in ((-far, 0, "neg"), (0, nd - 1, "tab"), (nd - 1, w, "far"), (w, far, "neg")):
        a, b = max(lo, seg_lo), min(hi, seg_hi)
        if b <= a:
            continue
        if kind == "neg":
            parts.append(jnp.full((H, b - a), NEG, F32))
        elif kind == "tab":
            parts.append(bd[:, a:b])
        else:
            parts.append(jnp.broadcast_to(bd[:, nd - 1:], (H, b - a)))
    return jnp.concatenate(parts, axis=1)


def _prompt_bias(bd, t, n_var, window=None):
    H = bd.shape[0]
    tiles = []
    for v in range(n_var):
        vec = jnp.concatenate([_bias_range(bd, v * t - t, v * t + 1, window)[:, ::-1],
                               _bias_range(bd, v * t + 1, v * t + t, window)[:, ::-1]], axis=1)
        flat = jnp.broadcast_to(vec[:, None, :], (H, t, 2 * t)).reshape(H, 2 * t * t)
        tiles.append(flat[:, :t * (2 * t - 1)].reshape(H, t, 2 * t - 1)[:, :, :t])
    return jnp.stack(tiles, axis=1)


def _decode_bias(bd, tq, q_rel, n_keys, tk, window=None):
    H = bd.shape[0]
    vec = _bias_range(bd, q_rel - n_keys + 1, q_rel + tq, window)[:, ::-1]
    rows = jnp.stack([vec[:, tq - 1 - i:tq - 1 - i + n_keys] for i in range(tq)], axis=1)
    return rows.reshape(H, tq, n_keys // tk, tk).transpose(0, 2, 1, 3)


def _cmp_bias_prompt(bd, T, n):
    H = bd.shape[0]
    L = T + CMP_STRIDE * n
    vec = jnp.concatenate([_bias_range(bd, 1 - CMP_LEN, T + 1 - CMP_LEN), jnp.full((H, L - T), NEG, F32)], axis=1)
    flat = jnp.broadcast_to(vec[:, None, :], (H, n, L)).reshape(H, n * L)
    rows = flat[:, :n * (L - CMP_STRIDE)].reshape(H, n, L - CMP_STRIDE)[:, :, :T]
    rows = jnp.where(jnp.asarray(np.arange(n) < n - 1)[None, :, None], rows, NEG)
    return rows.transpose(0, 2, 1)


def _cmp_bias(tab_t, q_pos, n):
    end = np.arange(n) * CMP_STRIDE + CMP_LEN - 1
    dist = q_pos[:, None] - end[None, :]
    valid = (dist >= 0) & (np.arange(n)[None, :] < n - 1)
    return _bias_from_dist(tab_t, dist, valid)


def _overlap_np(n, n_sel, n_sel_pad):
    cs = np.arange(n)[:, None] * CMP_STRIDE
    js = np.arange(n_sel_pad)[None, :] * SEL_BLOCK
    ov = (cs < js + SEL_BLOCK) & (cs + CMP_LEN > js) & (np.arange(n)[:, None] < n - 1) & (np.arange(n_sel_pad)[None, :] < n_sel)
    return ov.astype(np.float32)


def _round_up(x, m):
    return -(-x // m) * m


def _reorder_w_in(w, dims):
    h_a, hkv_a, h_b, hkv_b, h_c, hkv_c, h_d, hkv_d = dims
    widths = [h_a * DH] + [hkv_a * DH] * 6 + [h_a * 3, h_b * DH, hkv_b * DH, hkv_b * DH, h_b,
                                             h_c * 2 * DH, hkv_c * 2 * DH, hkv_c * 2 * DH, h_d * DH, hkv_d * DH, hkv_d * DH]
    offs = np.concatenate([[0], np.cumsum(widths)])
    seg = lambda i: w[:, offs[i]:offs[i + 1]]
    D = w.shape[0]
    g_c = h_c // hkv_c
    c_q = seg(12).reshape(D, hkv_c, g_c, 2, DH).transpose(0, 1, 3, 2, 4).reshape(D, -1)
    pad = lambda a: jnp.pad(a, ((0, 0), (0, LANE - a.shape[1])))
    return jnp.concatenate([seg(0), seg(8), seg(15), c_q, seg(1), seg(2), seg(3), seg(4), seg(5), seg(6),
                            seg(9), seg(10), seg(13), seg(14), seg(16), seg(17), pad(seg(7)), pad(seg(11))], axis=1)


def _layer(x, B, T, mod, layer, past, prm, tabs, dims, tiles):
    h_a, hkv_a, h_b, hkv_b, h_c, hkv_c, h_d, hkv_d = dims
    g_a, g_b, g_c, g_d = h_a // hkv_a, h_b // hkv_b, h_c // hkv_c, h_d // hkv_d
    D = x.shape[1]
    M = B * T
    sh1, sc1, gt1, sh2, sc2, gt2 = [m[:, None, :] for m in jnp.split(mod, 6, axis=-1)]
    tm_n = tiles["norm_tm"]
    h = _norm(x, prm["norm_attn"], layer, tm=tm_n, sc=sc1, sh=sh1, rows_per_batch=T, out_dtype=BF16)
    proj = _matmul(h, prm["w_in_r"], None, **tiles["w_in"]).reshape(B, T, U_TOTAL * LANE)
    new = {
        "nsa": proj[:, :, U_NSA * LANE:U_WIN * LANE], "fox": proj[:, :, U_FOX * LANE:U_DIFF * LANE],
        "diff": proj[:, :, U_DIFF * LANE:U_MOBA * LANE], "moba": proj[:, :, U_MOBA * LANE:U_AG * LANE],
    }
    kw_new = proj[:, :, U_WIN * LANE:U_FOX * LANE]
    tq, tk = tiles["tq"], tiles["tk"]

    if past is None:
        q_off, Tk = 0, T
        src = dict(
            cmp=_lane_src(proj, U_NSA, jstride=hkv_a),
            sel_k=_lane_src(proj, U_NSA + 2 * hkv_a), sel_v=_lane_src(proj, U_NSA + 3 * hkv_a),
            win_k=_lane_src(proj, U_WIN), win_v=_lane_src(proj, U_WIN + hkv_a),
            fox_k=_lane_src(proj, U_FOX), fox_v=_lane_src(proj, U_FOX + hkv_b),
            diff_k=[_lane_src(proj, U_DIFF + half, stride=2) for half in range(2)],
            diff_v=_lane_src(proj, U_DIFF + 2 * hkv_c),
            moba_k=_lane_src(proj, U_MOBA), moba_v=_lane_src(proj, U_MOBA + hkv_d))
        assert T >= tiles["wbuf"]
        new["win"] = kw_new[:, T - tiles["wbuf"]:]
        lf_new, cum = _fox_cum(proj[:, :, U_BF * LANE:], prm["fox_fbias"], None)
        mode, nsteps = "causal", T // tk
        band_steps = tabs["win"].shape[1]
        n_cmp_pad = T // CMP_STRIDE
        tk_pad = T
    else:
        q_off = past["len"]
        Tk = q_off + T
        page = past["page"]

        def full(name):
            new_page = jnp.pad(new[name], ((0, 0), (0, page - T), (0, 0)))
            return _gather_pages(past[name], layer, past["pt"], new_page, slots=past["slots"][name])

        nsa_kv, fox_kv, diff_kv, moba_kv = full("nsa"), full("fox"), full("diff"), full("moba")
        past_lf = _gather_pages(past["logf"], layer, past["pt"], jnp.zeros((B, page, h_b), F32))[:, :q_off]
        win_kv = jnp.concatenate([past["win"][layer], kw_new], axis=1)
        wbuf = past["win"].shape[2]
        new["win"] = win_kv[:, -wbuf:]
        src = dict(
            cmp=_lane_src(nsa_kv, 0, jstride=hkv_a),
            sel_k=_lane_src(nsa_kv, 2 * hkv_a), sel_v=_lane_src(nsa_kv, 3 * hkv_a),
            win_k=_lane_src(win_kv, 0), win_v=_lane_src(win_kv, hkv_a),
            fox_k=_lane_src(fox_kv, 0), fox_v=_lane_src(fox_kv, hkv_b),
            diff_k=[_lane_src(diff_kv, half, stride=2) for half in range(2)], diff_v=_lane_src(diff_kv, 2 * hkv_c),
            moba_k=_lane_src(moba_kv, 0), moba_v=_lane_src(moba_kv, hkv_d))
        lf_new, cum = _fox_cum(proj[:, :, U_BF * LANE:], prm["fox_fbias"], past_lf)
        mode, nsteps = "full", nsa_kv.shape[1] // tk
        band_steps = 1
        n_cmp_pad = q_off // CMP_STRIDE
        tk_pad = nsa_kv.shape[1]
    new["logf"] = lf_new

    cmp_kv = _compress(src["cmp"], hkv_a, n_cmp_pad, layer, prm["cmp_pos"], prm["cmp_w1"], prm["cmp_w2"])
    n_sel = -(-Tk // SEL_BLOCK)
    o_cmp, sel_mask = _nsa_cmp(proj, U_QA, cmp_kv, tabs["cmp"], tabs["overlap"], hkv=hkv_a, G=g_a,
                               tq=min(tiles["cmp_tq"], T), q_off=q_off, n_sel=n_sel)
    o_sel = _flash(proj, U_QA, src["sel_k"], src["sel_v"], tabs["a"], hkv=hkv_a, G=g_a,
                   tq=tq, tk=tk, dv=DH, mode=mode, nsteps=nsteps, mask=sel_mask[:, :, None], mask_blk=SEL_BLOCK)
    if past is None:
        o_win = _flash(proj, U_QA, src["win_k"], src["win_v"], (tabs["win"], 0, False), hkv=hkv_a, G=g_a,
                       tq=tabs["win"].shape[2], tk=tabs["win"].shape[3], dv=DH, mode="band", nsteps=band_steps)
    else:
        o_win = _flash(proj, U_QA, src["win_k"], src["win_v"], (tabs["win"], 0, False), hkv=hkv_a, G=g_a,
                       tq=tq, tk=tabs["win"].shape[3], dv=DH, mode="full", nsteps=1)

    cum_t = cum.transpose(0, 2, 1)
    cq = cum_t[:, :, Tk - T:, None]
    ck = jnp.pad(cum_t, ((0, 0), (0, 0), (0, tk_pad - Tk)))[:, :, None, :]
    o_b = _flash(proj, U_QB, src["fox_k"], src["fox_v"], tabs["mask"], hkv=hkv_b, G=g_b,
                 tq=tq, tk=tk, dv=DH, mode=mode, nsteps=nsteps, fox=(cq, ck))

    o_c = [_flash(proj, U_QC + half * g_c, src["diff_k"][half], src["diff_v"], tabs["c"],
                  hkv=hkv_c, G=g_c, tq=tq, tk=tk, dv=2 * DH, mode=mode, nsteps=nsteps, q_stride=2)
           for half in range(2)]

    nb = -(-Tk // MOBA_BLOCK)
    moba_mask = _moba_gate(proj, U_QD, src["moba_k"], hkv=hkv_d, G=g_d, Tk=Tk, q_off=q_off,
                           nbp=_round_up(nb, LANE))
    o_d = _flash(proj, U_QD, src["moba_k"], src["moba_v"], tabs["d"], hkv=hkv_d, G=g_d,
                 tq=tq, tk=tk, dv=DH, mode=mode, nsteps=nsteps, mask=moba_mask, mask_blk=MOBA_BLOCK)

    lam_init = 0.8 - 0.6 * math.exp(-0.3 * layer)
    flat = lambda a: a.reshape(M, a.shape[-1])
    o = _combine(flat(o_cmp), flat(o_sel), flat(o_win), flat(proj), flat(o_b), flat(o_c[0]), flat(o_c[1]),
                 flat(o_d), prm["mix_gain"], prm["diff_lambda"], layer, tm=tiles["comb_tm"], lam_init=lam_init)

    if T % tiles["w_out"]["tm"] == 0:
        gates = dict(gate1=gt1, gate2=gt2, rows_per_batch=T)
    else:
        gates = dict(gate1=jnp.repeat(gt1[:, 0], T, axis=0), gate2=jnp.repeat(gt2[:, 0], T, axis=0), rows_per_batch=None)
    x = _matmul(o, prm["w_out"], layer, res=x, gate=gates["gate1"], rows_per_batch=gates["rows_per_batch"],
                **tiles["w_out"])
    h2 = _norm(x, prm["norm_ffn"], layer, tm=tm_n, sc=sc2, sh=sh2, rows_per_batch=T, out_dtype=BF16)
    act = _matmul(h2, prm["w_gate"], layer, w2=prm["w_up"], out_dtype=BF16, **tiles["w_ff"])
    x = _matmul(act, prm["w_down"], layer, res=x, gate=gates["gate2"], rows_per_batch=gates["rows_per_batch"],
                **tiles["w_down"])
    return x, new


def _group_tables(t5_table, dims, T, q_off, tiles, tk_pad, win_len):
    h_a, hkv_a, h_b, hkv_b, h_c, hkv_c, h_d, hkv_d = dims
    tab_t = t5_table.astype(F32).T
    tab_a = tab_t[:h_a]
    tq, tk = tiles["tq"], tiles["tk"]
    Tk = q_off + T
    n = (Tk - CMP_LEN) // CMP_STRIDE + 2
    n_sel = -(-Tk // SEL_BLOCK)
    tabs = {"overlap": jnp.asarray(_overlap_np(n, n_sel, _round_up(n_sel, LANE)))}
    bd = jnp.concatenate([_dist_table(tab_t), _dist_table(None)], axis=0)
    n_bias = tab_t.shape[0]
    if q_off == 0:
        assert tq == tk
        tw = tiles["win_t"]
        nband = (WINDOW - 1 + tw - 1) // tw + 1
        tabs.update(cmp=_cmp_bias_prompt(bd[:h_a], T, n), causal=_prompt_bias(bd, tq, 3),
                    win=_prompt_bias(bd[:h_a], tw, min(nband, T // tw), WINDOW))
    else:
        assert win_len % 8 == 0 and q_off >= win_len - T
        tabs.update(cmp=_cmp_bias(tab_a, q_off + np.arange(T), n), causal=_decode_bias(bd, T, q_off, tk_pad, tk),
                    win=_decode_bias(bd[:h_a], T, win_len - T, win_len, win_len, WINDOW))
    tabs.update(a=(tabs["causal"], 0, False), c=(tabs["causal"], h_a, False), d=(tabs["causal"], h_a + h_c, False),
                mask=(tabs["causal"], n_bias, True))
    return tabs


PROMPT_TILES = dict(
    tq=512, tk=512, win_t=512, cmp_tq=512, norm_tm=256, comb_tm=256,
    w_in=dict(tm=1024, tn=512, tk=4096), w_out=dict(tm=1024, tn=512, tk=4096),
    w_ff=dict(tm=1024, tn=256, tk=4096), w_down=dict(tm=2048, tn=512, tk=1024),
)


def _sample_tiles(M, T, tk_pad):
    tk = tk_pad
    for div in (2, 3, 4, 6):
        if tk_pad % div == 0 and (tk_pad // div) % LANE == 0 and tk_pad // div >= 4096:
            tk = tk_pad // div
    return dict(
        tq=T, tk=tk, cmp_tq=T, norm_tm=T, comb_tm=M,
        w_in=dict(tm=M, tn=512, tk=4096), w_out=dict(tm=M, tn=512, tk=4096),
        w_ff=dict(tm=M, tn=256, tk=4096), w_down=dict(tm=M, tn=1024, tk=1024),
    )


def kernel(x_prompt, x_sample, cache_nsa, state_nsa_win, cache_fox, cache_fox_logf, cache_diff, cache_moba,
           page_table, c_prompt, c_sample, t5_table, ada_w, ada_b, norm_attn, norm_ffn, w_in, w_out, mix_gain,
           nsa_cmp_pos, nsa_cmp_w1, nsa_cmp_w2, fox_fbias, diff_lambda, w_gate, w_up, w_down, final_norm):
    depth = w_in.shape[0]
    B, T, D = x_prompt.shape
    Bs, Ts, _ = x_sample.shape
    hkv_a, hkv_b, hkv_c, hkv_d = cache_nsa.shape[4], cache_fox.shape[4], cache_diff.shape[4], cache_moba.shape[4]
    h_b, h_c = fox_fbias.shape[1], diff_lambda.shape[1]
    h_a = (D // DH - h_b - 2 * h_c) // 2
    h_d = h_a
    dims = (h_a, hkv_a, h_b, hkv_b, h_c, hkv_c, h_d, hkv_d)
    page = cache_nsa.shape[2]
    n_pages = page_table.shape[1]
    past_len = n_pages * page
    tk_pad = past_len + page
    win_len = state_nsa_win.shape[2] + Ts

    p_tiles = dict(PROMPT_TILES, wbuf=state_nsa_win.shape[2])
    s_tiles = _sample_tiles(Bs * Ts, Ts, tk_pad)
    tabs_p = _group_tables(t5_table, dims, T, 0, p_tiles, T, 0)
    tabs_s = _group_tables(t5_table, dims, Ts, past_len, s_tiles, tk_pad, win_len)

    def piece_rows(c):
        d, n_phys, pg, n_t, hkv, w = c.shape
        halves = w // LANE
        c = c.reshape(d, n_phys, pg, n_t, hkv, halves, LANE).transpose(0, 1, 2, 3, 5, 4, 6)
        slots = [(t * halves + half) * hkv + h for t in range(n_t) for h in range(hkv) for half in range(halves)]
        return c.reshape(d, n_phys, pg * n_t * hkv * halves, LANE), slots

    rows = {name: piece_rows(c) for name, c in
            (("nsa", cache_nsa), ("fox", cache_fox), ("diff", cache_diff), ("moba", cache_moba))}
    past = dict({name: r[0] for name, r in rows.items()}, slots={name: r[1] for name, r in rows.items()},
                logf=cache_fox_logf,
                win=state_nsa_win.reshape(depth, Bs, state_nsa_win.shape[2], -1),
                pt=page_table, len=past_len, page=page)

    n_c = _round_up(B + Bs, 8)
    c_all = jnp.pad(jnp.concatenate([c_prompt, c_sample], axis=0), ((0, n_c - B - Bs), (0, 0)))
    xp, xs = x_prompt.reshape(B * T, D), x_sample.reshape(Bs * Ts, D)
    names = ("nsa", "win", "fox", "logf", "diff", "moba")
    st_p = {n: [] for n in names}
    st_s = {n: [] for n in names}
    r3 = lambda a: a.reshape(a.shape[0], 1, a.shape[1])
    for l in range(depth):
        mod = _matmul(c_all, ada_w, l, bias=ada_b[l][None], silu_in=True, tm=n_c, tn=512, tk=D)
        prm = dict(norm_attn=r3(norm_attn), norm_ffn=r3(norm_ffn), w_in_r=_reorder_w_in(w_in[l], dims), w_out=w_out,
                   mix_gain=r3(mix_gain), cmp_pos=nsa_cmp_pos, cmp_w1=nsa_cmp_w1, cmp_w2=nsa_cmp_w2,
                   fox_fbias=fox_fbias[l][None], diff_lambda=diff_lambda, w_gate=w_gate, w_up=w_up, w_down=w_down)
        xp, new_p = _layer(xp, B, T, mod[:B], l, None, prm, tabs_p, dims, p_tiles)
        xs, new_s = _layer(xs, Bs, Ts, mod[B:B + Bs], l, past, prm, tabs_s, dims, s_tiles)
        for n in names:
            st_p[n].append(new_p[n])
            st_s[n].append(new_s[n])
    y_p = _norm(xp, final_norm[None], None, tm=p_tiles["norm_tm"]).reshape(B, T, D)
    y_s = _norm(xs, final_norm[None], None, tm=s_tiles["norm_tm"]).reshape(Bs, Ts, D)

    def stack(st, name, tail):
        a = jnp.stack(st[name])
        return a.reshape(a.shape[:3] + tail)

    kv = lambda n, hk, w: (n, hk, w)
    shapes = {"nsa": kv(4, hkv_a, DH), "win": kv(2, hkv_a, DH), "fox": kv(2, hkv_b, DH), "logf": (h_b,),
              "diff": kv(2, hkv_c, 2 * DH), "moba": kv(2, hkv_d, DH)}
    out = [y_p, y_s]
    for n in names:
        out += [stack(st_p, n, shapes[n]), stack(st_s, n, shapes[n])]
    return tuple(out)
```

```python
import functools
import math

import jax
import jax.numpy as jnp
import numpy as np
from jax import lax
from jax.experimental import pallas as pl
from jax.experimental.pallas import tpu as pltpu

DH = 128
CMP_LEN = 32
CMP_STRIDE = 16
SEL_BLOCK = 64
SEL_TOP = 16
WINDOW = 512
MOBA_BLOCK = 256
MOBA_TOP = 3
N_BUCKETS = 32
MAX_DIST = 128
NEG = -1e30
FORCE = 1e4
RMS_EPS = 1e-6
BELOW_NEG = -3e38

LANE = 128
V7X_VMEM_LIMIT_BYTES = 60000 * 1024

U_QA, U_QB, U_QD, U_QC = 0, 8, 16, 24
U_NSA, U_WIN, U_FOX, U_DIFF, U_MOBA, U_AG, U_BF, U_TOTAL = 32, 40, 44, 48, 56, 60, 61, 62

BF16 = jnp.bfloat16
F32 = jnp.float32


def _cparams(*sem):
    return pltpu.CompilerParams(dimension_semantics=sem, vmem_limit_bytes=V7X_VMEM_LIMIT_BYTES)


def _silu(x):
    return x * jax.nn.sigmoid(x)


def _div_pow2(x, d):
    assert d & (d - 1) == 0
    return lax.shift_right_logical(x, jnp.int32(d.bit_length() - 1))


def _lane_src(arr, unit0, stride=1, jstride=0):
    def make_spec(rows, width, to_brhj):
        def index_map(*g):
            b, r, h, j = to_brhj(*g)
            return (b, r, (unit0 * DH) // width + h * stride + j * jstride)
        return pl.BlockSpec((None, rows, width), index_map)
    return arr, make_spec


def _mm_body(*refs, nk, tk, k_rem, silu_in, dual, has_bias, has_res):
    it = iter(refs)
    x_ref, w_ref = next(it), next(it)
    w2_ref = next(it) if dual else None
    b_ref = next(it) if has_bias else None
    r_ref, g_ref = (next(it), next(it)) if has_res else (None, None)
    o_ref, acc = next(it), next(it)
    acc2 = next(it) if dual else None
    k = pl.program_id(2)

    @pl.when(k == 0)
    def _():
        acc[...] = jnp.zeros_like(acc)
        if dual:
            acc2[...] = jnp.zeros_like(acc2)

    def accumulate(overhang):
        xv = x_ref[...]
        if silu_in:
            xv = _silu(xv.astype(F32))
        xv = xv.astype(BF16)
        wv = w_ref[...].astype(BF16)
        w2v = w2_ref[...].astype(BF16) if dual else None
        if overhang:
            xv = jnp.where(lax.broadcasted_iota(jnp.int32, xv.shape, 1) < k_rem, xv, jnp.zeros_like(xv))
            rows = lax.broadcasted_iota(jnp.int32, wv.shape, 0) < k_rem
            wv = jnp.where(rows, wv, jnp.zeros_like(wv))
            if dual:
                w2v = jnp.where(rows, w2v, jnp.zeros_like(w2v))
        acc[...] += jnp.dot(xv, wv, preferred_element_type=F32)
        if dual:
            acc2[...] += jnp.dot(xv, w2v, preferred_element_type=F32)

    if k_rem:
        pl.when(k < nk - 1)(lambda: accumulate(False))
        pl.when(k == nk - 1)(lambda: accumulate(True))
    else:
        accumulate(False)

    @pl.when(k == nk - 1)
    def _():
        r = acc[...]
        if dual:
            r = _silu(r) * acc2[...]
        if has_bias:
            r = r + b_ref[...]
        if has_res:
            r = r_ref[...] + g_ref[...] * r
        o_ref[...] = r.astype(o_ref.dtype)


def _matmul(x, w, layer, *, tm, tn, tk, w2=None, bias=None, res=None, gate=None, rows_per_batch=None,
            silu_in=False, out_dtype=F32):
    M, K = x.shape
    N = w.shape[-1]
    assert M % tm == 0
    nk = pl.cdiv(K, tk)
    k_rem = K % tk
    grid = (M // tm, pl.cdiv(N, tn), nk)
    if layer is None:
        w_spec = pl.BlockSpec((tk, tn), lambda m, n, k: (k, n))
    else:
        w_spec = pl.BlockSpec((None, tk, tn), lambda m, n, k: (layer, k, n))
    in_specs = [pl.BlockSpec((tm, tk), lambda m, n, k: (m, k)), w_spec]
    args = [x, w]
    if w2 is not None:
        in_specs.append(w_spec)
        args.append(w2)
    if bias is not None:
        in_specs.append(pl.BlockSpec((1, tn), lambda m, n, k: (0, n)))
        args.append(bias)
    if res is not None:
        in_specs.append(pl.BlockSpec((tm, tn), lambda m, n, k: (m, n)))
        args.append(res)
        if gate.ndim == 2:
            in_specs.append(pl.BlockSpec((tm, tn), lambda m, n, k: (m, n)))
        else:
            assert rows_per_batch % tm == 0
            per = rows_per_batch // tm
            in_specs.append(pl.BlockSpec((None, 1, tn), lambda m, n, k: (m // per, 0, n)))
        args.append(gate)
    scratch = [pltpu.VMEM((tm, tn), F32)] * (2 if w2 is not None else 1)
    body = functools.partial(_mm_body, nk=nk, tk=tk, k_rem=k_rem, silu_in=silu_in, dual=w2 is not None,
                             has_bias=bias is not None, has_res=res is not None)
    return pl.pallas_call(
        body, grid=grid, in_specs=in_specs,
        out_specs=pl.BlockSpec((tm, tn), lambda m, n, k: (m, n)),
        out_shape=jax.ShapeDtypeStruct((M, N), out_dtype),
        scratch_shapes=scratch,
        compiler_params=_cparams("parallel", "parallel", "arbitrary"),
    )(*args)


def _norm_body(*refs, modulated):
    if modulated:
        x_ref, g_ref, sc_ref, sh_ref, o_ref = refs
    else:
        x_ref, g_ref, o_ref = refs
    x = x_ref[...]
    y = x * lax.rsqrt(jnp.mean(x * x, axis=-1, keepdims=True) + RMS_EPS) * g_ref[...]
    if modulated:
        y = y * (1.0 + sc_ref[...]) + sh_ref[...]
    o_ref[...] = y.astype(o_ref.dtype)


def _norm(x, g, layer, *, tm, sc=None, sh=None, rows_per_batch=None, out_dtype=F32):
    M, D = x.shape
    modulated = sc is not None
    if layer is None:
        g_spec = pl.BlockSpec((1, D), lambda m: (0, 0))
    else:
        g_spec = pl.BlockSpec((None, 1, D), lambda m: (layer, 0, 0))
    in_specs = [pl.BlockSpec((tm, D), lambda m: (m, 0)), g_spec]
    args = [x, g]
    if modulated:
        assert rows_per_batch % tm == 0
        per = rows_per_batch // tm
        mod_spec = pl.BlockSpec((None, 1, D), lambda m: (m // per, 0, 0))
        in_specs += [mod_spec, mod_spec]
        args += [sc, sh]
    return pl.pallas_call(
        functools.partial(_norm_body, modulated=modulated), grid=(M // tm,), in_specs=in_specs,
        out_specs=pl.BlockSpec((tm, D), lambda m: (m, 0)),
        out_shape=jax.ShapeDtypeStruct((M, D), out_dtype),
        compiler_params=_cparams("parallel"),
    )(*args)


GATHER_PAGES_PER_STEP = 8


def _gather_body(pt_ref, *refs, n_groups, per, page, slots, n_pieces):
    cache_refs, new_ref, o_ref = refs[:per], refs[per], refs[per + 1]
    j = pl.program_id(1)

    @pl.when(j < n_groups)
    def _():
        for p in range(per):
            if slots is None:
                o_ref[p * page:(p + 1) * page] = cache_refs[p][...]
            else:
                for s, src_slot in enumerate(slots):
                    o_ref[p * page:(p + 1) * page, s * LANE:(s + 1) * LANE] = (
                        cache_refs[p][pl.ds(src_slot, page, stride=n_pieces), :])

    @pl.when(j == n_groups)
    def _():
        o_ref[0:page] = new_ref[...]
        if per > 1:
            o_ref[page:per * page] = jnp.zeros(((per - 1) * page,) + o_ref.shape[1:], o_ref.dtype)


def _gather_pages(cache, layer, page_table, new_page, slots=None, n_pieces=1):
    page, W = new_page.shape[1:]
    B, n_pages = page_table.shape
    per = math.gcd(GATHER_PAGES_PER_STEP, n_pages)
    n_groups = n_pages // per
    assert cache.shape[2:] == ((page, W) if slots is None else (page * n_pieces, LANE))
    assert slots is None or W == len(slots) * LANE

    def page_spec(p):
        return pl.BlockSpec((None, None) + cache.shape[2:],
                            lambda b, j, pt: (layer, pt[b, jnp.minimum(j, n_groups - 1) * per + p], 0, 0))

    grid_spec = pltpu.PrefetchScalarGridSpec(
        num_scalar_prefetch=1, grid=(B, n_groups + 1),
        in_specs=[page_spec(p) for p in range(per)] + [pl.BlockSpec((None, page, W), lambda b, j, pt: (b, 0, 0))],
        out_specs=pl.BlockSpec((None, per * page, W), lambda b, j, pt: (b, j, 0)),
    )
    return pl.pallas_call(
        functools.partial(_gather_body, n_groups=n_groups, per=per, page=page, slots=slots, n_pieces=n_pieces),
        grid_spec=grid_spec,
        out_shape=jax.ShapeDtypeStruct((B, (n_pages + 1) * page, W), cache.dtype),
        compiler_params=_cparams("parallel", "arbitrary"),
    )(page_table, *([cache] * per), new_page)


CUM_CHUNK = 256


def _cum_body(*refs, n_past, n_new, nh):
    if n_past:
        past_ref, raw_ref, fb_ref, lf_ref, cum_ref = refs
    else:
        raw_ref, fb_ref, lf_ref, cum_ref = refs
    z = raw_ref[...][:, :nh] + fb_ref[...]
    lf = -(jnp.maximum(-z, 0.0) + jnp.log1p(jnp.exp(-jnp.abs(z))))
    lf_ref[...] = lf

    def tri(n):
        return (lax.broadcasted_iota(jnp.int32, (n, n), 0) >= lax.broadcasted_iota(jnp.int32, (n, n), 1)).astype(F32)

    def scan_rows(src, dst_off, n, carry):
        c = min(CUM_CHUNK, n)
        assert n % c == 0
        t = tri(c)

        def step(i, carry):
            r0 = pl.multiple_of(i * c, c)
            blk = jnp.dot(t, src(r0, c), preferred_element_type=F32, precision=lax.Precision.HIGHEST) + carry
            cum_ref[pl.ds(dst_off + r0, c), :] = blk
            return blk[c - 1:c, :]

        return lax.fori_loop(0, n // c, step, carry)

    carry = jnp.zeros((1, nh), F32)
    if n_past:
        carry = scan_rows(lambda r0, c: past_ref[pl.ds(r0, c), :], 0, n_past, carry)
    lf_ref[...] = lf
    scan_rows(lambda r0, c: lf_ref[pl.ds(r0, c), :], n_past, n_new, carry)


def _fox_cum(raw_f, fbias, past_lf):
    B, Tn, _ = raw_f.shape
    nh = fbias.shape[-1]
    n_past = 0 if past_lf is None else past_lf.shape[1]
    in_specs, args = [], []
    if n_past:
        in_specs.append(pl.BlockSpec((None, n_past, nh), lambda b: (b, 0, 0)))
        args.append(past_lf)
    in_specs += [pl.BlockSpec((None, Tn, LANE), lambda b: (b, 0, 0)), pl.BlockSpec((1, nh), lambda b: (0, 0))]
    args += [raw_f, fbias]
    return pl.pallas_call(
        functools.partial(_cum_body, n_past=n_past, n_new=Tn, nh=nh), grid=(B,), in_specs=in_specs,
        out_specs=[pl.BlockSpec((None, Tn, nh), lambda b: (b, 0, 0)),
                   pl.BlockSpec((None, n_past + Tn, nh), lambda b: (b, 0, 0))],
        out_shape=[jax.ShapeDtypeStruct((B, Tn, nh), F32), jax.ShapeDtypeStruct((B, n_past + Tn, nh), F32)],
        compiler_params=_cparams("parallel"),
    )(*args)


def _gelu_tanh(x):
    return 0.5 * x * (1.0 + jnp.tanh(math.sqrt(2.0 / math.pi) * (x + 0.044715 * (x * x * x))))


def _compress_body(x_ref, pos_ref, w1_ref, w2_ref, o_ref, *, n):
    hidden = w1_ref.shape[-1]
    acc_lo = jnp.zeros((n, hidden), F32)
    acc_hi = jnp.zeros((n, hidden), F32)
    for rho in range(CMP_STRIDE):
        xr = x_ref[pl.ds(rho, n, stride=CMP_STRIDE), :]
        lo = (xr + pos_ref[rho:rho + 1, :]).astype(BF16)
        hi = (xr + pos_ref[rho + CMP_STRIDE:rho + CMP_STRIDE + 1, :]).astype(BF16)
        w_lo = w1_ref[rho * DH:(rho + 1) * DH, :].astype(BF16)
        w_hi = w1_ref[(rho + CMP_STRIDE) * DH:(rho + CMP_STRIDE + 1) * DH, :].astype(BF16)
        acc_lo += jnp.dot(lo, w_lo, preferred_element_type=F32)
        acc_hi += jnp.dot(hi, w_hi, preferred_element_type=F32)
    hid = acc_lo + pltpu.roll(acc_hi, n - 1, 0)
    o_ref[...] = jnp.dot(_gelu_tanh(hid).astype(BF16), w2_ref[...].astype(BF16), preferred_element_type=F32)


def _compress(src, hkv, n, layer, pos, w1, w2):
    kv, make_spec = src
    B = kv.shape[0]
    hidden = w1.shape[-1]
    return pl.pallas_call(
        functools.partial(_compress_body, n=n), grid=(B, 2, hkv),
        in_specs=[
            make_spec(n * CMP_STRIDE, DH, lambda b, j, h: (b, 0, h, j)),
            pl.BlockSpec((None, None, CMP_LEN, DH), lambda b, j, h: (layer, j, 0, 0)),
            pl.BlockSpec((None, None, CMP_LEN * DH, hidden), lambda b, j, h: (layer, j, 0, 0)),
            pl.BlockSpec((None, None, hidden, DH), lambda b, j, h: (layer, j, 0, 0)),
        ],
        out_specs=pl.BlockSpec((None, None, None, n, DH), lambda b, j, h: (b, j, h, 0, 0)),
        out_shape=jax.ShapeDtypeStruct((B, 2, hkv, n, DH), F32),
        compiler_params=_cparams("parallel", "arbitrary", "arbitrary"),
    )(kv, pos, w1, w2)


def _take_top(score, lane_ids, count):
    sel = jnp.zeros(score.shape, F32)
    taken = []
    big = jnp.int32(score.shape[-1])
    for _ in range(count):
        m = jnp.max(score, axis=-1, keepdims=True)
        idx = jnp.min(jnp.where(score == m, lane_ids, big), axis=-1, keepdims=True)
        hit = lane_ids == idx
        sel = jnp.where(hit, 1.0, sel)
        score = jnp.where(hit, BELOW_NEG, score)
        taken.append((hit, m))
    return sel, taken


def _cmp_body(q_ref, kc_ref, vc_ref, bias_ref, ov_ref, o_ref, sel_ref, *, G, tq, q_off, n_sel, top):
    qi = pl.program_id(2)
    kc = kc_ref[...]
    vc = vc_ref[...].astype(BF16)
    scale = DH ** -0.5
    psum = None
    outs = []
    for g in range(G):
        qg = q_ref[:, g * DH:(g + 1) * DH]
        s = lax.dot_general(qg, kc, (((1,), (1,)), ((), ())), preferred_element_type=F32,
                            precision=lax.Precision.HIGHEST) * scale + bias_ref[g]
        m = jnp.max(s, axis=-1, keepdims=True)
        e = jnp.where(s > NEG / 2, jnp.exp(s - m), 0.0)
        p = e / jnp.maximum(jnp.sum(e, axis=-1, keepdims=True), 1e-30)
        outs.append(jnp.dot(p.astype(BF16), vc, preferred_element_type=F32))
        psum = p if psum is None else psum + p
    o_ref[...] = jnp.concatenate(outs, axis=1)
    imp = jnp.dot(psum, ov_ref[...], preferred_element_type=F32, precision=lax.Precision.HIGHEST)
    shape = imp.shape
    jb = lax.broadcasted_iota(jnp.int32, shape, 1)
    qpos = q_off + qi * tq + lax.broadcasted_iota(jnp.int32, shape, 0)
    cur = _div_pow2(qpos, SEL_BLOCK)
    forced = (jb == 0) | (jb == cur) | (jb == cur - 1)
    score = jnp.where(jb <= cur, imp + jnp.where(forced, FORCE, 0.0), NEG)
    score = jnp.where(jb < n_sel, score, BELOW_NEG)
    sel, _ = _take_top(score, jb, top)
    sel_ref[...] = jnp.where((jb <= cur) & (sel > 0.5), 0.0, NEG)


def _nsa_cmp(q, u_q, cmp_kv, bias, overlap, *, hkv, G, tq, q_off, n_sel):
    B, Tq, _ = q.shape
    n = cmp_kv.shape[3]
    nsp = overlap.shape[1]
    top = min(SEL_TOP, n_sel)
    body = functools.partial(_cmp_body, G=G, tq=tq, q_off=q_off, n_sel=n_sel, top=top)
    return pl.pallas_call(
        body, grid=(B, hkv, Tq // tq),
        in_specs=[
            pl.BlockSpec((None, tq, G * DH), lambda b, h, i: (b, i, u_q // G + h)),
            pl.BlockSpec((None, None, None, n, DH), lambda b, h, i: (b, 0, h, 0, 0)),
            pl.BlockSpec((None, None, None, n, DH), lambda b, h, i: (b, 1, h, 0, 0)),
            pl.BlockSpec((G, tq, n), lambda b, h, i: (h, i, 0)),
            pl.BlockSpec((n, nsp), lambda b, h, i: (0, 0)),
        ],
        out_specs=[pl.BlockSpec((None, tq, G * DH), lambda b, h, i: (b, i, h)),
                   pl.BlockSpec((None, None, tq, nsp), lambda b, h, i: (b, h, i, 0))],
        out_shape=[jax.ShapeDtypeStruct((B, Tq, hkv * G * DH), F32),
                   jax.ShapeDtypeStruct((B, hkv, Tq, nsp), F32)],
        compiler_params=_cparams("parallel", "parallel", "arbitrary"),
    )(q, cmp_kv, cmp_kv, bias, overlap)


def _moba_gate_body(q_ref, k_ref, m_ref, *, G, n_full, nbp, q_off, top):
    Tq = q_ref.shape[0]
    shape = (Tq, nbp)
    jb = lax.broadcasted_iota(jnp.int32, shape, 1)
    own = _div_pow2(q_off + lax.broadcasted_iota(jnp.int32, shape, 0), MOBA_BLOCK)
    if n_full > 0:
        kmean = jnp.sum(k_ref[...].reshape(n_full, MOBA_BLOCK, DH), axis=1) * (1.0 / MOBA_BLOCK)
        if nbp > n_full:
            kmean = jnp.concatenate([kmean, jnp.zeros((nbp - n_full, DH), F32)], axis=0)
    for g in range(G):
        mask = jnp.where(jb == own, 0.0, NEG)
        if n_full > 0:
            gate = lax.dot_general(q_ref[:, g * DH:(g + 1) * DH], kmean, (((1,), (1,)), ((), ())),
                                   preferred_element_type=F32, precision=lax.Precision.HIGHEST)
            score = jnp.where((jb < own) & (jb < n_full), gate, NEG)
            score = jnp.where(jb < n_full, score, BELOW_NEG)
            _, taken = _take_top(score, jb, top)
            for hit, val in taken:
                mask = jnp.where(hit & (val > NEG / 2), 0.0, mask)
        m_ref[g] = mask


def _moba_gate(q, u_q, ksrc, *, hkv, G, Tk, q_off, nbp):
    k, k_spec = ksrc
    B, Tq, _ = q.shape
    n_full = Tk // MOBA_BLOCK
    top = min(MOBA_TOP, n_full)
    body = functools.partial(_moba_gate_body, G=G, n_full=n_full, nbp=nbp, q_off=q_off, top=top)
    return pl.pallas_call(
        body, grid=(B, hkv),
        in_specs=[pl.BlockSpec((None, Tq, G * DH), lambda b, h: (b, 0, u_q // G + h)),
                  k_spec(max(n_full, 1) * MOBA_BLOCK, DH, lambda b, h: (b, 0, h, 0))],
        out_specs=pl.BlockSpec((None, None, G, Tq, nbp), lambda b, h: (b, h, 0, 0, 0)),
        out_shape=jax.ShapeDtypeStruct((B, hkv, G, Tq, nbp), F32),
        compiler_params=_cparams("parallel", "arbitrary"),
    )(q, k)


def _flash_body(*refs, G, Gm, tq, tk, dv, mode, nsteps, fox, mask_blk):
    it = iter(refs)
    q_ref, k_ref, v_ref, bias_ref = next(it), next(it), next(it), next(it)
    cq_ref, ck_ref = (next(it), next(it)) if fox else (None, None)
    mask_ref = next(it) if mask_blk else None
    o_ref, m_sc, acc_sc = next(it), next(it), next(it)
    qi, step = pl.program_id(2), pl.program_id(3)
    if mode == "band":
        ki = qi - (nsteps - 1) + step
        active = ki >= 0
    elif mode == "causal":
        ki = step
        active = ki <= qi
    else:
        ki = step
        active = step >= 0

    @pl.when(step == 0)
    def _():
        m_sc[...] = jnp.full(m_sc.shape, NEG, F32)
        acc_sc[...] = jnp.zeros_like(acc_sc)

    @pl.when(active)
    def _():
        k = k_ref[...].astype(BF16)
        v = jnp.concatenate([v_ref[...].astype(BF16), jnp.ones((tk, LANE), BF16)], axis=1)
        q = jnp.concatenate([q_ref[:, g * DH:(g + 1) * DH] for g in range(G)], axis=0)
        q = (q * (DH ** -0.5)).astype(BF16)
        s = lax.dot_general(q, k, (((1,), (1,)), ((), ())), preferred_element_type=F32)
        s = s.reshape(G, tq, tk) + bias_ref[...]
        if fox:
            s = s + (cq_ref[...] - ck_ref[...])
        if mask_blk:
            nbp = mask_ref.shape[-1]
            kpos = ki * tk + lax.broadcasted_iota(jnp.int32, (nbp, tk), 1)
            lo = lax.broadcasted_iota(jnp.int32, (nbp, tk), 0) * mask_blk
            expand = jnp.where((kpos >= lo) & (kpos < lo + mask_blk), 1.0, 0.0).astype(BF16)
            picked = jnp.dot(mask_ref[...].reshape(Gm * tq, nbp).astype(BF16), expand, preferred_element_type=F32)
            s = s + picked.reshape(Gm, tq, tk)
        m_old = m_sc[...]
        m_new = jnp.maximum(m_old, jnp.max(s, axis=-1, keepdims=True))
        m_use = jnp.where(m_new < NEG / 2, 0.0, m_new)
        p = jnp.exp((s - m_use).astype(BF16))
        alpha = jnp.exp(m_old - m_use)
        pv = jnp.dot(p.reshape(G * tq, tk), v, preferred_element_type=F32)
        acc_sc[...] = alpha * acc_sc[...] + pv.reshape(G, tq, dv + LANE)
        m_sc[...] = m_new

    @pl.when(step == nsteps - 1)
    def _():
        acc = acc_sc[...]
        o = acc[:, :, :dv] / jnp.maximum(acc[:, :, dv:dv + 1], 1e-30)
        for g in range(G):
            o_ref[:, g * dv:(g + 1) * dv] = o[g]


def _flash(q, u_q, ksrc, vsrc, bias, *, hkv, G, tq, tk, dv, mode, nsteps, fox=None, mask=None, mask_blk=0,
           q_stride=1):
    B, Tq, _ = q.shape
    nq = Tq // tq
    bias, bias_h0, bias_shared = bias
    NB = bias.shape[1]
    Gb = 1 if bias_shared else G
    assert bias_h0 % Gb == 0

    if mode == "causal":
        kidx = lambda i, s: jnp.minimum(s, i)
        bidx = lambda i, s: jnp.clip(i - s, 0, NB - 1)
    elif mode == "band":
        kidx = lambda i, s: jnp.maximum(i - (nsteps - 1) + s, 0)
        bidx = lambda i, s: jnp.clip(nsteps - 1 - s, 0, NB - 1)
    else:
        kidx = lambda i, s: s
        bidx = lambda i, s: s

    kv_index = lambda b, h, i, s: (b, kidx(i, s), h, 0)
    in_specs = [
        pl.BlockSpec((None, tq, G * DH), lambda b, h, i, s: (b, i, u_q // G + h * q_stride)),
        ksrc[1](tk, DH, kv_index),
        vsrc[1](tk, dv, kv_index),
        pl.BlockSpec((Gb, None, tq, tk),
                     lambda b, h, i, s: (bias_h0 // Gb + (0 if bias_shared else h), bidx(i, s), 0, 0)),
    ]
    args = [q, ksrc[0], vsrc[0], bias]
    if fox is not None:
        in_specs += [pl.BlockSpec((None, G, tq, 1), lambda b, h, i, s: (b, h, i, 0)),
                     pl.BlockSpec((None, G, 1, tk), lambda b, h, i, s: (b, h, 0, kidx(i, s)))]
        args += list(fox)
    Gm = 0
    if mask is not None:
        Gm, nbp = mask.shape[2], mask.shape[4]
        in_specs.append(pl.BlockSpec((None, None, Gm, tq, nbp), lambda b, h, i, s: (b, h, 0, i, 0)))
        args.append(mask)
    body = functools.partial(_flash_body, G=G, Gm=Gm, tq=tq, tk=tk, dv=dv, mode=mode, nsteps=nsteps,
                             fox=fox is not None, mask_blk=mask_blk if mask is not None else 0)
    return pl.pallas_call(
        body, grid=(B, hkv, nq, nsteps), in_specs=in_specs,
        out_specs=pl.BlockSpec((None, tq, G * dv), lambda b, h, i, s: (b, i, h)),
        out_shape=jax.ShapeDtypeStruct((B, Tq, hkv * G * dv), F32),
        scratch_shapes=[pltpu.VMEM((G, tq, 1), F32), pltpu.VMEM((G, tq, dv + LANE), F32)],
        compiler_params=_cparams("parallel", "parallel", "parallel", "arbitrary"),
    )(*args)


PAGED_PER_STEP = 8


def _flash_paged_body(pt_ref, *refs, groups, n_out, per, page, n_pieces, n_groups, dv, fox, has_mask, mask_blk):
    it = iter(refs)
    q_ref = next(it)
    page_refs = [next(it) for _ in range(per)]
    new_ref, bias_ref = next(it), next(it)
    cq_ref, ck_ref = (next(it), next(it)) if fox else (None, None)
    mask_ref = next(it) if has_mask else None
    o_refs = [next(it) for _ in range(n_out)]
    m_sc, acc_sc = next(it), next(it)
    j = pl.program_id(1)
    last = j == n_groups
    tk = per * page
    tq = q_ref.shape[0]

    @pl.when(j == 0)
    def _():
        m_sc[...] = jnp.full(m_sc.shape, NEG, F32)
        acc_sc[...] = jnp.zeros_like(acc_sc)

    loaded = {}

    def piece(slot):
        if slot not in loaded:
            parts = []
            for p in range(per):
                x = page_refs[p][pl.ds(slot, page, stride=n_pieces), :]
                if p == 0:
                    x = jnp.where(last, new_ref[pl.ds(slot, page, stride=n_pieces), :], x)
                parts.append(x)
            loaded[slot] = jnp.concatenate(parts, axis=0).astype(BF16)
        return loaded[slot]

    expand = None
    if has_mask:
        nbp = mask_ref.shape[-1]
        kpos = j * tk + lax.broadcasted_iota(jnp.int32, (nbp, tk), 1)
        lo = lax.broadcasted_iota(jnp.int32, (nbp, tk), 0) * mask_blk
        expand = jnp.where((kpos >= lo) & (kpos < lo + mask_blk), 1.0, 0.0).astype(BF16)

    ones = jnp.ones((tk, LANE), BF16)
    row = 0
    for grp in groups:
        G = len(grp["q_units"])
        k = piece(grp["k_piece"])
        v = jnp.concatenate([piece(s) for s in grp["v_pieces"]] + [ones], axis=1)
        q = jnp.concatenate([q_ref[:, u * DH:(u + 1) * DH] for u in grp["q_units"]], axis=0)
        q = (q * (DH ** -0.5)).astype(BF16)
        s = lax.dot_general(q, k, (((1,), (1,)), ((), ())), preferred_element_type=F32).reshape(G, tq, tk)
        b0 = grp["bias_head"]
        s = s + (bias_ref[...] if b0 is None else bias_ref[b0:b0 + G])
        if fox:
            f0 = grp["fox_head"]
            s = s + (cq_ref[f0:f0 + G] - ck_ref[f0:f0 + G])
        if has_mask:
            msk = mask_ref[grp["mask_head"]]
            gm = msk.shape[0]
            picked = jnp.dot(msk.reshape(gm * tq, nbp).astype(BF16), expand, preferred_element_type=F32)
            s = s + picked.reshape(gm, tq, tk)
        rows = slice(row, row + G)
        m_old = m_sc[rows]
        m_new = jnp.maximum(m_old, jnp.max(s, axis=-1, keepdims=True))
        m_use = jnp.where(m_new < NEG / 2, 0.0, m_new)
        p = jnp.exp((s - m_use).astype(BF16))
        alpha = jnp.exp(m_old - m_use)
        pv = jnp.dot(p.reshape(G * tq, tk), v, preferred_element_type=F32)
        acc_sc[rows] = alpha * acc_sc[rows] + pv.reshape(G, tq, dv + LANE)
        m_sc[rows] = m_new
        row += G

    @pl.when(last)
    def _():
        row = 0
        for grp in groups:
            G = len(grp["q_units"])
            acc = acc_sc[row:row + G]
            o = acc[:, :, :dv] / jnp.maximum(acc[:, :, dv:dv + 1], 1e-30)
            out_i, unit0 = grp["out"]
            for g in range(G):
                o_refs[out_i][:, (unit0 + g) * dv:(unit0 + g + 1) * dv] = o[g]
            row += G


def _flash_paged(q, q_block, groups, out_heads, cache, n_pieces, layer, page_table, new_page, bias, *, dv,
                 fox=None, mask=None, mask_blk=0):
    B, Tq, _ = q.shape
    n_pages = page_table.shape[1]
    page = cache.shape[2] // n_pieces
    per = math.gcd(PAGED_PER_STEP, n_pages)
    n_groups = n_pages // per
    tk = per * page
    assert bias.shape[1:] == (n_groups + 1, Tq, tk)
    n_rows = sum(len(g["q_units"]) for g in groups)
    q_w = 8 * DH

    def page_spec(p):
        return pl.BlockSpec((None, None, page * n_pieces, LANE),
                            lambda b, j, pt: (layer, pt[b, jnp.minimum(j, n_groups - 1) * per + p], 0, 0))

    in_specs = [pl.BlockSpec((None, Tq, q_w), lambda b, j, pt: (b, 0, q_block))]
    in_specs += [page_spec(p) for p in range(per)]
    in_specs += [pl.BlockSpec((None, page * n_pieces, LANE), lambda b, j, pt: (b, 0, 0)),
                 pl.BlockSpec((bias.shape[0], None, Tq, tk), lambda b, j, pt: (0, j, 0, 0))]
    args = [q] + [cache] * per + [new_page, bias]
    if fox is not None:
        H = fox[0].shape[1]
        in_specs += [pl.BlockSpec((None, H, Tq, 1), lambda b, j, pt: (b, 0, 0, 0)),
                     pl.BlockSpec((None, H, 1, tk), lambda b, j, pt: (b, 0, 0, j))]
        args += list(fox)
    if mask is not None:
        in_specs.append(pl.BlockSpec((None,) + mask.shape[1:], lambda b, j, pt: (b, 0, 0, 0, 0)))
        args.append(mask)
    body = functools.partial(_flash_paged_body, groups=groups, n_out=len(out_heads), per=per, page=page,
                             n_pieces=n_pieces, n_groups=n_groups, dv=dv, fox=fox is not None,
                             has_mask=mask is not None, mask_blk=mask_blk)
    grid_spec = pltpu.PrefetchScalarGridSpec(
        num_scalar_prefetch=1, grid=(B, n_groups + 1), in_specs=in_specs,
        out_specs=[pl.BlockSpec((None, Tq, nh * dv), lambda b, j, pt: (b, 0, 0)) for nh in out_heads],
        scratch_shapes=[pltpu.VMEM((n_rows, Tq, 1), F32), pltpu.VMEM((n_rows, Tq, dv + LANE), F32)],
    )
    return pl.pallas_call(
        body, grid_spec=grid_spec,
        out_shape=[jax.ShapeDtypeStruct((B, Tq, nh * dv), F32) for nh in out_heads],
        compiler_params=_cparams("parallel", "arbitrary"),
    )(page_table, *args)


def _head_rms(x, gain):
    return x * lax.rsqrt(jnp.mean(x * x, axis=-1, keepdims=True) + RMS_EPS) * gain


def _combine_body(cmp_ref, sel_ref, win_ref, ag_ref, b_ref, c1_ref, c2_ref, d_ref, mg_ref, lp_ref, o_ref, *,
                  h_a, h_b, h_c, h_d, lam_init):
    gates = jax.nn.sigmoid(ag_ref[...])
    col = 0
    for h in range(h_a):
        sl = slice(h * DH, (h + 1) * DH)
        o = (gates[:, 3 * h:3 * h + 1] * cmp_ref[:, sl] + gates[:, 3 * h + 1:3 * h + 2] * sel_ref[:, sl]
             + gates[:, 3 * h + 2:3 * h + 3] * win_ref[:, sl])
        o_ref[:, col:col + DH] = _head_rms(o, mg_ref[:, col:col + DH]).astype(o_ref.dtype)
        col += DH
    for h in range(h_b):
        o_ref[:, col:col + DH] = _head_rms(b_ref[:, h * DH:(h + 1) * DH], mg_ref[:, col:col + DH]).astype(o_ref.dtype)
        col += DH
    for h in range(h_c):
        lp = lp_ref[h]
        lam = (jnp.exp(jnp.sum(lp[0:1] * lp[1:2], axis=-1, keepdims=True))
               - jnp.exp(jnp.sum(lp[2:3] * lp[3:4], axis=-1, keepdims=True)) + lam_init)
        sl = slice(h * 2 * DH, (h + 1) * 2 * DH)
        o = c1_ref[:, sl] - lam * c2_ref[:, sl]
        o_ref[:, col:col + 2 * DH] = (_head_rms(o, mg_ref[:, col:col + 2 * DH]) * (1.0 - lam_init)).astype(o_ref.dtype)
        col += 2 * DH
    for h in range(h_d):
        o_ref[:, col:col + DH] = _head_rms(d_ref[:, h * DH:(h + 1) * DH], mg_ref[:, col:col + DH]).astype(o_ref.dtype)
        col += DH


def _combine(o_cmp, o_sel, o_win, proj, o_b, o_c1, o_c2, o_d, mix_gain, diff_lambda, layer, *, tm, lam_init):
    M = o_cmp.shape[0]
    D = mix_gain.shape[-1]
    h_a, h_b, h_d = o_cmp.shape[1] // DH, o_b.shape[1] // DH, o_d.shape[1] // DH
    h_c = o_c1.shape[1] // (2 * DH)
    row = lambda w: pl.BlockSpec((tm, w), lambda m: (m, 0))
    body = functools.partial(_combine_body, h_a=h_a, h_b=h_b, h_c=h_c, h_d=h_d, lam_init=lam_init)
    return pl.pallas_call(
        body, grid=(M // tm,),
        in_specs=[row(o_cmp.shape[1]), row(o_sel.shape[1]), row(o_win.shape[1]),
                  pl.BlockSpec((tm, LANE), lambda m: (m, U_AG)),
                  row(o_b.shape[1]), row(o_c1.shape[1]), row(o_c2.shape[1]), row(o_d.shape[1]),
                  pl.BlockSpec((None, 1, D), lambda m: (layer, 0, 0)),
                  pl.BlockSpec((None,) + diff_lambda.shape[1:], lambda m: (layer, 0, 0, 0))],
        out_specs=pl.BlockSpec((tm, D), lambda m: (m, 0)),
        out_shape=jax.ShapeDtypeStruct((M, D), BF16),
        compiler_params=_cparams("parallel"),
    )(o_cmp, o_sel, o_win, proj, o_b, o_c1, o_c2, o_d, mix_gain, diff_lambda)


def _t5_bucket_np(dist):
    n = np.maximum(dist, 0)
    exact = N_BUCKETS // 2
    nf = np.maximum(n, 1).astype(np.float32)
    big = exact + (np.log(nf / np.float32(exact)) / np.float32(math.log(MAX_DIST / exact))
                   * np.float32(N_BUCKETS - exact)).astype(np.int32)
    return np.where(n < exact, n, np.minimum(big, N_BUCKETS - 1)).astype(np.int32)


def _bias_from_dist(tab_t, dist, valid):
    if tab_t is None:
        return jnp.asarray(np.where(valid, 0.0, NEG).astype(np.float32))[None]
    idx = _t5_bucket_np(dist).reshape(-1)
    b = jnp.take(tab_t, jnp.asarray(idx), axis=1).reshape((tab_t.shape[0],) + dist.shape)
    return jnp.where(jnp.asarray(valid)[None], b, NEG)


def _dist_table(tab_t):
    if tab_t is None:
        return jnp.zeros((1, MAX_DIST + 1), F32)
    return jnp.take(tab_t, jnp.asarray(_t5_bucket_np(np.arange(MAX_DIST + 1))), axis=1)


def _bias_range(bd, lo, hi, window=None):
    H, nd = bd.shape
    far = 10 ** 9
    w = far if window is None else window
    assert w >= nd - 1
    parts = []
    for seg_lo, seg_hi, kind in ((-far, 0, "neg"), (0, nd - 1, "tab"), (nd - 1, w, "far"), (w, far, "neg")):
        a, b = max(lo, seg_lo), min(hi, seg_hi)
        if b <= a:
            continue
        if kind == "neg":
            parts.append(jnp.full((H, b - a), NEG, F32))
        elif kind == "tab":
            parts.append(bd[:, a:b])
        else:
            parts.append(jnp.broadcast_to(bd[:, nd - 1:], (H, b - a)))
    return jnp.concatenate(parts, axis=1)


def _prompt_bias(bd, t, n_var, window=None):
    H = bd.shape[0]
    tiles = []
    for v in range(n_var):
        vec = jnp.concatenate([_bias_range(bd, v * t - t, v * t + 1, window)[:, ::-1],
                               _bias_range(bd, v * t + 1, v * t + t, window)[:, ::-1]], axis=1)
        flat = jnp.broadcast_to(vec[:, None, :], (H, t, 2 * t)).reshape(H, 2 * t * t)
        tiles.append(flat[:, :t * (2 * t - 1)].reshape(H, t, 2 * t - 1)[:, :, :t])
    return jnp.stack(tiles, axis=1)


def _decode_bias(bd, tq, q_rel, n_keys, tk, window=None):
    H = bd.shape[0]
    vec = _bias_range(bd, q_rel - n_keys + 1, q_rel + tq, window)[:, ::-1]
    rows = jnp.stack([vec[:, tq - 1 - i:tq - 1 - i + n_keys] for i in range(tq)], axis=1)
    return rows.reshape(H, tq, n_keys // tk, tk).transpose(0, 2, 1, 3)


def _cmp_bias_prompt(bd, T, n):
    H = bd.shape[0]
    L = T + CMP_STRIDE * n
    vec = jnp.concatenate([_bias_range(bd, 1 - CMP_LEN, T + 1 - CMP_LEN), jnp.full((H, L - T), NEG, F32)], axis=1)
    flat = jnp.broadcast_to(vec[:, None, :], (H, n, L)).reshape(H, n * L)
    rows = flat[:, :n * (L - CMP_STRIDE)].reshape(H, n, L - CMP_STRIDE)[:, :, :T]
    rows = jnp.where(jnp.asarray(np.arange(n) < n - 1)[None, :, None], rows, NEG)
    return rows.transpose(0, 2, 1)


def _cmp_bias(tab_t, q_pos, n):
    end = np.arange(n) * CMP_STRIDE + CMP_LEN - 1
    dist = q_pos[:, None] - end[None, :]
    valid = (dist >= 0) & (np.arange(n)[None, :] < n - 1)
    return _bias_from_dist(tab_t, dist, valid)


def _overlap_np(n, n_sel, n_sel_pad):
    cs = np.arange(n)[:, None] * CMP_STRIDE
    js = np.arange(n_sel_pad)[None, :] * SEL_BLOCK
    ov = (cs < js + SEL_BLOCK) & (cs + CMP_LEN > js) & (np.arange(n)[:, None] < n - 1) & (np.arange(n_sel_pad)[None, :] < n_sel)
    return ov.astype(np.float32)


def _round_up(x, m):
    return -(-x // m) * m


def _reorder_w_in(w, dims):
    h_a, hkv_a, h_b, hkv_b, h_c, hkv_c, h_d, hkv_d = dims
    widths = [h_a * DH] + [hkv_a * DH] * 6 + [h_a * 3, h_b * DH, hkv_b * DH, hkv_b * DH, h_b,
                                             h_c * 2 * DH, hkv_c * 2 * DH, hkv_c * 2 * DH, h_d * DH, hkv_d * DH, hkv_d * DH]
    offs = np.concatenate([[0], np.cumsum(widths)])
    seg = lambda i: w[:, offs[i]:offs[i + 1]]
    D = w.shape[0]
    g_c = h_c // hkv_c
    c_q = seg(12).reshape(D, hkv_c, g_c, 2, DH).transpose(0, 1, 3, 2, 4).reshape(D, -1)
    pad = lambda a: jnp.pad(a, ((0, 0), (0, LANE - a.shape[1])))
    return jnp.concatenate([seg(0), seg(8), seg(15), c_q, seg(1), seg(2), seg(3), seg(4), seg(5), seg(6),
                            seg(9), seg(10), seg(13), seg(14), seg(16), seg(17), pad(seg(7)), pad(seg(11))], axis=1)


def _layer(x, B, T, mod, layer, past, prm, tabs, dims, tiles):
    h_a, hkv_a, h_b, hkv_b, h_c, hkv_c, h_d, hkv_d = dims
    g_a, g_b, g_c, g_d = h_a // hkv_a, h_b // hkv_b, h_c // hkv_c, h_d // hkv_d
    D = x.shape[1]
    M = B * T
    sh1, sc1, gt1, sh2, sc2, gt2 = [m[:, None, :] for m in jnp.split(mod, 6, axis=-1)]
    tm_n = tiles["norm_tm"]
    h = _norm(x, prm["norm_attn"], layer, tm=tm_n, sc=sc1, sh=sh1, rows_per_batch=T, out_dtype=BF16)
    proj = _matmul(h, prm["w_in_r"], None, **tiles["w_in"]).reshape(B, T, U_TOTAL * LANE)
    new = {
        "nsa": proj[:, :, U_NSA * LANE:U_WIN * LANE], "fox": proj[:, :, U_FOX * LANE:U_DIFF * LANE],
        "diff": proj[:, :, U_DIFF * LANE:U_MOBA * LANE], "moba": proj[:, :, U_MOBA * LANE:U_AG * LANE],
    }
    kw_new = proj[:, :, U_WIN * LANE:U_FOX * LANE]
    tq, tk = tiles["tq"], tiles["tk"]

    if past is None:
        q_off, Tk = 0, T
        src = dict(
            cmp=_lane_src(proj, U_NSA, jstride=hkv_a),
            sel_k=_lane_src(proj, U_NSA + 2 * hkv_a), sel_v=_lane_src(proj, U_NSA + 3 * hkv_a),
            win_k=_lane_src(proj, U_WIN), win_v=_lane_src(proj, U_WIN + hkv_a),
            fox_k=_lane_src(proj, U_FOX), fox_v=_lane_src(proj, U_FOX + hkv_b),
            diff_k=[_lane_src(proj, U_DIFF + half, stride=2) for half in range(2)],
            diff_v=_lane_src(proj, U_DIFF + 2 * hkv_c),
            moba_k=_lane_src(proj, U_MOBA), moba_v=_lane_src(proj, U_MOBA + hkv_d))
        assert T >= tiles["wbuf"]
        new["win"] = kw_new[:, T - tiles["wbuf"]:]
        lf_new, cum = _fox_cum(proj[:, :, U_BF * LANE:], prm["fox_fbias"], None)
        mode, nsteps = "causal", T // tk
        band_steps = tabs["win"].shape[1]
        n_cmp_pad = T // CMP_STRIDE
        tk_pad = T
    else:
        q_off = past["len"]
        Tk = q_off + T
        page = past["page"]

        pieces = past["pieces"]

        def contiguous(name, n_lanes, slots):
            new_page = jnp.pad(new[name][:, :, :n_lanes], ((0, 0), (0, page - T), (0, 0)))
            return _gather_pages(past[name], layer, past["pt"], new_page, slots=slots,
                                 n_pieces=pieces[name]["n"])

        pc = pieces["nsa"]["of"]
        nsa_cmp = contiguous("nsa", 2 * hkv_a * DH, [pc(t, h) for t in range(2) for h in range(hkv_a)])
        pc = pieces["moba"]["of"]
        moba_k = contiguous("moba", hkv_d * DH, [pc(0, h) for h in range(hkv_d)])
        past_lf = _gather_pages(past["logf"], layer, past["pt"], jnp.zeros((B, page, h_b), F32))[:, :q_off]
        win_kv = jnp.concatenate([past["win"][layer], kw_new], axis=1)
        wbuf = past["win"].shape[2]
        new["win"] = win_kv[:, -wbuf:]
        src = dict(cmp=_lane_src(nsa_cmp, 0, jstride=hkv_a), moba_k=_lane_src(moba_k, 0),
                   win_k=_lane_src(win_kv, 0), win_v=_lane_src(win_kv, hkv_a))
        lf_new, cum = _fox_cum(proj[:, :, U_BF * LANE:], prm["fox_fbias"], past_lf)
        n_cmp_pad = q_off // CMP_STRIDE
        tk_pad = tabs["causal"].shape[1] * tabs["causal"].shape[3]

        def paged(name, q_block, groups, out_heads, bias, dv=DH, **kw):
            info = pieces[name]
            rows = new[name].reshape(B, T, info["n_t"], info["hkv"], info["halves"], LANE)
            rows = rows.transpose(0, 1, 2, 4, 3, 5).reshape(B, T * info["n"], LANE)
            new_page = jnp.pad(rows, ((0, 0), (0, (page - T) * info["n"]), (0, 0)))
            return _flash_paged(proj, q_block, groups, out_heads, past[name], info["n"], layer, past["pt"],
                                new_page, bias, dv=dv, **kw)
    new["logf"] = lf_new

    cmp_kv = _compress(src["cmp"], hkv_a, n_cmp_pad, layer, prm["cmp_pos"], prm["cmp_w1"], prm["cmp_w2"])
    n_sel = -(-Tk // SEL_BLOCK)
    o_cmp, sel_mask = _nsa_cmp(proj, U_QA, cmp_kv, tabs["cmp"], tabs["overlap"], hkv=hkv_a, G=g_a,
                               tq=min(tiles["cmp_tq"], T), q_off=q_off, n_sel=n_sel)
    sel_mask = sel_mask[:, :, None]
    cum_t = cum.transpose(0, 2, 1)
    cq = cum_t[:, :, Tk - T:, None]
    ck = jnp.pad(cum_t, ((0, 0), (0, 0), (0, tk_pad - Tk)))[:, :, None, :]
    nb = -(-Tk // MOBA_BLOCK)
    moba_mask = _moba_gate(proj, U_QD, src["moba_k"], hkv=hkv_d, G=g_d, Tk=Tk, q_off=q_off,
                           nbp=_round_up(nb, LANE))
    if past is None:
        o_sel = _flash(proj, U_QA, src["sel_k"], src["sel_v"], tabs["a"], hkv=hkv_a, G=g_a,
                       tq=tq, tk=tk, dv=DH, mode=mode, nsteps=nsteps, mask=sel_mask, mask_blk=SEL_BLOCK)
        o_win = _flash(proj, U_QA, src["win_k"], src["win_v"], (tabs["win"], 0, False), hkv=hkv_a, G=g_a,
                       tq=tabs["win"].shape[2], tk=tabs["win"].shape[3], dv=DH, mode="band", nsteps=band_steps)
        o_b = _flash(proj, U_QB, src["fox_k"], src["fox_v"], tabs["mask"], hkv=hkv_b, G=g_b,
                     tq=tq, tk=tk, dv=DH, mode=mode, nsteps=nsteps, fox=(cq, ck))
        o_c = [_flash(proj, U_QC + half * g_c, src["diff_k"][half], src["diff_v"], tabs["c"],
                      hkv=hkv_c, G=g_c, tq=tq, tk=tk, dv=2 * DH, mode=mode, nsteps=nsteps, q_stride=2)
               for half in range(2)]
        o_d = _flash(proj, U_QD, src["moba_k"], src["moba_v"], tabs["d"], hkv=hkv_d, G=g_d,
                     tq=tq, tk=tk, dv=DH, mode=mode, nsteps=nsteps, mask=moba_mask, mask_blk=MOBA_BLOCK)
    else:
        o_win = _flash(proj, U_QA, src["win_k"], src["win_v"], (tabs["win"], 0, False), hkv=hkv_a, G=g_a,
                       tq=tq, tk=tabs["win"].shape[3], dv=DH, mode="full", nsteps=1)
        causal = tabs["causal"]
        pc = pieces["nsa"]["of"]
        o_sel, = paged("nsa", U_QA // 8, [
            dict(q_units=[h * g_a + g for g in range(g_a)], k_piece=pc(2, h), v_pieces=[pc(3, h)],
                 bias_head=h * g_a, mask_head=h, out=(0, h * g_a)) for h in range(hkv_a)],
            [h_a], causal[:h_a], mask=sel_mask, mask_blk=SEL_BLOCK)
        pc = pieces["fox"]["of"]
        o_b, = paged("fox", U_QB // 8, [
            dict(q_units=[h * g_b + g for g in range(g_b)], k_piece=pc(0, h), v_pieces=[pc(1, h)],
                 bias_head=None, fox_head=h * g_b, out=(0, h * g_b)) for h in range(hkv_b)],
            [h_b], causal[h_a + h_c + h_d:], fox=(cq, ck))
        pc = pieces["diff"]["of"]
        o_c = paged("diff", U_QC // 8, [
            dict(q_units=[(h * 2 + half) * g_c + g for g in range(g_c)], k_piece=pc(0, h, half),
                 v_pieces=[pc(1, h, 0), pc(1, h, 1)], bias_head=h * g_c, out=(half, h * g_c))
            for half in range(2) for h in range(hkv_c)],
            [h_c, h_c], causal[h_a:h_a + h_c], dv=2 * DH)
        pc = pieces["moba"]["of"]
        o_d, = paged("moba", U_QD // 8, [
            dict(q_units=[h * g_d + g for g in range(g_d)], k_piece=pc(0, h), v_pieces=[pc(1, h)],
                 bias_head=h * g_d, mask_head=h, out=(0, h * g_d)) for h in range(hkv_d)],
            [h_d], causal[h_a + h_c:h_a + h_c + h_d], mask=moba_mask, mask_blk=MOBA_BLOCK)

    lam_init = 0.8 - 0.6 * math.exp(-0.3 * layer)
    flat = lambda a: a.reshape(M, a.shape[-1])
    o = _combine(flat(o_cmp), flat(o_sel), flat(o_win), flat(proj), flat(o_b), flat(o_c[0]), flat(o_c[1]),
                 flat(o_d), prm["mix_gain"], prm["diff_lambda"], layer, tm=tiles["comb_tm"], lam_init=lam_init)

    if T % tiles["w_out"]["tm"] == 0:
        gates = dict(gate1=gt1, gate2=gt2, rows_per_batch=T)
    else:
        gates = dict(gate1=jnp.repeat(gt1[:, 0], T, axis=0), gate2=jnp.repeat(gt2[:, 0], T, axis=0), rows_per_batch=None)
    x = _matmul(o, prm["w_out"], layer, res=x, gate=gates["gate1"], rows_per_batch=gates["rows_per_batch"],
                **tiles["w_out"])
    h2 = _norm(x, prm["norm_ffn"], layer, tm=tm_n, sc=sc2, sh=sh2, rows_per_batch=T, out_dtype=BF16)
    act = _matmul(h2, prm["w_gate"], layer, w2=prm["w_up"], out_dtype=BF16, **tiles["w_ff"])
    x = _matmul(act, prm["w_down"], layer, res=x, gate=gates["gate2"], rows_per_batch=gates["rows_per_batch"],
                **tiles["w_down"])
    return x, new


def _group_tables(t5_table, dims, T, q_off, tiles, tk_pad, win_len):
    h_a, hkv_a, h_b, hkv_b, h_c, hkv_c, h_d, hkv_d = dims
    tab_t = t5_table.astype(F32).T
    tab_a = tab_t[:h_a]
    tq, tk = tiles["tq"], tiles["tk"]
    Tk = q_off + T
    n = (Tk - CMP_LEN) // CMP_STRIDE + 2
    n_sel = -(-Tk // SEL_BLOCK)
    tabs = {"overlap": jnp.asarray(_overlap_np(n, n_sel, _round_up(n_sel, LANE)))}
    bd = jnp.concatenate([_dist_table(tab_t), _dist_table(None)], axis=0)
    n_bias = tab_t.shape[0]
    if q_off == 0:
        assert tq == tk
        tw = tiles["win_t"]
        nband = (WINDOW - 1 + tw - 1) // tw + 1
        tabs.update(cmp=_cmp_bias_prompt(bd[:h_a], T, n), causal=_prompt_bias(bd, tq, 3),
                    win=_prompt_bias(bd[:h_a], tw, min(nband, T // tw), WINDOW))
    else:
        assert win_len % 8 == 0 and q_off >= win_len - T
        tabs.update(cmp=_cmp_bias(tab_a, q_off + np.arange(T), n), causal=_decode_bias(bd, T, q_off, tk_pad, tk),
                    win=_decode_bias(bd[:h_a], T, win_len - T, win_len, win_len, WINDOW))
    tabs.update(a=(tabs["causal"], 0, False), c=(tabs["causal"], h_a, False), d=(tabs["causal"], h_a + h_c, False),
                mask=(tabs["causal"], n_bias, True))
    return tabs


PROMPT_TILES = dict(
    tq=512, tk=512, win_t=512, cmp_tq=512, norm_tm=256, comb_tm=256,
    w_in=dict(tm=1024, tn=512, tk=4096), w_out=dict(tm=1024, tn=512, tk=4096),
    w_ff=dict(tm=1024, tn=256, tk=4096), w_down=dict(tm=2048, tn=512, tk=1024),
)


def _sample_tiles(M, T, tk):
    return dict(
        tq=T, tk=tk, cmp_tq=T, norm_tm=T, comb_tm=M,
        w_in=dict(tm=M, tn=512, tk=4096), w_out=dict(tm=M, tn=512, tk=4096),
        w_ff=dict(tm=M, tn=256, tk=4096), w_down=dict(tm=M, tn=1024, tk=1024),
    )


def kernel(x_prompt, x_sample, cache_nsa, state_nsa_win, cache_fox, cache_fox_logf, cache_diff, cache_moba,
           page_table, c_prompt, c_sample, t5_table, ada_w, ada_b, norm_attn, norm_ffn, w_in, w_out, mix_gain,
           nsa_cmp_pos, nsa_cmp_w1, nsa_cmp_w2, fox_fbias, diff_lambda, w_gate, w_up, w_down, final_norm):
    depth = w_in.shape[0]
    B, T, D = x_prompt.shape
    Bs, Ts, _ = x_sample.shape
    hkv_a, hkv_b, hkv_c, hkv_d = cache_nsa.shape[4], cache_fox.shape[4], cache_diff.shape[4], cache_moba.shape[4]
    h_b, h_c = fox_fbias.shape[1], diff_lambda.shape[1]
    h_a = (D // DH - h_b - 2 * h_c) // 2
    h_d = h_a
    dims = (h_a, hkv_a, h_b, hkv_b, h_c, hkv_c, h_d, hkv_d)
    page = cache_nsa.shape[2]
    n_pages = page_table.shape[1]
    past_len = n_pages * page
    pages_per_step = math.gcd(PAGED_PER_STEP, n_pages)
    tk_s = pages_per_step * page
    tk_pad = (n_pages // pages_per_step + 1) * tk_s
    win_len = state_nsa_win.shape[2] + Ts

    p_tiles = dict(PROMPT_TILES, wbuf=state_nsa_win.shape[2])
    s_tiles = _sample_tiles(Bs * Ts, Ts, tk_s)
    tabs_p = _group_tables(t5_table, dims, T, 0, p_tiles, T, 0)
    tabs_s = _group_tables(t5_table, dims, Ts, past_len, s_tiles, tk_pad, win_len)

    def piece_rows(c):
        d, n_phys, pg, n_t, hkv, w = c.shape
        halves = w // LANE
        c = c.reshape(d, n_phys, pg, n_t, hkv, halves, LANE).transpose(0, 1, 2, 3, 5, 4, 6)
        info = dict(n=n_t * hkv * halves, n_t=n_t, hkv=hkv, halves=halves,
                    of=lambda t, h, half=0: (t * halves + half) * hkv + h)
        return c.reshape(d, n_phys, pg * info["n"], LANE), info

    rows = {name: piece_rows(c) for name, c in
            (("nsa", cache_nsa), ("fox", cache_fox), ("diff", cache_diff), ("moba", cache_moba))}
    past = dict({name: r[0] for name, r in rows.items()}, pieces={name: r[1] for name, r in rows.items()},
                logf=cache_fox_logf,
                win=state_nsa_win.reshape(depth, Bs, state_nsa_win.shape[2], -1),
                pt=page_table, len=past_len, page=page)

    n_c = _round_up(B + Bs, 8)
    c_all = jnp.pad(jnp.concatenate([c_prompt, c_sample], axis=0), ((0, n_c - B - Bs), (0, 0)))
    xp, xs = x_prompt.reshape(B * T, D), x_sample.reshape(Bs * Ts, D)
    names = ("nsa", "win", "fox", "logf", "diff", "moba")
    st_p = {n: [] for n in names}
    st_s = {n: [] for n in names}
    r3 = lambda a: a.reshape(a.shape[0], 1, a.shape[1])
    for l in range(depth):
        mod = _matmul(c_all, ada_w, l, bias=ada_b[l][None], silu_in=True, tm=n_c, tn=512, tk=D)
        prm = dict(norm_attn=r3(norm_attn), norm_ffn=r3(norm_ffn), w_in_r=_reorder_w_in(w_in[l], dims), w_out=w_out,
                   mix_gain=r3(mix_gain), cmp_pos=nsa_cmp_pos, cmp_w1=nsa_cmp_w1, cmp_w2=nsa_cmp_w2,
                   fox_fbias=fox_fbias[l][None], diff_lambda=diff_lambda, w_gate=w_gate, w_up=w_up, w_down=w_down)
        xp, new_p = _layer(xp, B, T, mod[:B], l, None, prm, tabs_p, dims, p_tiles)
        xs, new_s = _layer(xs, Bs, Ts, mod[B:B + Bs], l, past, prm, tabs_s, dims, s_tiles)
        for n in names:
            st_p[n].append(new_p[n])
            st_s[n].append(new_s[n])
    y_p = _norm(xp, final_norm[None], None, tm=p_tiles["norm_tm"]).reshape(B, T, D)
    y_s = _norm(xs, final_norm[None], None, tm=s_tiles["norm_tm"]).reshape(Bs, Ts, D)

    def stack(st, name, tail):
        a = jnp.stack(st[name])
        return a.reshape(a.shape[:3] + tail)

    kv = lambda n, hk, w: (n, hk, w)
    shapes = {"nsa": kv(4, hkv_a, DH), "win": kv(2, hkv_a, DH), "fox": kv(2, hkv_b, DH), "logf": (h_b,),
              "diff": kv(2, hkv_c, 2 * DH), "moba": kv(2, hkv_d, DH)}
    out = [y_p, y_s]
    for n in names:
        out += [stack(st_p, n, shapes[n]), stack(st_s, n, shapes[n])]
    return tuple(out)
```

```python
import functools
import math

import jax
import jax.numpy as jnp
import numpy as np
from jax import lax
from jax.experimental import pallas as pl
from jax.experimental.pallas import tpu as pltpu

DH = 128
CMP_LEN = 32
CMP_STRIDE = 16
SEL_BLOCK = 64
SEL_TOP = 16
WINDOW = 512
MOBA_BLOCK = 256
MOBA_TOP = 3
N_BUCKETS = 32
MAX_DIST = 128
NEG = -1e30
FORCE = 1e4
RMS_EPS = 1e-6
BELOW_NEG = -3e38

LANE = 128
V7X_VMEM_LIMIT_BYTES = 60000 * 1024

U_QA, U_QB, U_QD, U_QC = 0, 8, 16, 24
U_NSA, U_WIN, U_FOX, U_DIFF, U_MOBA, U_AG, U_BF, U_TOTAL = 32, 40, 44, 48, 56, 60, 61, 62

BF16 = jnp.bfloat16
F32 = jnp.float32


def _cparams(*sem):
    return pltpu.CompilerParams(dimension_semantics=sem, vmem_limit_bytes=V7X_VMEM_LIMIT_BYTES)


def _silu(x):
    return x * jax.nn.sigmoid(x)


def _div_pow2(x, d):
    assert d & (d - 1) == 0
    return lax.shift_right_logical(x, jnp.int32(d.bit_length() - 1))


def _lane_src(arr, unit0, stride=1, jstride=0):
    def make_spec(rows, width, to_brhj):
        def index_map(*g):
            b, r, h, j = to_brhj(*g)
            return (b, r, (unit0 * DH) // width + h * stride + j * jstride)
        return pl.BlockSpec((None, rows, width), index_map)
    return arr, make_spec


def _mm_body(*refs, nk, tk, k_rem, silu_in, dual, has_bias, has_res):
    it = iter(refs)
    x_ref, w_ref = next(it), next(it)
    w2_ref = next(it) if dual else None
    b_ref = next(it) if has_bias else None
    r_ref, g_ref = (next(it), next(it)) if has_res else (None, None)
    o_ref, acc = next(it), next(it)
    acc2 = next(it) if dual else None
    k = pl.program_id(2)

    @pl.when(k == 0)
    def _():
        acc[...] = jnp.zeros_like(acc)
        if dual:
            acc2[...] = jnp.zeros_like(acc2)

    def accumulate(overhang):
        xv = x_ref[...]
        if silu_in:
            xv = _silu(xv.astype(F32))
        xv = xv.astype(BF16)
        wv = w_ref[...].astype(BF16)
        w2v = w2_ref[...].astype(BF16) if dual else None
        if overhang:
            xv = jnp.where(lax.broadcasted_iota(jnp.int32, xv.shape, 1) < k_rem, xv, jnp.zeros_like(xv))
            rows = lax.broadcasted_iota(jnp.int32, wv.shape, 0) < k_rem
            wv = jnp.where(rows, wv, jnp.zeros_like(wv))
            if dual:
                w2v = jnp.where(rows, w2v, jnp.zeros_like(w2v))
        acc[...] += jnp.dot(xv, wv, preferred_element_type=F32)
        if dual:
            acc2[...] += jnp.dot(xv, w2v, preferred_element_type=F32)

    if k_rem:
        pl.when(k < nk - 1)(lambda: accumulate(False))
        pl.when(k == nk - 1)(lambda: accumulate(True))
    else:
        accumulate(False)

    @pl.when(k == nk - 1)
    def _():
        r = acc[...]
        if dual:
            r = _silu(r) * acc2[...]
        if has_bias:
            r = r + b_ref[...]
        if has_res:
            r = r_ref[...] + g_ref[...] * r
        o_ref[...] = r.astype(o_ref.dtype)


def _matmul(x, w, layer, *, tm, tn, tk, w2=None, bias=None, res=None, gate=None, rows_per_batch=None,
            silu_in=False, out_dtype=F32):
    M, K = x.shape
    N = w.shape[-1]
    assert M % tm == 0
    nk = pl.cdiv(K, tk)
    k_rem = K % tk
    grid = (M // tm, pl.cdiv(N, tn), nk)
    if layer is None:
        w_spec = pl.BlockSpec((tk, tn), lambda m, n, k: (k, n))
    else:
        w_spec = pl.BlockSpec((None, tk, tn), lambda m, n, k: (layer, k, n))
    in_specs = [pl.BlockSpec((tm, tk), lambda m, n, k: (m, k)), w_spec]
    args = [x, w]
    if w2 is not None:
        in_specs.append(w_spec)
        args.append(w2)
    if bias is not None:
        in_specs.append(pl.BlockSpec((1, tn), lambda m, n, k: (0, n)))
        args.append(bias)
    if res is not None:
        in_specs.append(pl.BlockSpec((tm, tn), lambda m, n, k: (m, n)))
        args.append(res)
        if gate.ndim == 2:
            in_specs.append(pl.BlockSpec((tm, tn), lambda m, n, k: (m, n)))
        else:
            assert rows_per_batch % tm == 0
            per = rows_per_batch // tm
            in_specs.append(pl.BlockSpec((None, 1, tn), lambda m, n, k: (m // per, 0, n)))
        args.append(gate)
    scratch = [pltpu.VMEM((tm, tn), F32)] * (2 if w2 is not None else 1)
    body = functools.partial(_mm_body, nk=nk, tk=tk, k_rem=k_rem, silu_in=silu_in, dual=w2 is not None,
                             has_bias=bias is not None, has_res=res is not None)
    return pl.pallas_call(
        body, grid=grid, in_specs=in_specs,
        out_specs=pl.BlockSpec((tm, tn), lambda m, n, k: (m, n)),
        out_shape=jax.ShapeDtypeStruct((M, N), out_dtype),
        scratch_shapes=scratch,
        compiler_params=_cparams("parallel", "parallel", "arbitrary"),
    )(*args)


def _norm_body(*refs, modulated):
    if modulated:
        x_ref, g_ref, sc_ref, sh_ref, o_ref = refs
    else:
        x_ref, g_ref, o_ref = refs
    x = x_ref[...]
    y = x * lax.rsqrt(jnp.mean(x * x, axis=-1, keepdims=True) + RMS_EPS) * g_ref[...]
    if modulated:
        y = y * (1.0 + sc_ref[...]) + sh_ref[...]
    o_ref[...] = y.astype(o_ref.dtype)


def _norm(x, g, layer, *, tm, sc=None, sh=None, rows_per_batch=None, out_dtype=F32):
    M, D = x.shape
    modulated = sc is not None
    if layer is None:
        g_spec = pl.BlockSpec((1, D), lambda m: (0, 0))
    else:
        g_spec = pl.BlockSpec((None, 1, D), lambda m: (layer, 0, 0))
    in_specs = [pl.BlockSpec((tm, D), lambda m: (m, 0)), g_spec]
    args = [x, g]
    if modulated:
        assert rows_per_batch % tm == 0
        per = rows_per_batch // tm
        mod_spec = pl.BlockSpec((None, 1, D), lambda m: (m // per, 0, 0))
        in_specs += [mod_spec, mod_spec]
        args += [sc, sh]
    return pl.pallas_call(
        functools.partial(_norm_body, modulated=modulated), grid=(M // tm,), in_specs=in_specs,
        out_specs=pl.BlockSpec((tm, D), lambda m: (m, 0)),
        out_shape=jax.ShapeDtypeStruct((M, D), out_dtype),
        compiler_params=_cparams("parallel"),
    )(*args)


GATHER_PAGES_PER_STEP = 8


def _gather_body(pt_ref, *refs, n_groups, per, page, slots, n_pieces):
    cache_refs, new_ref, o_ref = refs[:per], refs[per], refs[per + 1]
    j = pl.program_id(1)

    @pl.when(j < n_groups)
    def _():
        for p in range(per):
            if slots is None:
                o_ref[p * page:(p + 1) * page] = cache_refs[p][...]
            else:
                for s, src_slot in enumerate(slots):
                    o_ref[p * page:(p + 1) * page, s * LANE:(s + 1) * LANE] = (
                        cache_refs[p][pl.ds(src_slot, page, stride=n_pieces), :])

    @pl.when(j == n_groups)
    def _():
        o_ref[0:page] = new_ref[...]
        if per > 1:
            o_ref[page:per * page] = jnp.zeros(((per - 1) * page,) + o_ref.shape[1:], o_ref.dtype)


def _gather_pages(cache, layer, page_table, new_page, slots=None, n_pieces=1):
    page, W = new_page.shape[1:]
    B, n_pages = page_table.shape
    per = math.gcd(GATHER_PAGES_PER_STEP, n_pages)
    n_groups = n_pages // per
    assert cache.shape[2:] == ((page, W) if slots is None else (page * n_pieces, LANE))
    assert slots is None or W == len(slots) * LANE

    def page_spec(p):
        return pl.BlockSpec((None, None) + cache.shape[2:],
                            lambda b, j, pt: (layer, pt[b, jnp.minimum(j, n_groups - 1) * per + p], 0, 0))

    grid_spec = pltpu.PrefetchScalarGridSpec(
        num_scalar_prefetch=1, grid=(B, n_groups + 1),
        in_specs=[page_spec(p) for p in range(per)] + [pl.BlockSpec((None, page, W), lambda b, j, pt: (b, 0, 0))],
        out_specs=pl.BlockSpec((None, per * page, W), lambda b, j, pt: (b, j, 0)),
    )
    return pl.pallas_call(
        functools.partial(_gather_body, n_groups=n_groups, per=per, page=page, slots=slots, n_pieces=n_pieces),
        grid_spec=grid_spec,
        out_shape=jax.ShapeDtypeStruct((B, (n_pages + 1) * page, W), cache.dtype),
        compiler_params=_cparams("parallel", "arbitrary"),
    )(page_table, *([cache] * per), new_page)


CUM_CHUNK = 256


def _cum_body(*refs, n_past, n_new, nh):
    if n_past:
        past_ref, raw_ref, fb_ref, lf_ref, cum_ref = refs
    else:
        raw_ref, fb_ref, lf_ref, cum_ref = refs
    z = raw_ref[...][:, :nh] + fb_ref[...]
    lf = -(jnp.maximum(-z, 0.0) + jnp.log1p(jnp.exp(-jnp.abs(z))))
    lf_ref[...] = lf

    def tri(n):
        return (lax.broadcasted_iota(jnp.int32, (n, n), 0) >= lax.broadcasted_iota(jnp.int32, (n, n), 1)).astype(F32)

    def scan_rows(src_ref, dst_off, n, carry):
        c = min(CUM_CHUNK, n)
        assert n % c == 0
        t = tri(c)
        for i in range(n // c):
            cum_ref[dst_off + i * c:dst_off + (i + 1) * c, :] = jnp.dot(
                t, src_ref[i * c:(i + 1) * c, :], preferred_element_type=F32, precision=lax.Precision.HIGHEST)
        for i in range(n // c):
            rows = slice(dst_off + i * c, dst_off + (i + 1) * c)
            total = cum_ref[dst_off + (i + 1) * c - 1:dst_off + (i + 1) * c, :]
            cum_ref[rows, :] = cum_ref[rows, :] + carry
            carry = carry + total
        return carry

    carry = jnp.zeros((1, nh), F32)
    if n_past:
        carry = scan_rows(past_ref, 0, n_past, carry)
    scan_rows(lf_ref, n_past, n_new, carry)


def _fox_cum(raw_f, fbias, past_lf, n_past=0):
    B, Tn, _ = raw_f.shape
    nh = fbias.shape[-1]
    in_specs, args = [], []
    if n_past:
        in_specs.append(pl.BlockSpec((None, n_past, nh), lambda b: (b, 0, 0)))
        args.append(past_lf)
    in_specs += [pl.BlockSpec((None, Tn, LANE), lambda b: (b, 0, 0)), pl.BlockSpec((1, nh), lambda b: (0, 0))]
    args += [raw_f, fbias]
    return pl.pallas_call(
        functools.partial(_cum_body, n_past=n_past, n_new=Tn, nh=nh), grid=(B,), in_specs=in_specs,
        out_specs=[pl.BlockSpec((None, Tn, nh), lambda b: (b, 0, 0)),
                   pl.BlockSpec((None, n_past + Tn, nh), lambda b: (b, 0, 0))],
        out_shape=[jax.ShapeDtypeStruct((B, Tn, nh), F32), jax.ShapeDtypeStruct((B, n_past + Tn, nh), F32)],
        compiler_params=_cparams("parallel"),
    )(*args)


def _gelu_tanh(x):
    return 0.5 * x * (1.0 + jnp.tanh(math.sqrt(2.0 / math.pi) * (x + 0.044715 * (x * x * x))))


def _compress_body(x_ref, pos_ref, w1_ref, w2_ref, o_ref, *, n):
    hidden = w1_ref.shape[-1]
    acc_lo = jnp.zeros((n, hidden), F32)
    acc_hi = jnp.zeros((n, hidden), F32)
    for rho in range(CMP_STRIDE):
        xr = x_ref[pl.ds(rho, n, stride=CMP_STRIDE), :]
        lo = (xr + pos_ref[rho:rho + 1, :]).astype(BF16)
        hi = (xr + pos_ref[rho + CMP_STRIDE:rho + CMP_STRIDE + 1, :]).astype(BF16)
        w_lo = w1_ref[rho * DH:(rho + 1) * DH, :].astype(BF16)
        w_hi = w1_ref[(rho + CMP_STRIDE) * DH:(rho + CMP_STRIDE + 1) * DH, :].astype(BF16)
        acc_lo += jnp.dot(lo, w_lo, preferred_element_type=F32)
        acc_hi += jnp.dot(hi, w_hi, preferred_element_type=F32)
    hid = acc_lo + pltpu.roll(acc_hi, n - 1, 0)
    o_ref[...] = jnp.dot(_gelu_tanh(hid).astype(BF16), w2_ref[...].astype(BF16), preferred_element_type=F32)


def _compress(src, hkv, n, layer, pos, w1, w2):
    kv, make_spec = src
    B = kv.shape[0]
    hidden = w1.shape[-1]
    return pl.pallas_call(
        functools.partial(_compress_body, n=n), grid=(B, 2, hkv),
        in_specs=[
            make_spec(n * CMP_STRIDE, DH, lambda b, j, h: (b, 0, h, j)),
            pl.BlockSpec((None, None, CMP_LEN, DH), lambda b, j, h: (layer, j, 0, 0)),
            pl.BlockSpec((None, None, CMP_LEN * DH, hidden), lambda b, j, h: (layer, j, 0, 0)),
            pl.BlockSpec((None, None, hidden, DH), lambda b, j, h: (layer, j, 0, 0)),
        ],
        out_specs=pl.BlockSpec((None, None, None, n, DH), lambda b, j, h: (b, j, h, 0, 0)),
        out_shape=jax.ShapeDtypeStruct((B, 2, hkv, n, DH), F32),
        compiler_params=_cparams("parallel", "arbitrary", "arbitrary"),
    )(kv, pos, w1, w2)


def _take_top(score, lane_ids, count):
    sel = jnp.zeros(score.shape, F32)
    taken = []
    big = jnp.int32(score.shape[-1])
    for _ in range(count):
        m = jnp.max(score, axis=-1, keepdims=True)
        idx = jnp.min(jnp.where(score == m, lane_ids, big), axis=-1, keepdims=True)
        hit = lane_ids == idx
        sel = jnp.where(hit, 1.0, sel)
        score = jnp.where(hit, BELOW_NEG, score)
        taken.append((hit, m))
    return sel, taken


def _cmp_body(q_ref, kc_ref, vc_ref, bias_ref, ov_ref, o_ref, sel_ref, *, G, tq, q_off, n_sel, top):
    qi = pl.program_id(2)
    kc = kc_ref[...]
    vc = vc_ref[...].astype(BF16)
    scale = DH ** -0.5
    psum = None
    outs = []
    for g in range(G):
        qg = q_ref[:, g * DH:(g + 1) * DH]
        s = lax.dot_general(qg, kc, (((1,), (1,)), ((), ())), preferred_element_type=F32,
                            precision=lax.Precision.HIGHEST) * scale + bias_ref[g]
        m = jnp.max(s, axis=-1, keepdims=True)
        e = jnp.where(s > NEG / 2, jnp.exp(s - m), 0.0)
        p = e / jnp.maximum(jnp.sum(e, axis=-1, keepdims=True), 1e-30)
        outs.append(jnp.dot(p.astype(BF16), vc, preferred_element_type=F32))
        psum = p if psum is None else psum + p
    o_ref[...] = jnp.concatenate(outs, axis=1)
    imp = jnp.dot(psum, ov_ref[...], preferred_element_type=F32, precision=lax.Precision.HIGHEST)
    shape = imp.shape
    jb = lax.broadcasted_iota(jnp.int32, shape, 1)
    qpos = q_off + qi * tq + lax.broadcasted_iota(jnp.int32, shape, 0)
    cur = _div_pow2(qpos, SEL_BLOCK)
    forced = (jb == 0) | (jb == cur) | (jb == cur - 1)
    score = jnp.where(jb <= cur, imp + jnp.where(forced, FORCE, 0.0), NEG)
    score = jnp.where(jb < n_sel, score, BELOW_NEG)
    sel, _ = _take_top(score, jb, top)
    sel_ref[...] = jnp.where((jb <= cur) & (sel > 0.5), 0.0, NEG)


def _nsa_cmp(q, u_q, cmp_kv, bias, overlap, *, hkv, G, tq, q_off, n_sel):
    B, Tq, _ = q.shape
    n = cmp_kv.shape[3]
    nsp = overlap.shape[1]
    top = min(SEL_TOP, n_sel)
    body = functools.partial(_cmp_body, G=G, tq=tq, q_off=q_off, n_sel=n_sel, top=top)
    return pl.pallas_call(
        body, grid=(B, hkv, Tq // tq),
        in_specs=[
            pl.BlockSpec((None, tq, G * DH), lambda b, h, i: (b, i, u_q // G + h)),
            pl.BlockSpec((None, None, None, n, DH), lambda b, h, i: (b, 0, h, 0, 0)),
            pl.BlockSpec((None, None, None, n, DH), lambda b, h, i: (b, 1, h, 0, 0)),
            pl.BlockSpec((G, tq, n), lambda b, h, i: (h, i, 0)),
            pl.BlockSpec((n, nsp), lambda b, h, i: (0, 0)),
        ],
        out_specs=[pl.BlockSpec((None, tq, G * DH), lambda b, h, i: (b, i, h)),
                   pl.BlockSpec((None, None, tq, nsp), lambda b, h, i: (b, h, i, 0))],
        out_shape=[jax.ShapeDtypeStruct((B, Tq, hkv * G * DH), F32),
                   jax.ShapeDtypeStruct((B, hkv, Tq, nsp), F32)],
        compiler_params=_cparams("parallel", "parallel", "arbitrary"),
    )(q, cmp_kv, cmp_kv, bias, overlap)


def _moba_gate_body(q_ref, k_ref, m_ref, *, G, n_full, nbp, q_off, top):
    Tq = q_ref.shape[0]
    shape = (Tq, nbp)
    jb = lax.broadcasted_iota(jnp.int32, shape, 1)
    own = _div_pow2(q_off + lax.broadcasted_iota(jnp.int32, shape, 0), MOBA_BLOCK)
    if n_full > 0:
        kmean = jnp.sum(k_ref[...].reshape(n_full, MOBA_BLOCK, DH), axis=1) * (1.0 / MOBA_BLOCK)
        if nbp > n_full:
            kmean = jnp.concatenate([kmean, jnp.zeros((nbp - n_full, DH), F32)], axis=0)
    for g in range(G):
        mask = jnp.where(jb == own, 0.0, NEG)
        if n_full > 0:
            gate = lax.dot_general(q_ref[:, g * DH:(g + 1) * DH], kmean, (((1,), (1,)), ((), ())),
                                   preferred_element_type=F32, precision=lax.Precision.HIGHEST)
            score = jnp.where((jb < own) & (jb < n_full), gate, NEG)
            score = jnp.where(jb < n_full, score, BELOW_NEG)
            _, taken = _take_top(score, jb, top)
            for hit, val in taken:
                mask = jnp.where(hit & (val > NEG / 2), 0.0, mask)
        m_ref[g] = mask


def _moba_gate(q, u_q, ksrc, *, hkv, G, Tk, q_off, nbp):
    k, k_spec = ksrc
    B, Tq, _ = q.shape
    n_full = Tk // MOBA_BLOCK
    top = min(MOBA_TOP, n_full)
    body = functools.partial(_moba_gate_body, G=G, n_full=n_full, nbp=nbp, q_off=q_off, top=top)
    return pl.pallas_call(
        body, grid=(B, hkv),
        in_specs=[pl.BlockSpec((None, Tq, G * DH), lambda b, h: (b, 0, u_q // G + h)),
                  k_spec(max(n_full, 1) * MOBA_BLOCK, DH, lambda b, h: (b, 0, h, 0))],
        out_specs=pl.BlockSpec((None, None, G, Tq, nbp), lambda b, h: (b, h, 0, 0, 0)),
        out_shape=jax.ShapeDtypeStruct((B, hkv, G, Tq, nbp), F32),
        compiler_params=_cparams("parallel", "arbitrary"),
    )(q, k)


def _flash_body(*refs, G, Gm, tq, tk, dv, mode, nsteps, fox, mask_blk):
    it = iter(refs)
    q_ref, k_ref, v_ref, bias_ref = next(it), next(it), next(it), next(it)
    cq_ref, ck_ref = (next(it), next(it)) if fox else (None, None)
    mask_ref = next(it) if mask_blk else None
    o_ref, m_sc, acc_sc = next(it), next(it), next(it)
    qi, step = pl.program_id(2), pl.program_id(3)
    if mode == "band":
        ki = qi - (nsteps - 1) + step
        active = ki >= 0
    elif mode == "causal":
        ki = step
        active = ki <= qi
    else:
        ki = step
        active = step >= 0

    @pl.when(step == 0)
    def _():
        m_sc[...] = jnp.full(m_sc.shape, NEG, F32)
        acc_sc[...] = jnp.zeros_like(acc_sc)

    @pl.when(active)
    def _():
        k = k_ref[...].astype(BF16)
        v = jnp.concatenate([v_ref[...].astype(BF16), jnp.ones((tk, LANE), BF16)], axis=1)
        q = jnp.concatenate([q_ref[:, g * DH:(g + 1) * DH] for g in range(G)], axis=0)
        q = (q * (DH ** -0.5)).astype(BF16)
        s = lax.dot_general(q, k, (((1,), (1,)), ((), ())), preferred_element_type=F32)
        s = s.reshape(G, tq, tk) + bias_ref[...]
        if fox:
            s = s + (cq_ref[...] - ck_ref[...])
        if mask_blk:
            nbp = mask_ref.shape[-1]
            kpos = ki * tk + lax.broadcasted_iota(jnp.int32, (nbp, tk), 1)
            lo = lax.broadcasted_iota(jnp.int32, (nbp, tk), 0) * mask_blk
            expand = jnp.where((kpos >= lo) & (kpos < lo + mask_blk), 1.0, 0.0).astype(BF16)
            picked = jnp.dot(mask_ref[...].reshape(Gm * tq, nbp).astype(BF16), expand, preferred_element_type=F32)
            s = s + picked.reshape(Gm, tq, tk)
        m_old = m_sc[...]
        m_new = jnp.maximum(m_old, jnp.max(s, axis=-1, keepdims=True))
        m_use = jnp.where(m_new < NEG / 2, 0.0, m_new)
        p = jnp.exp((s - m_use).astype(BF16))
        alpha = jnp.exp(m_old - m_use)
        pv = jnp.dot(p.reshape(G * tq, tk), v, preferred_element_type=F32)
        acc_sc[...] = alpha * acc_sc[...] + pv.reshape(G, tq, dv + LANE)
        m_sc[...] = m_new

    @pl.when(step == nsteps - 1)
    def _():
        acc = acc_sc[...]
        o = acc[:, :, :dv] / jnp.maximum(acc[:, :, dv:dv + 1], 1e-30)
        for g in range(G):
            o_ref[:, g * dv:(g + 1) * dv] = o[g]


def _flash(q, u_q, ksrc, vsrc, bias, *, hkv, G, tq, tk, dv, mode, nsteps, fox=None, mask=None, mask_blk=0,
           q_stride=1):
    B, Tq, _ = q.shape
    nq = Tq // tq
    bias, bias_h0, bias_shared = bias
    NB = bias.shape[1]
    Gb = 1 if bias_shared else G
    assert bias_h0 % Gb == 0

    if mode == "causal":
        kidx = lambda i, s: jnp.minimum(s, i)
        bidx = lambda i, s: jnp.clip(i - s, 0, NB - 1)
    elif mode == "band":
        kidx = lambda i, s: jnp.maximum(i - (nsteps - 1) + s, 0)
        bidx = lambda i, s: jnp.clip(nsteps - 1 - s, 0, NB - 1)
    else:
        kidx = lambda i, s: s
        bidx = lambda i, s: s

    kv_index = lambda b, h, i, s: (b, kidx(i, s), h, 0)
    in_specs = [
        pl.BlockSpec((None, tq, G * DH), lambda b, h, i, s: (b, i, u_q // G + h * q_stride)),
        ksrc[1](tk, DH, kv_index),
        vsrc[1](tk, dv, kv_index),
        pl.BlockSpec((Gb, None, tq, tk),
                     lambda b, h, i, s: (bias_h0 // Gb + (0 if bias_shared else h), bidx(i, s), 0, 0)),
    ]
    args = [q, ksrc[0], vsrc[0], bias]
    if fox is not None:
        in_specs += [pl.BlockSpec((None, G, tq, 1), lambda b, h, i, s: (b, h, i, 0)),
                     pl.BlockSpec((None, G, 1, tk), lambda b, h, i, s: (b, h, 0, kidx(i, s)))]
        args += list(fox)
    Gm = 0
    if mask is not None:
        Gm, nbp = mask.shape[2], mask.shape[4]
        in_specs.append(pl.BlockSpec((None, None, Gm, tq, nbp), lambda b, h, i, s: (b, h, 0, i, 0)))
        args.append(mask)
    body = functools.partial(_flash_body, G=G, Gm=Gm, tq=tq, tk=tk, dv=dv, mode=mode, nsteps=nsteps,
                             fox=fox is not None, mask_blk=mask_blk if mask is not None else 0)
    return pl.pallas_call(
        body, grid=(B, hkv, nq, nsteps), in_specs=in_specs,
        out_specs=pl.BlockSpec((None, tq, G * dv), lambda b, h, i, s: (b, i, h)),
        out_shape=jax.ShapeDtypeStruct((B, Tq, hkv * G * dv), F32),
        scratch_shapes=[pltpu.VMEM((G, tq, 1), F32), pltpu.VMEM((G, tq, dv + LANE), F32)],
        compiler_params=_cparams("parallel", "parallel", "parallel", "arbitrary"),
    )(*args)


PAGED_PER_STEP = 8


def _flash_paged_body(pt_ref, *refs, groups, n_out, per, page, n_pieces, n_groups, dv, fox, has_mask, mask_blk):
    it = iter(refs)
    q_ref = next(it)
    page_refs = [next(it) for _ in range(per)]
    new_ref, bias_ref = next(it), next(it)
    cq_ref, ck_ref = (next(it), next(it)) if fox else (None, None)
    mask_ref = next(it) if has_mask else None
    o_refs = [next(it) for _ in range(n_out)]
    m_sc, acc_sc = next(it), next(it)
    j = pl.program_id(1)
    last = j == n_groups
    tk = per * page
    tq = q_ref.shape[0]

    @pl.when(j == 0)
    def _():
        m_sc[...] = jnp.full(m_sc.shape, NEG, F32)
        acc_sc[...] = jnp.zeros_like(acc_sc)

    loaded = {}

    def piece(slot):
        if slot not in loaded:
            parts = []
            for p in range(per):
                x = page_refs[p][pl.ds(slot, page, stride=n_pieces), :]
                if p == 0:
                    x = jnp.where(last, new_ref[pl.ds(slot, page, stride=n_pieces), :], x)
                parts.append(x)
            loaded[slot] = jnp.concatenate(parts, axis=0).astype(BF16)
        return loaded[slot]

    expand = None
    if has_mask:
        nbp = mask_ref.shape[-1]
        kpos = j * tk + lax.broadcasted_iota(jnp.int32, (nbp, tk), 1)
        lo = lax.broadcasted_iota(jnp.int32, (nbp, tk), 0) * mask_blk
        expand = jnp.where((kpos >= lo) & (kpos < lo + mask_blk), 1.0, 0.0).astype(BF16)

    ones = jnp.ones((tk, LANE), BF16)
    row = 0
    for grp in groups:
        G = len(grp["q_units"])
        k = piece(grp["k_piece"])
        v = jnp.concatenate([piece(s) for s in grp["v_pieces"]] + [ones], axis=1)
        q = jnp.concatenate([q_ref[:, u * DH:(u + 1) * DH] for u in grp["q_units"]], axis=0)
        q = (q * (DH ** -0.5)).astype(BF16)
        s = lax.dot_general(q, k, (((1,), (1,)), ((), ())), preferred_element_type=F32).reshape(G, tq, tk)
        b0 = grp["bias_head"]
        s = s + (bias_ref[...] if b0 is None else bias_ref[b0:b0 + G])
        if fox:
            f0 = grp["fox_head"]
            s = s + (cq_ref[f0:f0 + G] - ck_ref[f0:f0 + G])
        if has_mask:
            msk = mask_ref[grp["mask_head"]]
            gm = msk.shape[0]
            picked = jnp.dot(msk.reshape(gm * tq, nbp).astype(BF16), expand, preferred_element_type=F32)
            s = s + picked.reshape(gm, tq, tk)
        rows = slice(row, row + G)
        m_old = m_sc[rows]
        m_new = jnp.maximum(m_old, jnp.max(s, axis=-1, keepdims=True))
        m_use = jnp.where(m_new < NEG / 2, 0.0, m_new)
        p = jnp.exp((s - m_use).astype(BF16))
        alpha = jnp.exp(m_old - m_use)
        pv = jnp.dot(p.reshape(G * tq, tk), v, preferred_element_type=F32)
        acc_sc[rows] = alpha * acc_sc[rows] + pv.reshape(G, tq, dv + LANE)
        m_sc[rows] = m_new
        row += G

    @pl.when(last)
    def _():
        row = 0
        for grp in groups:
            G = len(grp["q_units"])
            acc = acc_sc[row:row + G]
            o = acc[:, :, :dv] / jnp.maximum(acc[:, :, dv:dv + 1], 1e-30)
            out_i, unit0 = grp["out"]
            for g in range(G):
                o_refs[out_i][:, (unit0 + g) * dv:(unit0 + g + 1) * dv] = o[g]
            row += G


def _flash_paged(q, q_block, groups, out_heads, cache, n_pieces, layer, page_table, new_page, bias, *, dv,
                 fox=None, mask=None, mask_blk=0):
    B, Tq, _ = q.shape
    n_pages = page_table.shape[1]
    page = cache.shape[2] // n_pieces
    per = math.gcd(PAGED_PER_STEP, n_pages)
    n_groups = n_pages // per
    tk = per * page
    assert bias.shape[1:] == (n_groups + 1, Tq, tk)
    n_rows = sum(len(g["q_units"]) for g in groups)
    q_w = 8 * DH

    def page_spec(p):
        return pl.BlockSpec((None, None, page * n_pieces, LANE),
                            lambda b, j, pt: (layer, pt[b, jnp.minimum(j, n_groups - 1) * per + p], 0, 0))

    in_specs = [pl.BlockSpec((None, Tq, q_w), lambda b, j, pt: (b, 0, q_block))]
    in_specs += [page_spec(p) for p in range(per)]
    in_specs += [pl.BlockSpec((None, page * n_pieces, LANE), lambda b, j, pt: (b, 0, 0)),
                 pl.BlockSpec((bias.shape[0], None, Tq, tk), lambda b, j, pt: (0, j, 0, 0))]
    args = [q] + [cache] * per + [new_page, bias]
    if fox is not None:
        H = fox[0].shape[1]
        in_specs += [pl.BlockSpec((None, H, Tq, 1), lambda b, j, pt: (b, 0, 0, 0)),
                     pl.BlockSpec((None, H, 1, tk), lambda b, j, pt: (b, 0, 0, j))]
        args += list(fox)
    if mask is not None:
        in_specs.append(pl.BlockSpec((None,) + mask.shape[1:], lambda b, j, pt: (b, 0, 0, 0, 0)))
        args.append(mask)
    body = functools.partial(_flash_paged_body, groups=groups, n_out=len(out_heads), per=per, page=page,
                             n_pieces=n_pieces, n_groups=n_groups, dv=dv, fox=fox is not None,
                             has_mask=mask is not None, mask_blk=mask_blk)
    grid_spec = pltpu.PrefetchScalarGridSpec(
        num_scalar_prefetch=1, grid=(B, n_groups + 1), in_specs=in_specs,
        out_specs=[pl.BlockSpec((None, Tq, nh * dv), lambda b, j, pt: (b, 0, 0)) for nh in out_heads],
        scratch_shapes=[pltpu.VMEM((n_rows, Tq, 1), F32), pltpu.VMEM((n_rows, Tq, dv + LANE), F32)],
    )
    return pl.pallas_call(
        body, grid_spec=grid_spec,
        out_shape=[jax.ShapeDtypeStruct((B, Tq, nh * dv), F32) for nh in out_heads],
        compiler_params=_cparams("parallel", "arbitrary"),
    )(page_table, *args)


def _head_rms(x, gain):
    return x * lax.rsqrt(jnp.mean(x * x, axis=-1, keepdims=True) + RMS_EPS) * gain


def _combine_body(cmp_ref, sel_ref, win_ref, ag_ref, b_ref, c1_ref, c2_ref, d_ref, mg_ref, lp_ref, o_ref, *,
                  h_a, h_b, h_c, h_d, lam_init):
    gates = jax.nn.sigmoid(ag_ref[...])
    col = 0
    for h in range(h_a):
        sl = slice(h * DH, (h + 1) * DH)
        o = (gates[:, 3 * h:3 * h + 1] * cmp_ref[:, sl] + gates[:, 3 * h + 1:3 * h + 2] * sel_ref[:, sl]
             + gates[:, 3 * h + 2:3 * h + 3] * win_ref[:, sl])
        o_ref[:, col:col + DH] = _head_rms(o, mg_ref[:, col:col + DH]).astype(o_ref.dtype)
        col += DH
    for h in range(h_b):
        o_ref[:, col:col + DH] = _head_rms(b_ref[:, h * DH:(h + 1) * DH], mg_ref[:, col:col + DH]).astype(o_ref.dtype)
        col += DH
    for h in range(h_c):
        lp = lp_ref[h]
        lam = (jnp.exp(jnp.sum(lp[0:1] * lp[1:2], axis=-1, keepdims=True))
               - jnp.exp(jnp.sum(lp[2:3] * lp[3:4], axis=-1, keepdims=True)) + lam_init)
        sl = slice(h * 2 * DH, (h + 1) * 2 * DH)
        o = c1_ref[:, sl] - lam * c2_ref[:, sl]
        o_ref[:, col:col + 2 * DH] = (_head_rms(o, mg_ref[:, col:col + 2 * DH]) * (1.0 - lam_init)).astype(o_ref.dtype)
        col += 2 * DH
    for h in range(h_d):
        o_ref[:, col:col + DH] = _head_rms(d_ref[:, h * DH:(h + 1) * DH], mg_ref[:, col:col + DH]).astype(o_ref.dtype)
        col += DH


def _combine(o_cmp, o_sel, o_win, proj, o_b, o_c1, o_c2, o_d, mix_gain, diff_lambda, layer, *, tm, lam_init):
    M = o_cmp.shape[0]
    D = mix_gain.shape[-1]
    h_a, h_b, h_d = o_cmp.shape[1] // DH, o_b.shape[1] // DH, o_d.shape[1] // DH
    h_c = o_c1.shape[1] // (2 * DH)
    row = lambda w: pl.BlockSpec((tm, w), lambda m: (m, 0))
    body = functools.partial(_combine_body, h_a=h_a, h_b=h_b, h_c=h_c, h_d=h_d, lam_init=lam_init)
    return pl.pallas_call(
        body, grid=(M // tm,),
        in_specs=[row(o_cmp.shape[1]), row(o_sel.shape[1]), row(o_win.shape[1]),
                  pl.BlockSpec((tm, LANE), lambda m: (m, U_AG)),
                  row(o_b.shape[1]), row(o_c1.shape[1]), row(o_c2.shape[1]), row(o_d.shape[1]),
                  pl.BlockSpec((None, 1, D), lambda m: (layer, 0, 0)),
                  pl.BlockSpec((None,) + diff_lambda.shape[1:], lambda m: (layer, 0, 0, 0))],
        out_specs=pl.BlockSpec((tm, D), lambda m: (m, 0)),
        out_shape=jax.ShapeDtypeStruct((M, D), BF16),
        compiler_params=_cparams("parallel"),
    )(o_cmp, o_sel, o_win, proj, o_b, o_c1, o_c2, o_d, mix_gain, diff_lambda)


def _t5_bucket_np(dist):
    n = np.maximum(dist, 0)
    exact = N_BUCKETS // 2
    nf = np.maximum(n, 1).astype(np.float32)
    big = exact + (np.log(nf / np.float32(exact)) / np.float32(math.log(MAX_DIST / exact))
                   * np.float32(N_BUCKETS - exact)).astype(np.int32)
    return np.where(n < exact, n, np.minimum(big, N_BUCKETS - 1)).astype(np.int32)


def _bias_from_dist(tab_t, dist, valid):
    if tab_t is None:
        return jnp.asarray(np.where(valid, 0.0, NEG).astype(np.float32))[None]
    idx = _t5_bucket_np(dist).reshape(-1)
    b = jnp.take(tab_t, jnp.asarray(idx), axis=1).reshape((tab_t.shape[0],) + dist.shape)
    return jnp.where(jnp.asarray(valid)[None], b, NEG)


def _dist_table(tab_t):
    if tab_t is None:
        return jnp.zeros((1, MAX_DIST + 1), F32)
    return jnp.take(tab_t, jnp.asarray(_t5_bucket_np(np.arange(MAX_DIST + 1))), axis=1)


def _bias_range(bd, lo, hi, window=None, descending=False):
    H, nd = bd.shape
    far = 10 ** 9
    w = far if window is None else window
    assert w >= nd - 1
    parts = []
    for seg_lo, seg_hi, kind in ((-far, 0, "neg"), (0, nd - 1, "tab"), (nd - 1, w, "far"), (w, far, "neg")):
        a, b = max(lo, seg_lo), min(hi, seg_hi)
        if b <= a:
            continue
        if kind == "neg":
            parts.append(jnp.full((H, b - a), NEG, F32))
        elif kind == "tab":
            parts.append(bd[:, ::-1][:, nd - b:nd - a] if descending else bd[:, a:b])
        else:
            parts.append(jnp.broadcast_to(bd[:, nd - 1:], (H, b - a)))
    return jnp.concatenate(parts[::-1] if descending else parts, axis=1)


def _toeplitz_body(vec_ref, o_ref, *, t, stride, transpose, mask_last):
    rows = o_ref.shape[1] if transpose else o_ref.shape[0]
    x = jnp.broadcast_to(vec_ref[...], (rows, vec_ref.shape[-1]))
    x = pltpu.roll(x, 0, 1, stride=stride, stride_axis=0)[:, :t]
    if mask_last:
        x = jnp.where(lax.broadcasted_iota(jnp.int32, x.shape, 0) < rows - 1, x, NEG)
    o_ref[...] = x.T if transpose else x


def _toeplitz(vec, rows, t, stride=1, transpose=False, mask_last=False):
    H, NV, _, L = vec.shape
    out = (t, rows) if transpose else (rows, t)
    body = functools.partial(_toeplitz_body, t=t, stride=stride, transpose=transpose, mask_last=mask_last)
    return pl.pallas_call(
        body, grid=(H, NV),
        in_specs=[pl.BlockSpec((None, None, 1, L), lambda h, v: (h, v, 0, 0))],
        out_specs=pl.BlockSpec((None, None) + out, lambda h, v: (h, v, 0, 0)),
        out_shape=jax.ShapeDtypeStruct((H, NV) + out, F32),
        compiler_params=_cparams("parallel", "parallel"),
    )(vec)


def _prompt_bias(bd, t, n_var, window=None):
    vec = jnp.stack([jnp.concatenate([_bias_range(bd, v * t - t, v * t + 1, window, descending=True),
                                      _bias_range(bd, v * t + 1, v * t + t, window, descending=True)], axis=1)
                     for v in range(n_var)], axis=1)
    return _toeplitz(vec[:, :, None, :], t, t)


def _decode_bias(bd, tq, q_rel, n_keys, tk, window=None):
    H = bd.shape[0]
    vec = _bias_range(bd, q_rel - n_keys + 1, q_rel + tq, window, descending=True)
    rows = jnp.stack([vec[:, tq - 1 - i:tq - 1 - i + n_keys] for i in range(tq)], axis=1)
    return rows.reshape(H, tq, n_keys // tk, tk).transpose(0, 2, 1, 3)


def _cmp_bias_prompt(bd, T, n):
    H = bd.shape[0]
    L = T + CMP_STRIDE * n
    vec = jnp.concatenate([_bias_range(bd, 1 - CMP_LEN, T + 1 - CMP_LEN), jnp.full((H, L - T), NEG, F32)], axis=1)
    return _toeplitz(vec[:, None, None, :], n, T, stride=CMP_STRIDE, transpose=True, mask_last=True)[:, 0]


def _cmp_bias(tab_t, q_pos, n):
    end = np.arange(n) * CMP_STRIDE + CMP_LEN - 1
    dist = q_pos[:, None] - end[None, :]
    valid = (dist >= 0) & (np.arange(n)[None, :] < n - 1)
    return _bias_from_dist(tab_t, dist, valid)


def _overlap_np(n, n_sel, n_sel_pad):
    cs = np.arange(n)[:, None] * CMP_STRIDE
    js = np.arange(n_sel_pad)[None, :] * SEL_BLOCK
    ov = (cs < js + SEL_BLOCK) & (cs + CMP_LEN > js) & (np.arange(n)[:, None] < n - 1) & (np.arange(n_sel_pad)[None, :] < n_sel)
    return ov.astype(np.float32)


def _round_up(x, m):
    return -(-x // m) * m


def _reorder_w_in(w, dims):
    h_a, hkv_a, h_b, hkv_b, h_c, hkv_c, h_d, hkv_d = dims
    widths = [h_a * DH] + [hkv_a * DH] * 6 + [h_a * 3, h_b * DH, hkv_b * DH, hkv_b * DH, h_b,
                                             h_c * 2 * DH, hkv_c * 2 * DH, hkv_c * 2 * DH, h_d * DH, hkv_d * DH, hkv_d * DH]
    offs = np.concatenate([[0], np.cumsum(widths)])
    seg = lambda i: w[:, offs[i]:offs[i + 1]]
    D = w.shape[0]
    g_c = h_c // hkv_c
    c_q = seg(12).reshape(D, hkv_c, g_c, 2, DH).transpose(0, 1, 3, 2, 4).reshape(D, -1)
    pad = lambda a: jnp.pad(a, ((0, 0), (0, LANE - a.shape[1])))
    return jnp.concatenate([seg(0), seg(8), seg(15), c_q, seg(1), seg(2), seg(3), seg(4), seg(5), seg(6),
                            seg(9), seg(10), seg(13), seg(14), seg(16), seg(17), pad(seg(7)), pad(seg(11))], axis=1)


def _layer(x, B, T, mod, layer, past, prm, tabs, dims, tiles):
    h_a, hkv_a, h_b, hkv_b, h_c, hkv_c, h_d, hkv_d = dims
    g_a, g_b, g_c, g_d = h_a // hkv_a, h_b // hkv_b, h_c // hkv_c, h_d // hkv_d
    D = x.shape[1]
    M = B * T
    sh1, sc1, gt1, sh2, sc2, gt2 = [m[:, None, :] for m in jnp.split(mod, 6, axis=-1)]
    tm_n = tiles["norm_tm"]
    h = _norm(x, prm["norm_attn"], layer, tm=tm_n, sc=sc1, sh=sh1, rows_per_batch=T, out_dtype=BF16)
    proj = _matmul(h, prm["w_in_r"], None, **tiles["w_in"]).reshape(B, T, U_TOTAL * LANE)
    new = {
        "nsa": proj[:, :, U_NSA * LANE:U_WIN * LANE], "fox": proj[:, :, U_FOX * LANE:U_DIFF * LANE],
        "diff": proj[:, :, U_DIFF * LANE:U_MOBA * LANE], "moba": proj[:, :, U_MOBA * LANE:U_AG * LANE],
    }
    kw_new = proj[:, :, U_WIN * LANE:U_FOX * LANE]
    tq, tk = tiles["tq"], tiles["tk"]

    if past is None:
        q_off, Tk = 0, T
        src = dict(
            cmp=_lane_src(proj, U_NSA, jstride=hkv_a),
            sel_k=_lane_src(proj, U_NSA + 2 * hkv_a), sel_v=_lane_src(proj, U_NSA + 3 * hkv_a),
            win_k=_lane_src(proj, U_WIN), win_v=_lane_src(proj, U_WIN + hkv_a),
            fox_k=_lane_src(proj, U_FOX), fox_v=_lane_src(proj, U_FOX + hkv_b),
            diff_k=[_lane_src(proj, U_DIFF + half, stride=2) for half in range(2)],
            diff_v=_lane_src(proj, U_DIFF + 2 * hkv_c),
            moba_k=_lane_src(proj, U_MOBA), moba_v=_lane_src(proj, U_MOBA + hkv_d))
        assert T >= tiles["wbuf"]
        new["win"] = kw_new[:, T - tiles["wbuf"]:]
        lf_new, cum = _fox_cum(proj[:, :, U_BF * LANE:], prm["fox_fbias"], None)
        mode, nsteps = "causal", T // tk
        band_steps = tabs["win"].shape[1]
        n_cmp_pad = T // CMP_STRIDE
        tk_pad = T
    else:
        q_off = past["len"]
        Tk = q_off + T
        page = past["page"]

        pieces = past["pieces"]

        def contiguous(name, n_lanes, slots):
            new_page = jnp.pad(new[name][:, :, :n_lanes], ((0, 0), (0, page - T), (0, 0)))
            return _gather_pages(past[name], layer, past["pt"], new_page, slots=slots,
                                 n_pieces=pieces[name]["n"])

        pc = pieces["nsa"]["of"]
        nsa_cmp = contiguous("nsa", 2 * hkv_a * DH, [pc(t, h) for t in range(2) for h in range(hkv_a)])
        pc = pieces["moba"]["of"]
        moba_k = contiguous("moba", hkv_d * DH, [pc(0, h) for h in range(hkv_d)])
        past_lf = _gather_pages(past["logf"], layer, past["pt"], jnp.zeros((B, page, h_b), F32))
        win_kv = jnp.concatenate([past["win"][layer], kw_new], axis=1)
        wbuf = past["win"].shape[2]
        new["win"] = win_kv[:, -wbuf:]
        src = dict(cmp=_lane_src(nsa_cmp, 0, jstride=hkv_a), moba_k=_lane_src(moba_k, 0),
                   win_k=_lane_src(win_kv, 0), win_v=_lane_src(win_kv, hkv_a))
        lf_new, cum = _fox_cum(proj[:, :, U_BF * LANE:], prm["fox_fbias"], past_lf, n_past=q_off)
        n_cmp_pad = q_off // CMP_STRIDE
        tk_pad = tabs["causal"].shape[1] * tabs["causal"].shape[3]

        def paged(name, q_block, groups, out_heads, bias, dv=DH, **kw):
            info = pieces[name]
            rows = new[name].reshape(B, T, info["n_t"], info["hkv"], info["halves"], LANE)
            rows = rows.transpose(0, 1, 2, 4, 3, 5).reshape(B, T * info["n"], LANE)
            new_page = jnp.pad(rows, ((0, 0), (0, (page - T) * info["n"]), (0, 0)))
            return _flash_paged(proj, q_block, groups, out_heads, past[name], info["n"], layer, past["pt"],
                                new_page, bias, dv=dv, **kw)
    new["logf"] = lf_new

    cmp_kv = _compress(src["cmp"], hkv_a, n_cmp_pad, layer, prm["cmp_pos"], prm["cmp_w1"], prm["cmp_w2"])
    n_sel = -(-Tk // SEL_BLOCK)
    o_cmp, sel_mask = _nsa_cmp(proj, U_QA, cmp_kv, tabs["cmp"], tabs["overlap"], hkv=hkv_a, G=g_a,
                               tq=min(tiles["cmp_tq"], T), q_off=q_off, n_sel=n_sel)
    sel_mask = sel_mask[:, :, None]
    cum_t = cum.transpose(0, 2, 1)
    cq = cum_t[:, :, Tk - T:, None]
    ck = jnp.pad(cum_t, ((0, 0), (0, 0), (0, tk_pad - Tk)))[:, :, None, :]
    nb = -(-Tk // MOBA_BLOCK)
    moba_mask = _moba_gate(proj, U_QD, src["moba_k"], hkv=hkv_d, G=g_d, Tk=Tk, q_off=q_off,
                           nbp=_round_up(nb, LANE))
    if past is None:
        o_sel = _flash(proj, U_QA, src["sel_k"], src["sel_v"], tabs["a"], hkv=hkv_a, G=g_a,
                       tq=tq, tk=tk, dv=DH, mode=mode, nsteps=nsteps, mask=sel_mask, mask_blk=SEL_BLOCK)
        o_win = _flash(proj, U_QA, src["win_k"], src["win_v"], (tabs["win"], 0, False), hkv=hkv_a, G=g_a,
                       tq=tabs["win"].shape[2], tk=tabs["win"].shape[3], dv=DH, mode="band", nsteps=band_steps)
        o_b = _flash(proj, U_QB, src["fox_k"], src["fox_v"], tabs["mask"], hkv=hkv_b, G=g_b,
                     tq=tq, tk=tk, dv=DH, mode=mode, nsteps=nsteps, fox=(cq, ck))
        o_c = [_flash(proj, U_QC + half * g_c, src["diff_k"][half], src["diff_v"], tabs["c"],
                      hkv=hkv_c, G=g_c, tq=tq, tk=tk, dv=2 * DH, mode=mode, nsteps=nsteps, q_stride=2)
               for half in range(2)]
        o_d = _flash(proj, U_QD, src["moba_k"], src["moba_v"], tabs["d"], hkv=hkv_d, G=g_d,
                     tq=tq, tk=tk, dv=DH, mode=mode, nsteps=nsteps, mask=moba_mask, mask_blk=MOBA_BLOCK)
    else:
        o_win = _flash(proj, U_QA, src["win_k"], src["win_v"], (tabs["win"], 0, False), hkv=hkv_a, G=g_a,
                       tq=tq, tk=tabs["win"].shape[3], dv=DH, mode="full", nsteps=1)
        causal = tabs["causal"]
        pc = pieces["nsa"]["of"]
        o_sel, = paged("nsa", U_QA // 8, [
            dict(q_units=[h * g_a + g for g in range(g_a)], k_piece=pc(2, h), v_pieces=[pc(3, h)],
                 bias_head=h * g_a, mask_head=h, out=(0, h * g_a)) for h in range(hkv_a)],
            [h_a], causal[:h_a], mask=sel_mask, mask_blk=SEL_BLOCK)
        pc = pieces["fox"]["of"]
        o_b, = paged("fox", U_QB // 8, [
            dict(q_units=[h * g_b + g for g in range(g_b)], k_piece=pc(0, h), v_pieces=[pc(1, h)],
                 bias_head=None, fox_head=h * g_b, out=(0, h * g_b)) for h in range(hkv_b)],
            [h_b], causal[h_a + h_c + h_d:], fox=(cq, ck))
        pc = pieces["diff"]["of"]
        o_c = paged("diff", U_QC // 8, [
            dict(q_units=[(h * 2 + half) * g_c + g for g in range(g_c)], k_piece=pc(0, h, half),
                 v_pieces=[pc(1, h, 0), pc(1, h, 1)], bias_head=h * g_c, out=(half, h * g_c))
            for half in range(2) for h in range(hkv_c)],
            [h_c, h_c], causal[h_a:h_a + h_c], dv=2 * DH)
        pc = pieces["moba"]["of"]
        o_d, = paged("moba", U_QD // 8, [
            dict(q_units=[h * g_d + g for g in range(g_d)], k_piece=pc(0, h), v_pieces=[pc(1, h)],
                 bias_head=h * g_d, mask_head=h, out=(0, h * g_d)) for h in range(hkv_d)],
            [h_d], causal[h_a + h_c:h_a + h_c + h_d], mask=moba_mask, mask_blk=MOBA_BLOCK)

    lam_init = 0.8 - 0.6 * math.exp(-0.3 * layer)
    flat = lambda a: a.reshape(M, a.shape[-1])
    o = _combine(flat(o_cmp), flat(o_sel), flat(o_win), flat(proj), flat(o_b), flat(o_c[0]), flat(o_c[1]),
                 flat(o_d), prm["mix_gain"], prm["diff_lambda"], layer, tm=tiles["comb_tm"], lam_init=lam_init)

    if T % tiles["w_out"]["tm"] == 0:
        gates = dict(gate1=gt1, gate2=gt2, rows_per_batch=T)
    else:
        gates = dict(gate1=jnp.repeat(gt1[:, 0], T, axis=0), gate2=jnp.repeat(gt2[:, 0], T, axis=0), rows_per_batch=None)
    x = _matmul(o, prm["w_out"], layer, res=x, gate=gates["gate1"], rows_per_batch=gates["rows_per_batch"],
                **tiles["w_out"])
    h2 = _norm(x, prm["norm_ffn"], layer, tm=tm_n, sc=sc2, sh=sh2, rows_per_batch=T, out_dtype=BF16)
    act = _matmul(h2, prm["w_gate"], layer, w2=prm["w_up"], out_dtype=BF16, **tiles["w_ff"])
    x = _matmul(act, prm["w_down"], layer, res=x, gate=gates["gate2"], rows_per_batch=gates["rows_per_batch"],
                **tiles["w_down"])
    return x, new


def _group_tables(t5_table, dims, T, q_off, tiles, tk_pad, win_len):
    h_a, hkv_a, h_b, hkv_b, h_c, hkv_c, h_d, hkv_d = dims
    tab_t = t5_table.astype(F32).T
    tab_a = tab_t[:h_a]
    tq, tk = tiles["tq"], tiles["tk"]
    Tk = q_off + T
    n = (Tk - CMP_LEN) // CMP_STRIDE + 2
    n_sel = -(-Tk // SEL_BLOCK)
    tabs = {"overlap": jnp.asarray(_overlap_np(n, n_sel, _round_up(n_sel, LANE)))}
    bd = jnp.concatenate([_dist_table(tab_t), _dist_table(None)], axis=0)
    n_bias = tab_t.shape[0]
    if q_off == 0:
        assert tq == tk
        tw = tiles["win_t"]
        nband = (WINDOW - 1 + tw - 1) // tw + 1
        tabs.update(cmp=_cmp_bias_prompt(bd[:h_a], T, n), causal=_prompt_bias(bd, tq, 3),
                    win=_prompt_bias(bd[:h_a], tw, min(nband, T // tw), WINDOW))
    else:
        assert win_len % 8 == 0 and q_off >= win_len - T
        tabs.update(cmp=_cmp_bias(tab_a, q_off + np.arange(T), n), causal=_decode_bias(bd, T, q_off, tk_pad, tk),
                    win=_decode_bias(bd[:h_a], T, win_len - T, win_len, win_len, WINDOW))
    tabs.update(a=(tabs["causal"], 0, False), c=(tabs["causal"], h_a, False), d=(tabs["causal"], h_a + h_c, False),
                mask=(tabs["causal"], n_bias, True))
    return tabs


PROMPT_TILES = dict(
    tq=512, tk=512, win_t=512, cmp_tq=512, norm_tm=256, comb_tm=256,
    w_in=dict(tm=1024, tn=512, tk=4096), w_out=dict(tm=1024, tn=512, tk=4096),
    w_ff=dict(tm=1024, tn=256, tk=4096), w_down=dict(tm=2048, tn=512, tk=1024),
)


def _sample_tiles(M, T, tk):
    return dict(
        tq=T, tk=tk, cmp_tq=T, norm_tm=T, comb_tm=M,
        w_in=dict(tm=M, tn=512, tk=4096), w_out=dict(tm=M, tn=512, tk=4096),
        w_ff=dict(tm=M, tn=256, tk=4096), w_down=dict(tm=M, tn=1024, tk=1024),
    )


def kernel(x_prompt, x_sample, cache_nsa, state_nsa_win, cache_fox, cache_fox_logf, cache_diff, cache_moba,
           page_table, c_prompt, c_sample, t5_table, ada_w, ada_b, norm_attn, norm_ffn, w_in, w_out, mix_gain,
           nsa_cmp_pos, nsa_cmp_w1, nsa_cmp_w2, fox_fbias, diff_lambda, w_gate, w_up, w_down, final_norm):
    depth = w_in.shape[0]
    B, T, D = x_prompt.shape
    Bs, Ts, _ = x_sample.shape
    hkv_a, hkv_b, hkv_c, hkv_d = cache_nsa.shape[4], cache_fox.shape[4], cache_diff.shape[4], cache_moba.shape[4]
    h_b, h_c = fox_fbias.shape[1], diff_lambda.shape[1]
    h_a = (D // DH - h_b - 2 * h_c) // 2
    h_d = h_a
    dims = (h_a, hkv_a, h_b, hkv_b, h_c, hkv_c, h_d, hkv_d)
    page = cache_nsa.shape[2]
    n_pages = page_table.shape[1]
    past_len = n_pages * page
    pages_per_step = math.gcd(PAGED_PER_STEP, n_pages)
    tk_s = pages_per_step * page
    tk_pad = (n_pages // pages_per_step + 1) * tk_s
    win_len = state_nsa_win.shape[2] + Ts

    p_tiles = dict(PROMPT_TILES, wbuf=state_nsa_win.shape[2])
    s_tiles = _sample_tiles(Bs * Ts, Ts, tk_s)
    tabs_p = _group_tables(t5_table, dims, T, 0, p_tiles, T, 0)
    tabs_s = _group_tables(t5_table, dims, Ts, past_len, s_tiles, tk_pad, win_len)

    def piece_rows(c):
        d, n_phys, pg, n_t, hkv, w = c.shape
        halves = w // LANE
        c = c.reshape(d, n_phys, pg, n_t, hkv, halves, LANE).transpose(0, 1, 2, 3, 5, 4, 6)
        info = dict(n=n_t * hkv * halves, n_t=n_t, hkv=hkv, halves=halves,
                    of=lambda t, h, half=0: (t * halves + half) * hkv + h)
        return c.reshape(d, n_phys, pg * info["n"], LANE), info

    rows = {name: piece_rows(c) for name, c in
            (("nsa", cache_nsa), ("fox", cache_fox), ("diff", cache_diff), ("moba", cache_moba))}
    past = dict({name: r[0] for name, r in rows.items()}, pieces={name: r[1] for name, r in rows.items()},
                logf=cache_fox_logf,
                win=state_nsa_win.reshape(depth, Bs, state_nsa_win.shape[2], -1),
                pt=page_table, len=past_len, page=page)

    n_c = _round_up(B + Bs, 8)
    c_all = jnp.pad(jnp.concatenate([c_prompt, c_sample], axis=0), ((0, n_c - B - Bs), (0, 0)))
    xp, xs = x_prompt.reshape(B * T, D), x_sample.reshape(Bs * Ts, D)
    names = ("nsa", "win", "fox", "logf", "diff", "moba")
    st_p = {n: [] for n in names}
    st_s = {n: [] for n in names}
    r3 = lambda a: a.reshape(a.shape[0], 1, a.shape[1])
    for l in range(depth):
        mod = _matmul(c_all, ada_w, l, bias=ada_b[l][None], silu_in=True, tm=n_c, tn=512, tk=D)
        prm = dict(norm_attn=r3(norm_attn), norm_ffn=r3(norm_ffn), w_in_r=_reorder_w_in(w_in[l], dims), w_out=w_out,
                   mix_gain=r3(mix_gain), cmp_pos=nsa_cmp_pos, cmp_w1=nsa_cmp_w1, cmp_w2=nsa_cmp_w2,
                   fox_fbias=fox_fbias[l][None], diff_lambda=diff_lambda, w_gate=w_gate, w_up=w_up, w_down=w_down)
        xp, new_p = _layer(xp, B, T, mod[:B], l, None, prm, tabs_p, dims, p_tiles)
        xs, new_s = _layer(xs, Bs, Ts, mod[B:B + Bs], l, past, prm, tabs_s, dims, s_tiles)
        for n in names:
            st_p[n].append(new_p[n])
            st_s[n].append(new_s[n])
    y_p = _norm(xp, final_norm[None], None, tm=p_tiles["norm_tm"]).reshape(B, T, D)
    y_s = _norm(xs, final_norm[None], None, tm=s_tiles["norm_tm"]).reshape(Bs, Ts, D)

    def stack(st, name, tail):
        a = jnp.stack(st[name])
        return a.reshape(a.shape[:3] + tail)

    kv = lambda n, hk, w: (n, hk, w)
    shapes = {"nsa": kv(4, hkv_a, DH), "win": kv(2, hkv_a, DH), "fox": kv(2, hkv_b, DH), "logf": (h_b,),
              "diff": kv(2, hkv_c, 2 * DH), "moba": kv(2, hkv_d, DH)}
    out = [y_p, y_s]
    for n in names:
        out += [stack(st_p, n, shapes[n]), stack(st_s, n, shapes[n])]
    return tuple(out)
```

```python
import functools
import math

import jax
import jax.numpy as jnp
import numpy as np
from jax import lax
from jax.experimental import pallas as pl
from jax.experimental.pallas import tpu as pltpu

DH = 128
CMP_LEN = 32
CMP_STRIDE = 16
SEL_BLOCK = 64
SEL_TOP = 16
WINDOW = 512
MOBA_BLOCK = 256
MOBA_TOP = 3
N_BUCKETS = 32
MAX_DIST = 128
NEG = -1e30
FORCE = 1e4
RMS_EPS = 1e-6
BELOW_NEG = -3e38

LANE = 128
V7X_VMEM_LIMIT_BYTES = 60000 * 1024

U_QA, U_QB, U_QD, U_QC = 0, 8, 16, 24
U_NSA, U_WIN, U_FOX, U_DIFF, U_MOBA, U_AG, U_BF, U_TOTAL = 32, 40, 44, 48, 56, 60, 61, 62

BF16 = jnp.bfloat16
F32 = jnp.float32


def _cparams(*sem):
    return pltpu.CompilerParams(dimension_semantics=sem, vmem_limit_bytes=V7X_VMEM_LIMIT_BYTES)


def _silu(x):
    return x * jax.nn.sigmoid(x)


def _div_pow2(x, d):
    assert d & (d - 1) == 0
    return lax.shift_right_logical(x, jnp.int32(d.bit_length() - 1))


def _lane_src(arr, unit0, stride=1, jstride=0):
    def make_spec(rows, width, to_brhj):
        def index_map(*g):
            b, r, h, j = to_brhj(*g)
            return (b, r, (unit0 * DH) // width + h * stride + j * jstride)
        return pl.BlockSpec((None, rows, width), index_map)
    return arr, make_spec


def _mm_body(*refs, nk, tk, k_rem, silu_in, dual, has_bias, has_res):
    it = iter(refs)
    x_ref, w_ref = next(it), next(it)
    w2_ref = next(it) if dual else None
    b_ref = next(it) if has_bias else None
    r_ref, g_ref = (next(it), next(it)) if has_res else (None, None)
    o_ref, acc = next(it), next(it)
    acc2 = next(it) if dual else None
    k = pl.program_id(2)

    @pl.when(k == 0)
    def _():
        acc[...] = jnp.zeros_like(acc)
        if dual:
            acc2[...] = jnp.zeros_like(acc2)

    def accumulate(overhang):
        xv = x_ref[...]
        if silu_in:
            xv = _silu(xv.astype(F32))
        xv = xv.astype(BF16)
        wv = w_ref[...].astype(BF16)
        w2v = w2_ref[...].astype(BF16) if dual else None
        if overhang:
            xv = jnp.where(lax.broadcasted_iota(jnp.int32, xv.shape, 1) < k_rem, xv, jnp.zeros_like(xv))
            rows = lax.broadcasted_iota(jnp.int32, wv.shape, 0) < k_rem
            wv = jnp.where(rows, wv, jnp.zeros_like(wv))
            if dual:
                w2v = jnp.where(rows, w2v, jnp.zeros_like(w2v))
        acc[...] += jnp.dot(xv, wv, preferred_element_type=F32)
        if dual:
            acc2[...] += jnp.dot(xv, w2v, preferred_element_type=F32)

    if k_rem:
        pl.when(k < nk - 1)(lambda: accumulate(False))
        pl.when(k == nk - 1)(lambda: accumulate(True))
    else:
        accumulate(False)

    @pl.when(k == nk - 1)
    def _():
        r = acc[...]
        if dual:
            r = _silu(r) * acc2[...]
        if has_bias:
            r = r + b_ref[...]
        if has_res:
            r = r_ref[...] + g_ref[...] * r
        o_ref[...] = r.astype(o_ref.dtype)


def _matmul(x, w, layer, *, tm, tn, tk, w2=None, bias=None, res=None, gate=None, rows_per_batch=None,
            silu_in=False, out_dtype=F32):
    M, K = x.shape
    N = w.shape[-1]
    assert M % tm == 0
    nk = pl.cdiv(K, tk)
    k_rem = K % tk
    grid = (M // tm, pl.cdiv(N, tn), nk)
    if layer is None:
        w_spec = pl.BlockSpec((tk, tn), lambda m, n, k: (k, n))
    else:
        w_spec = pl.BlockSpec((None, tk, tn), lambda m, n, k: (layer, k, n))
    in_specs = [pl.BlockSpec((tm, tk), lambda m, n, k: (m, k)), w_spec]
    args = [x, w]
    if w2 is not None:
        in_specs.append(w_spec)
        args.append(w2)
    if bias is not None:
        in_specs.append(pl.BlockSpec((1, tn), lambda m, n, k: (0, n)))
        args.append(bias)
    if res is not None:
        in_specs.append(pl.BlockSpec((tm, tn), lambda m, n, k: (m, n)))
        args.append(res)
        if gate.ndim == 2:
            in_specs.append(pl.BlockSpec((tm, tn), lambda m, n, k: (m, n)))
        else:
            assert rows_per_batch % tm == 0
            per = rows_per_batch // tm
            in_specs.append(pl.BlockSpec((None, 1, tn), lambda m, n, k: (m // per, 0, n)))
        args.append(gate)
    scratch = [pltpu.VMEM((tm, tn), F32)] * (2 if w2 is not None else 1)
    body = functools.partial(_mm_body, nk=nk, tk=tk, k_rem=k_rem, silu_in=silu_in, dual=w2 is not None,
                             has_bias=bias is not None, has_res=res is not None)
    return pl.pallas_call(
        body, grid=grid, in_specs=in_specs,
        out_specs=pl.BlockSpec((tm, tn), lambda m, n, k: (m, n)),
        out_shape=jax.ShapeDtypeStruct((M, N), out_dtype),
        scratch_shapes=scratch,
        compiler_params=_cparams("parallel", "parallel", "arbitrary"),
    )(*args)


def _norm_body(*refs, modulated):
    if modulated:
        x_ref, g_ref, sc_ref, sh_ref, o_ref = refs
    else:
        x_ref, g_ref, o_ref = refs
    x = x_ref[...]
    y = x * lax.rsqrt(jnp.mean(x * x, axis=-1, keepdims=True) + RMS_EPS) * g_ref[...]
    if modulated:
        y = y * (1.0 + sc_ref[...]) + sh_ref[...]
    o_ref[...] = y.astype(o_ref.dtype)


def _norm(x, g, layer, *, tm, sc=None, sh=None, rows_per_batch=None, out_dtype=F32):
    M, D = x.shape
    modulated = sc is not None
    if layer is None:
        g_spec = pl.BlockSpec((1, D), lambda m: (0, 0))
    else:
        g_spec = pl.BlockSpec((None, 1, D), lambda m: (layer, 0, 0))
    in_specs = [pl.BlockSpec((tm, D), lambda m: (m, 0)), g_spec]
    args = [x, g]
    if modulated:
        assert rows_per_batch % tm == 0
        per = rows_per_batch // tm
        mod_spec = pl.BlockSpec((None, 1, D), lambda m: (m // per, 0, 0))
        in_specs += [mod_spec, mod_spec]
        args += [sc, sh]
    return pl.pallas_call(
        functools.partial(_norm_body, modulated=modulated), grid=(M // tm,), in_specs=in_specs,
        out_specs=pl.BlockSpec((tm, D), lambda m: (m, 0)),
        out_shape=jax.ShapeDtypeStruct((M, D), out_dtype),
        compiler_params=_cparams("parallel"),
    )(*args)


GATHER_PAGES_PER_STEP = 8


def _gather_body(pt_ref, *refs, n_groups, per, page, slots, n_pieces):
    cache_refs, new_ref, o_ref = refs[:per], refs[per], refs[per + 1]
    j = pl.program_id(1)

    @pl.when(j < n_groups)
    def _():
        for p in range(per):
            if slots is None:
                o_ref[p * page:(p + 1) * page] = cache_refs[p][...]
            else:
                for s, src_slot in enumerate(slots):
                    o_ref[p * page:(p + 1) * page, s * LANE:(s + 1) * LANE] = (
                        cache_refs[p][pl.ds(src_slot, page, stride=n_pieces), :])

    @pl.when(j == n_groups)
    def _():
        o_ref[0:page] = new_ref[...]
        if per > 1:
            o_ref[page:per * page] = jnp.zeros(((per - 1) * page,) + o_ref.shape[1:], o_ref.dtype)


def _gather_pages(cache, layer, page_table, new_page, slots=None, n_pieces=1):
    page, W = new_page.shape[1:]
    B, n_pages = page_table.shape
    per = math.gcd(GATHER_PAGES_PER_STEP, n_pages)
    n_groups = n_pages // per
    assert cache.shape[2:] == ((page, W) if slots is None else (page * n_pieces, LANE))
    assert slots is None or W == len(slots) * LANE

    def page_spec(p):
        return pl.BlockSpec((None, None) + cache.shape[2:],
                            lambda b, j, pt: (layer, pt[b, jnp.minimum(j, n_groups - 1) * per + p], 0, 0))

    grid_spec = pltpu.PrefetchScalarGridSpec(
        num_scalar_prefetch=1, grid=(B, n_groups + 1),
        in_specs=[page_spec(p) for p in range(per)] + [pl.BlockSpec((None, page, W), lambda b, j, pt: (b, 0, 0))],
        out_specs=pl.BlockSpec((None, per * page, W), lambda b, j, pt: (b, j, 0)),
    )
    return pl.pallas_call(
        functools.partial(_gather_body, n_groups=n_groups, per=per, page=page, slots=slots, n_pieces=n_pieces),
        grid_spec=grid_spec,
        out_shape=jax.ShapeDtypeStruct((B, (n_pages + 1) * page, W), cache.dtype),
        compiler_params=_cparams("parallel", "arbitrary"),
    )(page_table, *([cache] * per), new_page)


CUM_CHUNK = 256


def _cum_body(*refs, n_past, n_new, nh):
    if n_past:
        past_ref, raw_ref, fb_ref, lf_ref, cum_ref = refs
    else:
        raw_ref, fb_ref, lf_ref, cum_ref = refs
    z = raw_ref[...][:, :nh] + fb_ref[...]
    lf = -(jnp.maximum(-z, 0.0) + jnp.log1p(jnp.exp(-jnp.abs(z))))
    lf_ref[...] = lf

    def tri(n):
        return (lax.broadcasted_iota(jnp.int32, (n, n), 0) >= lax.broadcasted_iota(jnp.int32, (n, n), 1)).astype(F32)

    def scan_rows(src_ref, dst_off, n, carry):
        c = min(CUM_CHUNK, n)
        assert n % c == 0
        t = tri(c)
        for i in range(n // c):
            cum_ref[dst_off + i * c:dst_off + (i + 1) * c, :] = jnp.dot(
                t, src_ref[i * c:(i + 1) * c, :], preferred_element_type=F32, precision=lax.Precision.HIGHEST)
        for i in range(n // c):
            rows = slice(dst_off + i * c, dst_off + (i + 1) * c)
            total = cum_ref[dst_off + (i + 1) * c - 1:dst_off + (i + 1) * c, :]
            cum_ref[rows, :] = cum_ref[rows, :] + carry
            carry = carry + total
        return carry

    carry = jnp.zeros((1, nh), F32)
    if n_past:
        carry = scan_rows(past_ref, 0, n_past, carry)
    scan_rows(lf_ref, n_past, n_new, carry)


def _fox_cum(raw_f, fbias, past_lf, n_past=0):
    B, Tn, _ = raw_f.shape
    nh = fbias.shape[-1]
    in_specs, args = [], []
    if n_past:
        in_specs.append(pl.BlockSpec((None, n_past, nh), lambda b: (b, 0, 0)))
        args.append(past_lf)
    in_specs += [pl.BlockSpec((None, Tn, LANE), lambda b: (b, 0, 0)), pl.BlockSpec((1, nh), lambda b: (0, 0))]
    args += [raw_f, fbias]
    return pl.pallas_call(
        functools.partial(_cum_body, n_past=n_past, n_new=Tn, nh=nh), grid=(B,), in_specs=in_specs,
        out_specs=[pl.BlockSpec((None, Tn, nh), lambda b: (b, 0, 0)),
                   pl.BlockSpec((None, n_past + Tn, nh), lambda b: (b, 0, 0))],
        out_shape=[jax.ShapeDtypeStruct((B, Tn, nh), F32), jax.ShapeDtypeStruct((B, n_past + Tn, nh), F32)],
        compiler_params=_cparams("parallel"),
    )(*args)


def _gelu_tanh(x):
    return 0.5 * x * (1.0 + jnp.tanh(math.sqrt(2.0 / math.pi) * (x + 0.044715 * (x * x * x))))


def _compress_body(x_ref, pos_ref, w1_ref, w2_ref, o_ref, *, n):
    hidden = w1_ref.shape[-1]
    acc_lo = jnp.zeros((n, hidden), F32)
    acc_hi = jnp.zeros((n, hidden), F32)
    for rho in range(CMP_STRIDE):
        xr = x_ref[pl.ds(rho, n, stride=CMP_STRIDE), :]
        lo = (xr + pos_ref[rho:rho + 1, :]).astype(BF16)
        hi = (xr + pos_ref[rho + CMP_STRIDE:rho + CMP_STRIDE + 1, :]).astype(BF16)
        w_lo = w1_ref[rho * DH:(rho + 1) * DH, :].astype(BF16)
        w_hi = w1_ref[(rho + CMP_STRIDE) * DH:(rho + CMP_STRIDE + 1) * DH, :].astype(BF16)
        acc_lo += jnp.dot(lo, w_lo, preferred_element_type=F32)
        acc_hi += jnp.dot(hi, w_hi, preferred_element_type=F32)
    hid = acc_lo + pltpu.roll(acc_hi, n - 1, 0)
    o_ref[...] = jnp.dot(_gelu_tanh(hid).astype(BF16), w2_ref[...].astype(BF16), preferred_element_type=F32)


def _compress(src, hkv, n, layer, pos, w1, w2):
    kv, make_spec = src
    B = kv.shape[0]
    hidden = w1.shape[-1]
    return pl.pallas_call(
        functools.partial(_compress_body, n=n), grid=(B, 2, hkv),
        in_specs=[
            make_spec(n * CMP_STRIDE, DH, lambda b, j, h: (b, 0, h, j)),
            pl.BlockSpec((None, None, CMP_LEN, DH), lambda b, j, h: (layer, j, 0, 0)),
            pl.BlockSpec((None, None, CMP_LEN * DH, hidden), lambda b, j, h: (layer, j, 0, 0)),
            pl.BlockSpec((None, None, hidden, DH), lambda b, j, h: (layer, j, 0, 0)),
        ],
        out_specs=pl.BlockSpec((None, None, None, n, DH), lambda b, j, h: (b, j, h, 0, 0)),
        out_shape=jax.ShapeDtypeStruct((B, 2, hkv, n, DH), F32),
        compiler_params=_cparams("parallel", "arbitrary", "arbitrary"),
    )(kv, pos, w1, w2)


def _take_top(score, lane_ids, count):
    sel = jnp.zeros(score.shape, F32)
    taken = []
    big = jnp.int32(score.shape[-1])
    for _ in range(count):
        m = jnp.max(score, axis=-1, keepdims=True)
        idx = jnp.min(jnp.where(score == m, lane_ids, big), axis=-1, keepdims=True)
        hit = lane_ids == idx
        sel = jnp.where(hit, 1.0, sel)
        score = jnp.where(hit, BELOW_NEG, score)
        taken.append((hit, m))
    return sel, taken


def _cmp_body(q_ref, kc_ref, vc_ref, bias_ref, ov_ref, o_ref, sel_ref, *, G, tq, q_off, n_sel, top):
    qi = pl.program_id(2)
    kc = kc_ref[...]
    vc = vc_ref[...].astype(BF16)
    scale = DH ** -0.5
    psum = None
    outs = []
    for g in range(G):
        qg = q_ref[:, g * DH:(g + 1) * DH]
        s = lax.dot_general(qg, kc, (((1,), (1,)), ((), ())), preferred_element_type=F32,
                            precision=lax.Precision.HIGHEST) * scale + bias_ref[g]
        m = jnp.max(s, axis=-1, keepdims=True)
        e = jnp.where(s > NEG / 2, jnp.exp(s - m), 0.0)
        p = e / jnp.maximum(jnp.sum(e, axis=-1, keepdims=True), 1e-30)
        outs.append(jnp.dot(p.astype(BF16), vc, preferred_element_type=F32))
        psum = p if psum is None else psum + p
    o_ref[...] = jnp.concatenate(outs, axis=1)
    imp = jnp.dot(psum, ov_ref[...], preferred_element_type=F32, precision=lax.Precision.HIGHEST)
    shape = imp.shape
    jb = lax.broadcasted_iota(jnp.int32, shape, 1)
    qpos = q_off + qi * tq + lax.broadcasted_iota(jnp.int32, shape, 0)
    cur = _div_pow2(qpos, SEL_BLOCK)
    forced = (jb == 0) | (jb == cur) | (jb == cur - 1)
    score = jnp.where(jb <= cur, imp + jnp.where(forced, FORCE, 0.0), NEG)
    score = jnp.where(jb < n_sel, score, BELOW_NEG)
    sel, _ = _take_top(score, jb, top)
    sel_ref[...] = jnp.where((jb <= cur) & (sel > 0.5), 0.0, NEG)


def _nsa_cmp(q, u_q, cmp_kv, bias, overlap, *, hkv, G, tq, q_off, n_sel):
    B, Tq, _ = q.shape
    n = cmp_kv.shape[3]
    nsp = overlap.shape[1]
    top = min(SEL_TOP, n_sel)
    body = functools.partial(_cmp_body, G=G, tq=tq, q_off=q_off, n_sel=n_sel, top=top)
    return pl.pallas_call(
        body, grid=(B, hkv, Tq // tq),
        in_specs=[
            pl.BlockSpec((None, tq, G * DH), lambda b, h, i: (b, i, u_q // G + h)),
            pl.BlockSpec((None, None, None, n, DH), lambda b, h, i: (b, 0, h, 0, 0)),
            pl.BlockSpec((None, None, None, n, DH), lambda b, h, i: (b, 1, h, 0, 0)),
            pl.BlockSpec((G, tq, n), lambda b, h, i: (h, i, 0)),
            pl.BlockSpec((n, nsp), lambda b, h, i: (0, 0)),
        ],
        out_specs=[pl.BlockSpec((None, tq, G * DH), lambda b, h, i: (b, i, h)),
                   pl.BlockSpec((None, None, tq, nsp), lambda b, h, i: (b, h, i, 0))],
        out_shape=[jax.ShapeDtypeStruct((B, Tq, hkv * G * DH), F32),
                   jax.ShapeDtypeStruct((B, hkv, Tq, nsp), F32)],
        compiler_params=_cparams("parallel", "parallel", "arbitrary"),
    )(q, cmp_kv, cmp_kv, bias, overlap)


def _block_means_body(pt_ref, *refs, per, page, n_pieces, k_pieces):
    page_refs, o_ref = refs[:per], refs[per]
    for h, piece in enumerate(k_pieces):
        k = jnp.concatenate([ref[pl.ds(piece, page, stride=n_pieces), :] for ref in page_refs], axis=0)
        o_ref[h] = jnp.sum(k.reshape(per * page // MOBA_BLOCK, MOBA_BLOCK, DH), axis=1) * (1.0 / MOBA_BLOCK)


def _block_means_paged(cache, n_pieces, layer, page_table, k_pieces):
    B, n_pages = page_table.shape
    page = cache.shape[2] // n_pieces
    per = math.gcd(MEANS_PAGES_PER_STEP, n_pages)
    blocks = per * page // MOBA_BLOCK
    assert (per * page) % MOBA_BLOCK == 0 and blocks % 8 == 0

    def page_spec(p):
        return pl.BlockSpec((None, None, page * n_pieces, LANE), lambda b, j, pt: (layer, pt[b, j * per + p], 0, 0))

    grid_spec = pltpu.PrefetchScalarGridSpec(
        num_scalar_prefetch=1, grid=(B, n_pages // per), in_specs=[page_spec(p) for p in range(per)],
        out_specs=pl.BlockSpec((None, len(k_pieces), blocks, DH), lambda b, j, pt: (b, 0, j, 0)))
    body = functools.partial(_block_means_body, per=per, page=page, n_pieces=n_pieces, k_pieces=k_pieces)
    return pl.pallas_call(
        body, grid_spec=grid_spec,
        out_shape=jax.ShapeDtypeStruct((B, len(k_pieces), n_pages * page // MOBA_BLOCK, DH), F32),
        compiler_params=_cparams("parallel", "arbitrary"),
    )(page_table, *([cache] * per))


MEANS_PAGES_PER_STEP = 16


def _moba_gate_body(q_ref, k_ref, m_ref, *, G, n_full, nbp, q_off, top, is_means):
    Tq = q_ref.shape[0]
    shape = (Tq, nbp)
    jb = lax.broadcasted_iota(jnp.int32, shape, 1)
    own = _div_pow2(q_off + lax.broadcasted_iota(jnp.int32, shape, 0), MOBA_BLOCK)
    if n_full > 0:
        if is_means:
            kmean = k_ref[...]
        else:
            kmean = jnp.sum(k_ref[...].reshape(n_full, MOBA_BLOCK, DH), axis=1) * (1.0 / MOBA_BLOCK)
        if nbp > n_full:
            kmean = jnp.concatenate([kmean, jnp.zeros((nbp - n_full, DH), F32)], axis=0)
    for g in range(G):
        mask = jnp.where(jb == own, 0.0, NEG)
        if n_full > 0:
            gate = lax.dot_general(q_ref[:, g * DH:(g + 1) * DH], kmean, (((1,), (1,)), ((), ())),
                                   preferred_element_type=F32, precision=lax.Precision.HIGHEST)
            score = jnp.where((jb < own) & (jb < n_full), gate, NEG)
            score = jnp.where(jb < n_full, score, BELOW_NEG)
            _, taken = _take_top(score, jb, top)
            for hit, val in taken:
                mask = jnp.where(hit & (val > NEG / 2), 0.0, mask)
        m_ref[g] = mask


def _moba_gate(q, u_q, ksrc, *, hkv, G, Tk, q_off, nbp, means=None):
    B, Tq, _ = q.shape
    n_full = Tk // MOBA_BLOCK
    top = min(MOBA_TOP, n_full)
    if means is None:
        k, k_spec = ksrc
        k_in = k_spec(max(n_full, 1) * MOBA_BLOCK, DH, lambda b, h: (b, 0, h, 0))
    else:
        assert means.shape[2] == n_full
        k, k_in = means, pl.BlockSpec((None, None, n_full, DH), lambda b, h: (b, h, 0, 0))
    body = functools.partial(_moba_gate_body, G=G, n_full=n_full, nbp=nbp, q_off=q_off, top=top,
                             is_means=means is not None)
    return pl.pallas_call(
        body, grid=(B, hkv),
        in_specs=[pl.BlockSpec((None, Tq, G * DH), lambda b, h: (b, 0, u_q // G + h)), k_in],
        out_specs=pl.BlockSpec((None, None, G, Tq, nbp), lambda b, h: (b, h, 0, 0, 0)),
        out_shape=jax.ShapeDtypeStruct((B, hkv, G, Tq, nbp), F32),
        compiler_params=_cparams("parallel", "arbitrary"),
    )(q, k)


def _flash_body(*refs, G, Gm, tq, tk, dv, mode, nsteps, fox, mask_blk):
    it = iter(refs)
    q_ref, k_ref, v_ref, bias_ref = next(it), next(it), next(it), next(it)
    cq_ref, ck_ref = (next(it), next(it)) if fox else (None, None)
    mask_ref = next(it) if mask_blk else None
    o_ref, m_sc, acc_sc = next(it), next(it), next(it)
    qi, step = pl.program_id(2), pl.program_id(3)
    if mode == "band":
        ki = qi - (nsteps - 1) + step
        active = ki >= 0
    elif mode == "causal":
        ki = step
        active = ki <= qi
    else:
        ki = step
        active = step >= 0

    @pl.when(step == 0)
    def _():
        m_sc[...] = jnp.full(m_sc.shape, NEG, F32)
        acc_sc[...] = jnp.zeros_like(acc_sc)

    @pl.when(active)
    def _():
        k = k_ref[...].astype(BF16)
        v = jnp.concatenate([v_ref[...].astype(BF16), jnp.ones((tk, LANE), BF16)], axis=1)
        q = jnp.concatenate([q_ref[:, g * DH:(g + 1) * DH] for g in range(G)], axis=0)
        q = (q * (DH ** -0.5)).astype(BF16)
        s = lax.dot_general(q, k, (((1,), (1,)), ((), ())), preferred_element_type=F32)
        s = s.reshape(G, tq, tk) + bias_ref[...]
        if fox:
            s = s + (cq_ref[...] - ck_ref[...])
        if mask_blk:
            nbp = mask_ref.shape[-1]
            kpos = ki * tk + lax.broadcasted_iota(jnp.int32, (nbp, tk), 1)
            lo = lax.broadcasted_iota(jnp.int32, (nbp, tk), 0) * mask_blk
            expand = jnp.where((kpos >= lo) & (kpos < lo + mask_blk), 1.0, 0.0).astype(BF16)
            picked = jnp.dot(mask_ref[...].reshape(Gm * tq, nbp).astype(BF16), expand, preferred_element_type=F32)
            s = s + picked.reshape(Gm, tq, tk)
        m_old = m_sc[...]
        m_new = jnp.maximum(m_old, jnp.max(s, axis=-1, keepdims=True))
        m_use = jnp.where(m_new < NEG / 2, 0.0, m_new)
        p = jnp.exp((s - m_use).astype(BF16))
        alpha = jnp.exp(m_old - m_use)
        pv = jnp.dot(p.reshape(G * tq, tk), v, preferred_element_type=F32)
        acc_sc[...] = alpha * acc_sc[...] + pv.reshape(G, tq, dv + LANE)
        m_sc[...] = m_new

    @pl.when(step == nsteps - 1)
    def _():
        acc = acc_sc[...]
        o = acc[:, :, :dv] / jnp.maximum(acc[:, :, dv:dv + 1], 1e-30)
        for g in range(G):
            o_ref[:, g * dv:(g + 1) * dv] = o[g]


def _flash(q, u_q, ksrc, vsrc, bias, *, hkv, G, tq, tk, dv, mode, nsteps, fox=None, mask=None, mask_blk=0,
           q_stride=1):
    B, Tq, _ = q.shape
    nq = Tq // tq
    bias, bias_h0, bias_shared = bias
    NB = bias.shape[1]
    Gb = 1 if bias_shared else G
    assert bias_h0 % Gb == 0

    if mode == "causal":
        kidx = lambda i, s: jnp.minimum(s, i)
        bidx = lambda i, s: jnp.clip(i - s, 0, NB - 1)
    elif mode == "band":
        kidx = lambda i, s: jnp.maximum(i - (nsteps - 1) + s, 0)
        bidx = lambda i, s: jnp.clip(nsteps - 1 - s, 0, NB - 1)
    else:
        kidx = lambda i, s: s
        bidx = lambda i, s: s

    kv_index = lambda b, h, i, s: (b, kidx(i, s), h, 0)
    in_specs = [
        pl.BlockSpec((None, tq, G * DH), lambda b, h, i, s: (b, i, u_q // G + h * q_stride)),
        ksrc[1](tk, DH, kv_index),
        vsrc[1](tk, dv, kv_index),
        pl.BlockSpec((Gb, None, tq, tk),
                     lambda b, h, i, s: (bias_h0 // Gb + (0 if bias_shared else h), bidx(i, s), 0, 0)),
    ]
    args = [q, ksrc[0], vsrc[0], bias]
    if fox is not None:
        in_specs += [pl.BlockSpec((None, G, tq, 1), lambda b, h, i, s: (b, h, i, 0)),
                     pl.BlockSpec((None, G, 1, tk), lambda b, h, i, s: (b, h, 0, kidx(i, s)))]
        args += list(fox)
    Gm = 0
    if mask is not None:
        Gm, nbp = mask.shape[2], mask.shape[4]
        in_specs.append(pl.BlockSpec((None, None, Gm, tq, nbp), lambda b, h, i, s: (b, h, 0, i, 0)))
        args.append(mask)
    body = functools.partial(_flash_body, G=G, Gm=Gm, tq=tq, tk=tk, dv=dv, mode=mode, nsteps=nsteps,
                             fox=fox is not None, mask_blk=mask_blk if mask is not None else 0)
    return pl.pallas_call(
        body, grid=(B, hkv, nq, nsteps), in_specs=in_specs,
        out_specs=pl.BlockSpec((None, tq, G * dv), lambda b, h, i, s: (b, i, h)),
        out_shape=jax.ShapeDtypeStruct((B, Tq, hkv * G * dv), F32),
        scratch_shapes=[pltpu.VMEM((G, tq, 1), F32), pltpu.VMEM((G, tq, dv + LANE), F32)],
        compiler_params=_cparams("parallel", "parallel", "parallel", "arbitrary"),
    )(*args)


PAGED_PER_STEP = 32


def _flash_paged_body(pt_ref, *refs, groups, n_out, per, page, n_pieces, n_groups, dv, fox, has_mask, mask_blk):
    it = iter(refs)
    q_ref = next(it)
    page_refs = [next(it) for _ in range(per)]
    new_ref, bias_ref = next(it), next(it)
    cq_ref, ck_ref = (next(it), next(it)) if fox else (None, None)
    mask_ref = next(it) if has_mask else None
    o_refs = [next(it) for _ in range(n_out)]
    m_sc, acc_sc = next(it), next(it)
    j = pl.program_id(1)
    last = j == n_groups
    tk = per * page
    tq = q_ref.shape[0]

    @pl.when(j == 0)
    def _():
        m_sc[...] = jnp.full(m_sc.shape, NEG, F32)
        acc_sc[...] = jnp.zeros_like(acc_sc)

    def attend(sources):
        nk = len(sources) * page
        loaded = {}

        def piece(slot):
            if slot not in loaded:
                parts = [ref[pl.ds(slot, page, stride=n_pieces), :] for ref in sources]
                loaded[slot] = jnp.concatenate(parts, axis=0).astype(BF16)
            return loaded[slot]

        if has_mask:
            nbp = mask_ref.shape[-1]
            kpos = j * tk + lax.broadcasted_iota(jnp.int32, (nbp, nk), 1)
            lo = lax.broadcasted_iota(jnp.int32, (nbp, nk), 0) * mask_blk
            expand = jnp.where((kpos >= lo) & (kpos < lo + mask_blk), 1.0, 0.0).astype(BF16)
        ones = jnp.ones((nk, LANE), BF16)
        row = 0
        for grp in groups:
            G = len(grp["q_units"])
            k = piece(grp["k_piece"])
            v = jnp.concatenate([piece(s) for s in grp["v_pieces"]] + [ones], axis=1)
            q = jnp.concatenate([q_ref[:, u * DH:(u + 1) * DH] for u in grp["q_units"]], axis=0)
            q = (q * (DH ** -0.5)).astype(BF16)
            s = lax.dot_general(q, k, (((1,), (1,)), ((), ())), preferred_element_type=F32).reshape(G, tq, nk)
            b0 = grp["bias_head"]
            s = s + (bias_ref[:, :, :nk] if b0 is None else bias_ref[b0:b0 + G, :, :nk])
            if fox:
                f0 = grp["fox_head"]
                s = s + (cq_ref[f0:f0 + G] - ck_ref[f0:f0 + G, :, :nk])
            if has_mask:
                msk = mask_ref[grp["mask_head"]]
                gm = msk.shape[0]
                picked = jnp.dot(msk.reshape(gm * tq, nbp).astype(BF16), expand, preferred_element_type=F32)
                s = s + picked.reshape(gm, tq, nk)
            rows = slice(row, row + G)
            m_old = m_sc[rows]
            m_new = jnp.maximum(m_old, jnp.max(s, axis=-1, keepdims=True))
            m_use = jnp.where(m_new < NEG / 2, 0.0, m_new)
            p = jnp.exp((s - m_use).astype(BF16))
            alpha = jnp.exp(m_old - m_use)
            pv = jnp.dot(p.reshape(G * tq, nk), v, preferred_element_type=F32)
            acc_sc[rows] = alpha * acc_sc[rows] + pv.reshape(G, tq, dv + LANE)
            m_sc[rows] = m_new
            row += G

    @pl.when(j < n_groups)
    def _():
        attend(page_refs)

    @pl.when(last)
    def _():
        attend([new_ref])
        row = 0
        for grp in groups:
            G = len(grp["q_units"])
            acc = acc_sc[row:row + G]
            o = acc[:, :, :dv] / jnp.maximum(acc[:, :, dv:dv + 1], 1e-30)
            out_i, unit0 = grp["out"]
            for g in range(G):
                o_refs[out_i][:, (unit0 + g) * dv:(unit0 + g + 1) * dv] = o[g]
            row += G


def _flash_paged(q, q_block, groups, out_heads, cache, n_pieces, layer, page_table, new_page, bias, *, dv,
                 fox=None, mask=None, mask_blk=0):
    B, Tq, _ = q.shape
    n_pages = page_table.shape[1]
    page = cache.shape[2] // n_pieces
    per = math.gcd(PAGED_PER_STEP, n_pages)
    n_groups = n_pages // per
    tk = per * page
    assert bias.shape[1:] == (n_groups + 1, Tq, tk)
    n_rows = sum(len(g["q_units"]) for g in groups)
    q_w = 8 * DH

    def page_spec(p):
        return pl.BlockSpec((None, None, page * n_pieces, LANE),
                            lambda b, j, pt: (layer, pt[b, jnp.minimum(j, n_groups - 1) * per + p], 0, 0))

    in_specs = [pl.BlockSpec((None, Tq, q_w), lambda b, j, pt: (b, 0, q_block))]
    in_specs += [page_spec(p) for p in range(per)]
    in_specs += [pl.BlockSpec((None, page * n_pieces, LANE), lambda b, j, pt: (b, 0, 0)),
                 pl.BlockSpec((bias.shape[0], None, Tq, tk), lambda b, j, pt: (0, j, 0, 0))]
    args = [q] + [cache] * per + [new_page, bias]
    if fox is not None:
        H = fox[0].shape[1]
        in_specs += [pl.BlockSpec((None, H, Tq, 1), lambda b, j, pt: (b, 0, 0, 0)),
                     pl.BlockSpec((None, H, 1, tk), lambda b, j, pt: (b, 0, 0, j))]
        args += list(fox)
    if mask is not None:
        in_specs.append(pl.BlockSpec((None,) + mask.shape[1:], lambda b, j, pt: (b, 0, 0, 0, 0)))
        args.append(mask)
    body = functools.partial(_flash_paged_body, groups=groups, n_out=len(out_heads), per=per, page=page,
                             n_pieces=n_pieces, n_groups=n_groups, dv=dv, fox=fox is not None,
                             has_mask=mask is not None, mask_blk=mask_blk)
    grid_spec = pltpu.PrefetchScalarGridSpec(
        num_scalar_prefetch=1, grid=(B, n_groups + 1), in_specs=in_specs,
        out_specs=[pl.BlockSpec((None, Tq, nh * dv), lambda b, j, pt: (b, 0, 0)) for nh in out_heads],
        scratch_shapes=[pltpu.VMEM((n_rows, Tq, 1), F32), pltpu.VMEM((n_rows, Tq, dv + LANE), F32)],
    )
    return pl.pallas_call(
        body, grid_spec=grid_spec,
        out_shape=[jax.ShapeDtypeStruct((B, Tq, nh * dv), F32) for nh in out_heads],
        compiler_params=_cparams("parallel", "arbitrary"),
    )(page_table, *args)


def _head_rms(x, gain):
    return x * lax.rsqrt(jnp.mean(x * x, axis=-1, keepdims=True) + RMS_EPS) * gain


def _combine_body(cmp_ref, sel_ref, win_ref, ag_ref, b_ref, c1_ref, c2_ref, d_ref, mg_ref, lp_ref, o_ref, *,
                  h_a, h_b, h_c, h_d, lam_init):
    gates = jax.nn.sigmoid(ag_ref[...])
    col = 0
    for h in range(h_a):
        sl = slice(h * DH, (h + 1) * DH)
        o = (gates[:, 3 * h:3 * h + 1] * cmp_ref[:, sl] + gates[:, 3 * h + 1:3 * h + 2] * sel_ref[:, sl]
             + gates[:, 3 * h + 2:3 * h + 3] * win_ref[:, sl])
        o_ref[:, col:col + DH] = _head_rms(o, mg_ref[:, col:col + DH]).astype(o_ref.dtype)
        col += DH
    for h in range(h_b):
        o_ref[:, col:col + DH] = _head_rms(b_ref[:, h * DH:(h + 1) * DH], mg_ref[:, col:col + DH]).astype(o_ref.dtype)
        col += DH
    for h in range(h_c):
        lp = lp_ref[h]
        lam = (jnp.exp(jnp.sum(lp[0:1] * lp[1:2], axis=-1, keepdims=True))
               - jnp.exp(jnp.sum(lp[2:3] * lp[3:4], axis=-1, keepdims=True)) + lam_init)
        sl = slice(h * 2 * DH, (h + 1) * 2 * DH)
        o = c1_ref[:, sl] - lam * c2_ref[:, sl]
        o_ref[:, col:col + 2 * DH] = (_head_rms(o, mg_ref[:, col:col + 2 * DH]) * (1.0 - lam_init)).astype(o_ref.dtype)
        col += 2 * DH
    for h in range(h_d):
        o_ref[:, col:col + DH] = _head_rms(d_ref[:, h * DH:(h + 1) * DH], mg_ref[:, col:col + DH]).astype(o_ref.dtype)
        col += DH


def _combine(o_cmp, o_sel, o_win, proj, o_b, o_c1, o_c2, o_d, mix_gain, diff_lambda, layer, *, tm, lam_init):
    M = o_cmp.shape[0]
    D = mix_gain.shape[-1]
    h_a, h_b, h_d = o_cmp.shape[1] // DH, o_b.shape[1] // DH, o_d.shape[1] // DH
    h_c = o_c1.shape[1] // (2 * DH)
    row = lambda w: pl.BlockSpec((tm, w), lambda m: (m, 0))
    body = functools.partial(_combine_body, h_a=h_a, h_b=h_b, h_c=h_c, h_d=h_d, lam_init=lam_init)
    return pl.pallas_call(
        body, grid=(M // tm,),
        in_specs=[row(o_cmp.shape[1]), row(o_sel.shape[1]), row(o_win.shape[1]),
                  pl.BlockSpec((tm, LANE), lambda m: (m, U_AG)),
                  row(o_b.shape[1]), row(o_c1.shape[1]), row(o_c2.shape[1]), row(o_d.shape[1]),
                  pl.BlockSpec((None, 1, D), lambda m: (layer, 0, 0)),
                  pl.BlockSpec((None,) + diff_lambda.shape[1:], lambda m: (layer, 0, 0, 0))],
        out_specs=pl.BlockSpec((tm, D), lambda m: (m, 0)),
        out_shape=jax.ShapeDtypeStruct((M, D), BF16),
        compiler_params=_cparams("parallel"),
    )(o_cmp, o_sel, o_win, proj, o_b, o_c1, o_c2, o_d, mix_gain, diff_lambda)


def _t5_bucket_np(dist):
    n = np.maximum(dist, 0)
    exact = N_BUCKETS // 2
    nf = np.maximum(n, 1).astype(np.float32)
    big = exact + (np.log(nf / np.float32(exact)) / np.float32(math.log(MAX_DIST / exact))
                   * np.float32(N_BUCKETS - exact)).astype(np.int32)
    return np.where(n < exact, n, np.minimum(big, N_BUCKETS - 1)).astype(np.int32)


def _bias_from_dist(tab_t, dist, valid):
    if tab_t is None:
        return jnp.asarray(np.where(valid, 0.0, NEG).astype(np.float32))[None]
    idx = _t5_bucket_np(dist).reshape(-1)
    b = jnp.take(tab_t, jnp.asarray(idx), axis=1).reshape((tab_t.shape[0],) + dist.shape)
    return jnp.where(jnp.asarray(valid)[None], b, NEG)


def _dist_table(tab_t):
    if tab_t is None:
        return jnp.zeros((1, MAX_DIST + 1), F32)
    return jnp.take(tab_t, jnp.asarray(_t5_bucket_np(np.arange(MAX_DIST + 1))), axis=1)


def _bias_range(bd, lo, hi, window=None, descending=False):
    H, nd = bd.shape
    far = 10 ** 9
    w = far if window is None else window
    assert w >= nd - 1
    parts = []
    for seg_lo, seg_hi, kind in ((-far, 0, "neg"), (0, nd - 1, "tab"), (nd - 1, w, "far"), (w, far, "neg")):
        a, b = max(lo, seg_lo), min(hi, seg_hi)
        if b <= a:
            continue
        if kind == "neg":
            parts.append(jnp.full((H, b - a), NEG, F32))
        elif kind == "tab":
            parts.append(bd[:, ::-1][:, nd - b:nd - a] if descending else bd[:, a:b])
        else:
            parts.append(jnp.broadcast_to(bd[:, nd - 1:], (H, b - a)))
    return jnp.concatenate(parts[::-1] if descending else parts, axis=1)


def _toeplitz_body(vec_ref, o_ref, *, t, stride, transpose, mask_last):
    rows = o_ref.shape[1] if transpose else o_ref.shape[0]
    x = jnp.broadcast_to(vec_ref[...], (rows, vec_ref.shape[-1]))
    x = pltpu.roll(x, 0, 1, stride=stride, stride_axis=0)[:, :t]
    if mask_last:
        x = jnp.where(lax.broadcasted_iota(jnp.int32, x.shape, 0) < rows - 1, x, NEG)
    o_ref[...] = x.T if transpose else x


def _toeplitz(vec, rows, t, stride=1, transpose=False, mask_last=False):
    H, NV, _, L = vec.shape
    out = (t, rows) if transpose else (rows, t)
    body = functools.partial(_toeplitz_body, t=t, stride=stride, transpose=transpose, mask_last=mask_last)
    return pl.pallas_call(
        body, grid=(H, NV),
        in_specs=[pl.BlockSpec((None, None, 1, L), lambda h, v: (h, v, 0, 0))],
        out_specs=pl.BlockSpec((None, None) + out, lambda h, v: (h, v, 0, 0)),
        out_shape=jax.ShapeDtypeStruct((H, NV) + out, F32),
        compiler_params=_cparams("parallel", "parallel"),
    )(vec)


def _prompt_bias(bd, t, n_var, window=None):
    vec = jnp.stack([jnp.concatenate([_bias_range(bd, v * t - t, v * t + 1, window, descending=True),
                                      _bias_range(bd, v * t + 1, v * t + t, window, descending=True)], axis=1)
                     for v in range(n_var)], axis=1)
    return _toeplitz(vec[:, :, None, :], t, t)


def _decode_bias(bd, tq, q_rel, n_keys, tk, window=None):
    H = bd.shape[0]
    vec = _bias_range(bd, q_rel - n_keys + 1, q_rel + tq, window, descending=True)
    rows = jnp.stack([vec[:, tq - 1 - i:tq - 1 - i + n_keys] for i in range(tq)], axis=1)
    return rows.reshape(H, tq, n_keys // tk, tk).transpose(0, 2, 1, 3)


def _cmp_bias_prompt(bd, T, n):
    H = bd.shape[0]
    L = T + CMP_STRIDE * n
    vec = jnp.concatenate([_bias_range(bd, 1 - CMP_LEN, T + 1 - CMP_LEN), jnp.full((H, L - T), NEG, F32)], axis=1)
    return _toeplitz(vec[:, None, None, :], n, T, stride=CMP_STRIDE, transpose=True, mask_last=True)[:, 0]


def _cmp_bias(tab_t, q_pos, n):
    end = np.arange(n) * CMP_STRIDE + CMP_LEN - 1
    dist = q_pos[:, None] - end[None, :]
    valid = (dist >= 0) & (np.arange(n)[None, :] < n - 1)
    return _bias_from_dist(tab_t, dist, valid)


def _overlap_np(n, n_sel, n_sel_pad):
    cs = np.arange(n)[:, None] * CMP_STRIDE
    js = np.arange(n_sel_pad)[None, :] * SEL_BLOCK
    ov = (cs < js + SEL_BLOCK) & (cs + CMP_LEN > js) & (np.arange(n)[:, None] < n - 1) & (np.arange(n_sel_pad)[None, :] < n_sel)
    return ov.astype(np.float32)


def _round_up(x, m):
    return -(-x // m) * m


def _reorder_w_in(w, dims):
    h_a, hkv_a, h_b, hkv_b, h_c, hkv_c, h_d, hkv_d = dims
    widths = [h_a * DH] + [hkv_a * DH] * 6 + [h_a * 3, h_b * DH, hkv_b * DH, hkv_b * DH, h_b,
                                             h_c * 2 * DH, hkv_c * 2 * DH, hkv_c * 2 * DH, h_d * DH, hkv_d * DH, hkv_d * DH]
    offs = np.concatenate([[0], np.cumsum(widths)])
    seg = lambda i: w[:, offs[i]:offs[i + 1]]
    D = w.shape[0]
    g_c = h_c // hkv_c
    c_q = seg(12).reshape(D, hkv_c, g_c, 2, DH).transpose(0, 1, 3, 2, 4).reshape(D, -1)
    pad = lambda a: jnp.pad(a, ((0, 0), (0, LANE - a.shape[1])))
    return jnp.concatenate([seg(0), seg(8), seg(15), c_q, seg(1), seg(2), seg(3), seg(4), seg(5), seg(6),
                            seg(9), seg(10), seg(13), seg(14), seg(16), seg(17), pad(seg(7)), pad(seg(11))], axis=1)


def _layer(x, B, T, mod, layer, past, prm, tabs, dims, tiles):
    h_a, hkv_a, h_b, hkv_b, h_c, hkv_c, h_d, hkv_d = dims
    g_a, g_b, g_c, g_d = h_a // hkv_a, h_b // hkv_b, h_c // hkv_c, h_d // hkv_d
    D = x.shape[1]
    M = B * T
    sh1, sc1, gt1, sh2, sc2, gt2 = [m[:, None, :] for m in jnp.split(mod, 6, axis=-1)]
    tm_n = tiles["norm_tm"]
    h = _norm(x, prm["norm_attn"], layer, tm=tm_n, sc=sc1, sh=sh1, rows_per_batch=T, out_dtype=BF16)
    proj = _matmul(h, prm["w_in_r"], None, **tiles["w_in"]).reshape(B, T, U_TOTAL * LANE)
    new = {
        "nsa": proj[:, :, U_NSA * LANE:U_WIN * LANE], "fox": proj[:, :, U_FOX * LANE:U_DIFF * LANE],
        "diff": proj[:, :, U_DIFF * LANE:U_MOBA * LANE], "moba": proj[:, :, U_MOBA * LANE:U_AG * LANE],
    }
    kw_new = proj[:, :, U_WIN * LANE:U_FOX * LANE]
    tq, tk = tiles["tq"], tiles["tk"]

    if past is None:
        q_off, Tk = 0, T
        src = dict(
            cmp=_lane_src(proj, U_NSA, jstride=hkv_a),
            sel_k=_lane_src(proj, U_NSA + 2 * hkv_a), sel_v=_lane_src(proj, U_NSA + 3 * hkv_a),
            win_k=_lane_src(proj, U_WIN), win_v=_lane_src(proj, U_WIN + hkv_a),
            fox_k=_lane_src(proj, U_FOX), fox_v=_lane_src(proj, U_FOX + hkv_b),
            diff_k=[_lane_src(proj, U_DIFF + half, stride=2) for half in range(2)],
            diff_v=_lane_src(proj, U_DIFF + 2 * hkv_c),
            moba_k=_lane_src(proj, U_MOBA), moba_v=_lane_src(proj, U_MOBA + hkv_d))
        assert T >= tiles["wbuf"]
        new["win"] = kw_new[:, T - tiles["wbuf"]:]
        lf_new, cum = _fox_cum(proj[:, :, U_BF * LANE:], prm["fox_fbias"], None)
        mode, nsteps = "causal", T // tk
        band_steps = tabs["win"].shape[1]
        n_cmp_pad = T // CMP_STRIDE
        tk_pad = T
    else:
        q_off = past["len"]
        Tk = q_off + T
        page = past["page"]

        pieces = past["pieces"]

        def contiguous(name, n_lanes, slots):
            new_page = jnp.pad(new[name][:, :, :n_lanes], ((0, 0), (0, page - T), (0, 0)))
            return _gather_pages(past[name], layer, past["pt"], new_page, slots=slots,
                                 n_pieces=pieces[name]["n"])

        pc = pieces["nsa"]["of"]
        nsa_cmp = contiguous("nsa", 2 * hkv_a * DH, [pc(t, h) for t in range(2) for h in range(hkv_a)])
        pc = pieces["moba"]["of"]
        moba_means = _block_means_paged(past["moba"], pieces["moba"]["n"], layer, past["pt"],
                                        [pc(0, h) for h in range(hkv_d)])
        past_lf = _gather_pages(past["logf"], layer, past["pt"], jnp.zeros((B, page, h_b), F32))
        win_kv = jnp.concatenate([past["win"][layer], kw_new], axis=1)
        wbuf = past["win"].shape[2]
        new["win"] = win_kv[:, -wbuf:]
        src = dict(cmp=_lane_src(nsa_cmp, 0, jstride=hkv_a), moba_k=None,
                   win_k=_lane_src(win_kv, 0), win_v=_lane_src(win_kv, hkv_a))
        lf_new, cum = _fox_cum(proj[:, :, U_BF * LANE:], prm["fox_fbias"], past_lf, n_past=q_off)
        n_cmp_pad = q_off // CMP_STRIDE
        tk_pad = tabs["causal"].shape[1] * tabs["causal"].shape[3]

        def paged(name, q_block, groups, out_heads, bias, dv=DH, **kw):
            info = pieces[name]
            rows = new[name].reshape(B, T, info["n_t"], info["hkv"], info["halves"], LANE)
            rows = rows.transpose(0, 1, 2, 4, 3, 5).reshape(B, T * info["n"], LANE)
            new_page = jnp.pad(rows, ((0, 0), (0, (page - T) * info["n"]), (0, 0)))
            return _flash_paged(proj, q_block, groups, out_heads, past[name], info["n"], layer, past["pt"],
                                new_page, bias, dv=dv, **kw)
    new["logf"] = lf_new

    cmp_kv = _compress(src["cmp"], hkv_a, n_cmp_pad, layer, prm["cmp_pos"], prm["cmp_w1"], prm["cmp_w2"])
    n_sel = -(-Tk // SEL_BLOCK)
    o_cmp, sel_mask = _nsa_cmp(proj, U_QA, cmp_kv, tabs["cmp"], tabs["overlap"], hkv=hkv_a, G=g_a,
                               tq=min(tiles["cmp_tq"], T), q_off=q_off, n_sel=n_sel)
    sel_mask = sel_mask[:, :, None]
    cum_t = cum.transpose(0, 2, 1)
    cq = cum_t[:, :, Tk - T:, None]
    ck = jnp.pad(cum_t, ((0, 0), (0, 0), (0, tk_pad - Tk)))[:, :, None, :]
    nb = -(-Tk // MOBA_BLOCK)
    moba_mask = _moba_gate(proj, U_QD, src["moba_k"], hkv=hkv_d, G=g_d, Tk=Tk, q_off=q_off,
                           nbp=_round_up(nb, LANE), means=None if past is None else moba_means)
    if past is None:
        o_sel = _flash(proj, U_QA, src["sel_k"], src["sel_v"], tabs["a"], hkv=hkv_a, G=g_a,
                       tq=tq, tk=tk, dv=DH, mode=mode, nsteps=nsteps, mask=sel_mask, mask_blk=SEL_BLOCK)
        o_win = _flash(proj, U_QA, src["win_k"], src["win_v"], (tabs["win"], 0, False), hkv=hkv_a, G=g_a,
                       tq=tabs["win"].shape[2], tk=tabs["win"].shape[3], dv=DH, mode="band", nsteps=band_steps)
        o_b = _flash(proj, U_QB, src["fox_k"], src["fox_v"], tabs["mask"], hkv=hkv_b, G=g_b,
                     tq=tq, tk=tk, dv=DH, mode=mode, nsteps=nsteps, fox=(cq, ck))
        o_c = [_flash(proj, U_QC + half * g_c, src["diff_k"][half], src["diff_v"], tabs["c"],
                      hkv=hkv_c, G=g_c, tq=tq, tk=tk, dv=2 * DH, mode=mode, nsteps=nsteps, q_stride=2)
               for half in range(2)]
        o_d = _flash(proj, U_QD, src["moba_k"], src["moba_v"], tabs["d"], hkv=hkv_d, G=g_d,
                     tq=tq, tk=tk, dv=DH, mode=mode, nsteps=nsteps, mask=moba_mask, mask_blk=MOBA_BLOCK)
    else:
        o_win = _flash(proj, U_QA, src["win_k"], src["win_v"], (tabs["win"], 0, False), hkv=hkv_a, G=g_a,
                       tq=tq, tk=tabs["win"].shape[3], dv=DH, mode="full", nsteps=1)
        causal = tabs["causal"]
        pc = pieces["nsa"]["of"]
        o_sel, = paged("nsa", U_QA // 8, [
            dict(q_units=[h * g_a + g for g in range(g_a)], k_piece=pc(2, h), v_pieces=[pc(3, h)],
                 bias_head=h * g_a, mask_head=h, out=(0, h * g_a)) for h in range(hkv_a)],
            [h_a], causal[:h_a], mask=sel_mask, mask_blk=SEL_BLOCK)
        pc = pieces["fox"]["of"]
        o_b, = paged("fox", U_QB // 8, [
            dict(q_units=[h * g_b + g for g in range(g_b)], k_piece=pc(0, h), v_pieces=[pc(1, h)],
                 bias_head=None, fox_head=h * g_b, out=(0, h * g_b)) for h in range(hkv_b)],
            [h_b], causal[h_a + h_c + h_d:], fox=(cq, ck))
        pc = pieces["diff"]["of"]
        o_c = paged("diff", U_QC // 8, [
            dict(q_units=[(h * 2 + half) * g_c + g for g in range(g_c)], k_piece=pc(0, h, half),
                 v_pieces=[pc(1, h, 0), pc(1, h, 1)], bias_head=h * g_c, out=(half, h * g_c))
            for half in range(2) for h in range(hkv_c)],
            [h_c, h_c], causal[h_a:h_a + h_c], dv=2 * DH)
        pc = pieces["moba"]["of"]
        o_d, = paged("moba", U_QD // 8, [
            dict(q_units=[h * g_d + g for g in range(g_d)], k_piece=pc(0, h), v_pieces=[pc(1, h)],
                 bias_head=h * g_d, mask_head=h, out=(0, h * g_d)) for h in range(hkv_d)],
            [h_d], causal[h_a + h_c:h_a + h_c + h_d], mask=moba_mask, mask_blk=MOBA_BLOCK)

    lam_init = 0.8 - 0.6 * math.exp(-0.3 * layer)
    flat = lambda a: a.reshape(M, a.shape[-1])
    o = _combine(flat(o_cmp), flat(o_sel), flat(o_win), flat(proj), flat(o_b), flat(o_c[0]), flat(o_c[1]),
                 flat(o_d), prm["mix_gain"], prm["diff_lambda"], layer, tm=tiles["comb_tm"], lam_init=lam_init)

    if T % tiles["w_out"]["tm"] == 0:
        gates = dict(gate1=gt1, gate2=gt2, rows_per_batch=T)
    else:
        gates = dict(gate1=jnp.repeat(gt1[:, 0], T, axis=0), gate2=jnp.repeat(gt2[:, 0], T, axis=0), rows_per_batch=None)
    x = _matmul(o, prm["w_out"], layer, res=x, gate=gates["gate1"], rows_per_batch=gates["rows_per_batch"],
                **tiles["w_out"])
    h2 = _norm(x, prm["norm_ffn"], layer, tm=tm_n, sc=sc2, sh=sh2, rows_per_batch=T, out_dtype=BF16)
    act = _matmul(h2, prm["w_gate"], layer, w2=prm["w_up"], out_dtype=BF16, **tiles["w_ff"])
    x = _matmul(act, prm["w_down"], layer, res=x, gate=gates["gate2"], rows_per_batch=gates["rows_per_batch"],
                **tiles["w_down"])
    return x, new


def _group_tables(t5_table, dims, T, q_off, tiles, tk_pad, win_len):
    h_a, hkv_a, h_b, hkv_b, h_c, hkv_c, h_d, hkv_d = dims
    tab_t = t5_table.astype(F32).T
    tab_a = tab_t[:h_a]
    tq, tk = tiles["tq"], tiles["tk"]
    Tk = q_off + T
    n = (Tk - CMP_LEN) // CMP_STRIDE + 2
    n_sel = -(-Tk // SEL_BLOCK)
    tabs = {"overlap": jnp.asarray(_overlap_np(n, n_sel, _round_up(n_sel, LANE)))}
    bd = jnp.concatenate([_dist_table(tab_t), _dist_table(None)], axis=0)
    n_bias = tab_t.shape[0]
    if q_off == 0:
        assert tq == tk
        tw = tiles["win_t"]
        nband = (WINDOW - 1 + tw - 1) // tw + 1
        tabs.update(cmp=_cmp_bias_prompt(bd[:h_a], T, n), causal=_prompt_bias(bd, tq, 3),
                    win=_prompt_bias(bd[:h_a], tw, min(nband, T // tw), WINDOW))
    else:
        assert win_len % 8 == 0 and q_off >= win_len - T
        tabs.update(cmp=_cmp_bias(tab_a, q_off + np.arange(T), n), causal=_decode_bias(bd, T, q_off, tk_pad, tk),
                    win=_decode_bias(bd[:h_a], T, win_len - T, win_len, win_len, WINDOW))
    tabs.update(a=(tabs["causal"], 0, False), c=(tabs["causal"], h_a, False), d=(tabs["causal"], h_a + h_c, False),
                mask=(tabs["causal"], n_bias, True))
    return tabs


PROMPT_TILES = dict(
    tq=512, tk=512, win_t=512, cmp_tq=512, norm_tm=256, comb_tm=256,
    w_in=dict(tm=1024, tn=512, tk=4096), w_out=dict(tm=1024, tn=512, tk=4096),
    w_ff=dict(tm=1024, tn=256, tk=4096), w_down=dict(tm=2048, tn=512, tk=1024),
)


def _sample_tiles(M, T, tk):
    return dict(
        tq=T, tk=tk, cmp_tq=T, norm_tm=T, comb_tm=M,
        w_in=dict(tm=M, tn=512, tk=4096), w_out=dict(tm=M, tn=512, tk=4096),
        w_ff=dict(tm=M, tn=256, tk=4096), w_down=dict(tm=M, tn=1024, tk=1024),
    )


def kernel(x_prompt, x_sample, cache_nsa, state_nsa_win, cache_fox, cache_fox_logf, cache_diff, cache_moba,
           page_table, c_prompt, c_sample, t5_table, ada_w, ada_b, norm_attn, norm_ffn, w_in, w_out, mix_gain,
           nsa_cmp_pos, nsa_cmp_w1, nsa_cmp_w2, fox_fbias, diff_lambda, w_gate, w_up, w_down, final_norm):
    depth = w_in.shape[0]
    B, T, D = x_prompt.shape
    Bs, Ts, _ = x_sample.shape
    hkv_a, hkv_b, hkv_c, hkv_d = cache_nsa.shape[4], cache_fox.shape[4], cache_diff.shape[4], cache_moba.shape[4]
    h_b, h_c = fox_fbias.shape[1], diff_lambda.shape[1]
    h_a = (D // DH - h_b - 2 * h_c) // 2
    h_d = h_a
    dims = (h_a, hkv_a, h_b, hkv_b, h_c, hkv_c, h_d, hkv_d)
    page = cache_nsa.shape[2]
    n_pages = page_table.shape[1]
    past_len = n_pages * page
    pages_per_step = math.gcd(PAGED_PER_STEP, n_pages)
    tk_s = pages_per_step * page
    tk_pad = (n_pages // pages_per_step + 1) * tk_s
    win_len = state_nsa_win.shape[2] + Ts

    p_tiles = dict(PROMPT_TILES, wbuf=state_nsa_win.shape[2])
    s_tiles = _sample_tiles(Bs * Ts, Ts, tk_s)
    tabs_p = _group_tables(t5_table, dims, T, 0, p_tiles, T, 0)
    tabs_s = _group_tables(t5_table, dims, Ts, past_len, s_tiles, tk_pad, win_len)

    def piece_rows(c):
        d, n_phys, pg, n_t, hkv, w = c.shape
        halves = w // LANE
        c = c.reshape(d, n_phys, pg, n_t, hkv, halves, LANE).transpose(0, 1, 2, 3, 5, 4, 6)
        info = dict(n=n_t * hkv * halves, n_t=n_t, hkv=hkv, halves=halves,
                    of=lambda t, h, half=0: (t * halves + half) * hkv + h)
        return c.reshape(d, n_phys, pg * info["n"], LANE), info

    rows = {name: piece_rows(c) for name, c in
            (("nsa", cache_nsa), ("fox", cache_fox), ("diff", cache_diff), ("moba", cache_moba))}
    past = dict({name: r[0] for name, r in rows.items()}, pieces={name: r[1] for name, r in rows.items()},
                logf=cache_fox_logf,
                win=state_nsa_win.reshape(depth, Bs, state_nsa_win.shape[2], -1),
                pt=page_table, len=past_len, page=page)

    n_c = _round_up(B + Bs, 8)
    c_all = jnp.pad(jnp.concatenate([c_prompt, c_sample], axis=0), ((0, n_c - B - Bs), (0, 0)))
    xp, xs = x_prompt.reshape(B * T, D), x_sample.reshape(Bs * Ts, D)
    names = ("nsa", "win", "fox", "logf", "diff", "moba")
    st_p = {n: [] for n in names}
    st_s = {n: [] for n in names}
    r3 = lambda a: a.reshape(a.shape[0], 1, a.shape[1])
    for l in range(depth):
        mod = _matmul(c_all, ada_w, l, bias=ada_b[l][None], silu_in=True, tm=n_c, tn=512, tk=D)
        prm = dict(norm_attn=r3(norm_attn), norm_ffn=r3(norm_ffn), w_in_r=_reorder_w_in(w_in[l], dims), w_out=w_out,
                   mix_gain=r3(mix_gain), cmp_pos=nsa_cmp_pos, cmp_w1=nsa_cmp_w1, cmp_w2=nsa_cmp_w2,
                   fox_fbias=fox_fbias[l][None], diff_lambda=diff_lambda, w_gate=w_gate, w_up=w_up, w_down=w_down)
        xp, new_p = _layer(xp, B, T, mod[:B], l, None, prm, tabs_p, dims, p_tiles)
        xs, new_s = _layer(xs, Bs, Ts, mod[B:B + Bs], l, past, prm, tabs_s, dims, s_tiles)
        for n in names:
            st_p[n].append(new_p[n])
            st_s[n].append(new_s[n])
    y_p = _norm(xp, final_norm[None], None, tm=p_tiles["norm_tm"]).reshape(B, T, D)
    y_s = _norm(xs, final_norm[None], None, tm=s_tiles["norm_tm"]).reshape(Bs, Ts, D)

    def stack(st, name, tail):
        a = jnp.stack(st[name])
        return a.reshape(a.shape[:3] + tail)

    kv = lambda n, hk, w: (n, hk, w)
    shapes = {"nsa": kv(4, hkv_a, DH), "win": kv(2, hkv_a, DH), "fox": kv(2, hkv_b, DH), "logf": (h_b,),
              "diff": kv(2, hkv_c, 2 * DH), "moba": kv(2, hkv_d, DH)}
    out = [y_p, y_s]
    for n in names:
        out += [stack(st_p, n, shapes[n]), stack(st_s, n, shapes[n])]
    return tuple(out)
```

```python
import functools
import math

import jax
import jax.numpy as jnp
import numpy as np
from jax import lax
from jax.experimental import pallas as pl
from jax.experimental.pallas import tpu as pltpu

DH = 128
CMP_LEN = 32
CMP_STRIDE = 16
SEL_BLOCK = 64
SEL_TOP = 16
WINDOW = 512
MOBA_BLOCK = 256
MOBA_TOP = 3
N_BUCKETS = 32
MAX_DIST = 128
NEG = -1e30
FORCE = 1e4
RMS_EPS = 1e-6
BELOW_NEG = -3e38

LANE = 128
V7X_VMEM_LIMIT_BYTES = 60000 * 1024

U_QA, U_QB, U_QD, U_QC = 0, 8, 16, 24
U_NSA, U_WIN, U_FOX, U_DIFF, U_MOBA, U_AG, U_BF, U_TOTAL = 32, 40, 44, 48, 56, 60, 61, 62

BF16 = jnp.bfloat16
F32 = jnp.float32


def _cparams(*sem):
    return pltpu.CompilerParams(dimension_semantics=sem, vmem_limit_bytes=V7X_VMEM_LIMIT_BYTES)


def _silu(x):
    return x * jax.nn.sigmoid(x)


def _div_pow2(x, d):
    assert d & (d - 1) == 0
    return lax.shift_right_logical(x, jnp.int32(d.bit_length() - 1))


def _lane_src(arr, unit0, stride=1, jstride=0):
    def make_spec(rows, width, to_brhj):
        def index_map(*g):
            b, r, h, j = to_brhj(*g)
            return (b, r, (unit0 * DH) // width + h * stride + j * jstride)
        return pl.BlockSpec((None, rows, width), index_map)
    return arr, make_spec


def _mm_body(*refs, nk, tk, k_rem, silu_in, dual, has_bias, has_res):
    it = iter(refs)
    x_ref, w_ref = next(it), next(it)
    w2_ref = next(it) if dual else None
    b_ref = next(it) if has_bias else None
    r_ref, g_ref = (next(it), next(it)) if has_res else (None, None)
    o_ref, acc = next(it), next(it)
    acc2 = next(it) if dual else None
    k = pl.program_id(2)

    @pl.when(k == 0)
    def _():
        acc[...] = jnp.zeros_like(acc)
        if dual:
            acc2[...] = jnp.zeros_like(acc2)

    def accumulate(overhang):
        xv = x_ref[...]
        if silu_in:
            xv = _silu(xv.astype(F32))
        xv = xv.astype(BF16)
        wv = w_ref[...].astype(BF16)
        w2v = w2_ref[...].astype(BF16) if dual else None
        if overhang:
            xv = jnp.where(lax.broadcasted_iota(jnp.int32, xv.shape, 1) < k_rem, xv, jnp.zeros_like(xv))
            rows = lax.broadcasted_iota(jnp.int32, wv.shape, 0) < k_rem
            wv = jnp.where(rows, wv, jnp.zeros_like(wv))
            if dual:
                w2v = jnp.where(rows, w2v, jnp.zeros_like(w2v))
        acc[...] += jnp.dot(xv, wv, preferred_element_type=F32)
        if dual:
            acc2[...] += jnp.dot(xv, w2v, preferred_element_type=F32)

    if k_rem:
        pl.when(k < nk - 1)(lambda: accumulate(False))
        pl.when(k == nk - 1)(lambda: accumulate(True))
    else:
        accumulate(False)

    @pl.when(k == nk - 1)
    def _():
        r = acc[...]
        if dual:
            r = _silu(r) * acc2[...]
        if has_bias:
            r = r + b_ref[...]
        if has_res:
            r = r_ref[...] + g_ref[...] * r
        o_ref[...] = r.astype(o_ref.dtype)


def _matmul(x, w, layer, *, tm, tn, tk, w2=None, bias=None, res=None, gate=None, rows_per_batch=None,
            silu_in=False, out_dtype=F32):
    M, K = x.shape
    N = w.shape[-1]
    assert M % tm == 0
    nk = pl.cdiv(K, tk)
    k_rem = K % tk
    grid = (M // tm, pl.cdiv(N, tn), nk)
    if layer is None:
        w_spec = pl.BlockSpec((tk, tn), lambda m, n, k: (k, n))
    else:
        w_spec = pl.BlockSpec((None, tk, tn), lambda m, n, k: (layer, k, n))
    in_specs = [pl.BlockSpec((tm, tk), lambda m, n, k: (m, k)), w_spec]
    args = [x, w]
    if w2 is not None:
        in_specs.append(w_spec)
        args.append(w2)
    if bias is not None:
        in_specs.append(pl.BlockSpec((1, tn), lambda m, n, k: (0, n)))
        args.append(bias)
    if res is not None:
        in_specs.append(pl.BlockSpec((tm, tn), lambda m, n, k: (m, n)))
        args.append(res)
        if gate.ndim == 2:
            in_specs.append(pl.BlockSpec((tm, tn), lambda m, n, k: (m, n)))
        else:
            assert rows_per_batch % tm == 0
            per = rows_per_batch // tm
            in_specs.append(pl.BlockSpec((None, 1, tn), lambda m, n, k: (m // per, 0, n)))
        args.append(gate)
    scratch = [pltpu.VMEM((tm, tn), F32)] * (2 if w2 is not None else 1)
    body = functools.partial(_mm_body, nk=nk, tk=tk, k_rem=k_rem, silu_in=silu_in, dual=w2 is not None,
                             has_bias=bias is not None, has_res=res is not None)
    return pl.pallas_call(
        body, grid=grid, in_specs=in_specs,
        out_specs=pl.BlockSpec((tm, tn), lambda m, n, k: (m, n)),
        out_shape=jax.ShapeDtypeStruct((M, N), out_dtype),
        scratch_shapes=scratch,
        compiler_params=_cparams("parallel", "parallel", "arbitrary"),
    )(*args)


def _norm_body(*refs, modulated):
    if modulated:
        x_ref, g_ref, sc_ref, sh_ref, o_ref = refs
    else:
        x_ref, g_ref, o_ref = refs
    x = x_ref[...]
    y = x * lax.rsqrt(jnp.mean(x * x, axis=-1, keepdims=True) + RMS_EPS) * g_ref[...]
    if modulated:
        y = y * (1.0 + sc_ref[...]) + sh_ref[...]
    o_ref[...] = y.astype(o_ref.dtype)


def _norm(x, g, layer, *, tm, sc=None, sh=None, rows_per_batch=None, out_dtype=F32):
    M, D = x.shape
    modulated = sc is not None
    if layer is None:
        g_spec = pl.BlockSpec((1, D), lambda m: (0, 0))
    else:
        g_spec = pl.BlockSpec((None, 1, D), lambda m: (layer, 0, 0))
    in_specs = [pl.BlockSpec((tm, D), lambda m: (m, 0)), g_spec]
    args = [x, g]
    if modulated:
        assert rows_per_batch % tm == 0
        per = rows_per_batch // tm
        mod_spec = pl.BlockSpec((None, 1, D), lambda m: (m // per, 0, 0))
        in_specs += [mod_spec, mod_spec]
        args += [sc, sh]
    return pl.pallas_call(
        functools.partial(_norm_body, modulated=modulated), grid=(M // tm,), in_specs=in_specs,
        out_specs=pl.BlockSpec((tm, D), lambda m: (m, 0)),
        out_shape=jax.ShapeDtypeStruct((M, D), out_dtype),
        compiler_params=_cparams("parallel"),
    )(*args)


GATHER_PAGES_PER_STEP = 8


def _gather_body(pt_ref, *refs, n_groups, per, page, slots, n_pieces):
    cache_refs, new_ref, o_ref = refs[:per], refs[per], refs[per + 1]
    j = pl.program_id(1)

    @pl.when(j < n_groups)
    def _():
        for p in range(per):
            if slots is None:
                o_ref[p * page:(p + 1) * page] = cache_refs[p][...]
            else:
                for s, src_slot in enumerate(slots):
                    o_ref[p * page:(p + 1) * page, s * LANE:(s + 1) * LANE] = (
                        cache_refs[p][pl.ds(src_slot, page, stride=n_pieces), :])

    @pl.when(j == n_groups)
    def _():
        o_ref[0:page] = new_ref[...]
        if per > 1:
            o_ref[page:per * page] = jnp.zeros(((per - 1) * page,) + o_ref.shape[1:], o_ref.dtype)


def _gather_pages(cache, layer, page_table, new_page, slots=None, n_pieces=1):
    page, W = new_page.shape[1:]
    B, n_pages = page_table.shape
    per = math.gcd(GATHER_PAGES_PER_STEP, n_pages)
    n_groups = n_pages // per
    assert cache.shape[2:] == ((page, W) if slots is None else (page * n_pieces, LANE))
    assert slots is None or W == len(slots) * LANE

    def page_spec(p):
        return pl.BlockSpec((None, None) + cache.shape[2:],
                            lambda b, j, pt: (layer, pt[b, jnp.minimum(j, n_groups - 1) * per + p], 0, 0))

    grid_spec = pltpu.PrefetchScalarGridSpec(
        num_scalar_prefetch=1, grid=(B, n_groups + 1),
        in_specs=[page_spec(p) for p in range(per)] + [pl.BlockSpec((None, page, W), lambda b, j, pt: (b, 0, 0))],
        out_specs=pl.BlockSpec((None, per * page, W), lambda b, j, pt: (b, j, 0)),
    )
    return pl.pallas_call(
        functools.partial(_gather_body, n_groups=n_groups, per=per, page=page, slots=slots, n_pieces=n_pieces),
        grid_spec=grid_spec,
        out_shape=jax.ShapeDtypeStruct((B, (n_pages + 1) * page, W), cache.dtype),
        compiler_params=_cparams("parallel", "arbitrary"),
    )(page_table, *([cache] * per), new_page)


CUM_CHUNK = 256


def _cum_body(*refs, n_past, n_new, nh):
    if n_past:
        past_ref, raw_ref, fb_ref, lf_ref, cum_ref = refs
    else:
        raw_ref, fb_ref, lf_ref, cum_ref = refs
    z = raw_ref[...][:, :nh] + fb_ref[...]
    lf = -(jnp.maximum(-z, 0.0) + jnp.log1p(jnp.exp(-jnp.abs(z))))
    lf_ref[...] = lf

    def tri(n):
        return (lax.broadcasted_iota(jnp.int32, (n, n), 0) >= lax.broadcasted_iota(jnp.int32, (n, n), 1)).astype(F32)

    def scan_rows(src_ref, dst_off, n, carry):
        c = min(CUM_CHUNK, n)
        assert n % c == 0
        t = tri(c)
        for i in range(n // c):
            cum_ref[dst_off + i * c:dst_off + (i + 1) * c, :] = jnp.dot(
                t, src_ref[i * c:(i + 1) * c, :], preferred_element_type=F32, precision=lax.Precision.HIGHEST)
        for i in range(n // c):
            rows = slice(dst_off + i * c, dst_off + (i + 1) * c)
            total = cum_ref[dst_off + (i + 1) * c - 1:dst_off + (i + 1) * c, :]
            cum_ref[rows, :] = cum_ref[rows, :] + carry
            carry = carry + total
        return carry

    carry = jnp.zeros((1, nh), F32)
    if n_past:
        carry = scan_rows(past_ref, 0, n_past, carry)
    scan_rows(lf_ref, n_past, n_new, carry)


def _fox_cum(raw_f, fbias, past_lf, n_past=0):
    B, Tn, _ = raw_f.shape
    nh = fbias.shape[-1]
    in_specs, args = [], []
    if n_past:
        in_specs.append(pl.BlockSpec((None, n_past, nh), lambda b: (b, 0, 0)))
        args.append(past_lf)
    in_specs += [pl.BlockSpec((None, Tn, LANE), lambda b: (b, 0, 0)), pl.BlockSpec((1, nh), lambda b: (0, 0))]
    args += [raw_f, fbias]
    return pl.pallas_call(
        functools.partial(_cum_body, n_past=n_past, n_new=Tn, nh=nh), grid=(B,), in_specs=in_specs,
        out_specs=[pl.BlockSpec((None, Tn, nh), lambda b: (b, 0, 0)),
                   pl.BlockSpec((None, n_past + Tn, nh), lambda b: (b, 0, 0))],
        out_shape=[jax.ShapeDtypeStruct((B, Tn, nh), F32), jax.ShapeDtypeStruct((B, n_past + Tn, nh), F32)],
        compiler_params=_cparams("parallel"),
    )(*args)


def _gelu_tanh(x):
    return 0.5 * x * (1.0 + jnp.tanh(math.sqrt(2.0 / math.pi) * (x + 0.044715 * (x * x * x))))


def _compress_body(x_ref, pos_ref, w1_ref, w2_ref, o_ref, *, n):
    hidden = w1_ref.shape[-1]
    acc_lo = jnp.zeros((n, hidden), F32)
    acc_hi = jnp.zeros((n, hidden), F32)
    for rho in range(CMP_STRIDE):
        xr = x_ref[pl.ds(rho, n, stride=CMP_STRIDE), :]
        lo = (xr + pos_ref[rho:rho + 1, :]).astype(BF16)
        hi = (xr + pos_ref[rho + CMP_STRIDE:rho + CMP_STRIDE + 1, :]).astype(BF16)
        w_lo = w1_ref[rho * DH:(rho + 1) * DH, :].astype(BF16)
        w_hi = w1_ref[(rho + CMP_STRIDE) * DH:(rho + CMP_STRIDE + 1) * DH, :].astype(BF16)
        acc_lo += jnp.dot(lo, w_lo, preferred_element_type=F32)
        acc_hi += jnp.dot(hi, w_hi, preferred_element_type=F32)
    hid = acc_lo + pltpu.roll(acc_hi, n - 1, 0)
    o_ref[...] = jnp.dot(_gelu_tanh(hid).astype(BF16), w2_ref[...].astype(BF16), preferred_element_type=F32)


def _compress(src, hkv, n, layer, pos, w1, w2):
    kv, make_spec = src
    B = kv.shape[0]
    hidden = w1.shape[-1]
    return pl.pallas_call(
        functools.partial(_compress_body, n=n), grid=(B, 2, hkv),
        in_specs=[
            make_spec(n * CMP_STRIDE, DH, lambda b, j, h: (b, 0, h, j)),
            pl.BlockSpec((None, None, CMP_LEN, DH), lambda b, j, h: (layer, j, 0, 0)),
            pl.BlockSpec((None, None, CMP_LEN * DH, hidden), lambda b, j, h: (layer, j, 0, 0)),
            pl.BlockSpec((None, None, hidden, DH), lambda b, j, h: (layer, j, 0, 0)),
        ],
        out_specs=pl.BlockSpec((None, None, None, n, DH), lambda b, j, h: (b, j, h, 0, 0)),
        out_shape=jax.ShapeDtypeStruct((B, 2, hkv, n, DH), F32),
        compiler_params=_cparams("parallel", "arbitrary", "arbitrary"),
    )(kv, pos, w1, w2)


def _take_top(score, lane_ids, count):
    sel = jnp.zeros(score.shape, F32)
    taken = []
    big = jnp.int32(score.shape[-1])
    for _ in range(count):
        m = jnp.max(score, axis=-1, keepdims=True)
        idx = jnp.min(jnp.where(score == m, lane_ids, big), axis=-1, keepdims=True)
        hit = lane_ids == idx
        sel = jnp.where(hit, 1.0, sel)
        score = jnp.where(hit, BELOW_NEG, score)
        taken.append((hit, m))
    return sel, taken


def _cmp_body(q_ref, kc_ref, vc_ref, bias_ref, ov_ref, o_ref, sel_ref, *, G, tq, q_off, n_sel, top):
    qi = pl.program_id(2)
    kc = kc_ref[...]
    vc = vc_ref[...].astype(BF16)
    scale = DH ** -0.5
    psum = None
    outs = []
    for g in range(G):
        qg = q_ref[:, g * DH:(g + 1) * DH]
        s = lax.dot_general(qg, kc, (((1,), (1,)), ((), ())), preferred_element_type=F32,
                            precision=lax.Precision.HIGHEST) * scale + bias_ref[g]
        m = jnp.max(s, axis=-1, keepdims=True)
        e = jnp.where(s > NEG / 2, jnp.exp(s - m), 0.0)
        p = e / jnp.maximum(jnp.sum(e, axis=-1, keepdims=True), 1e-30)
        outs.append(jnp.dot(p.astype(BF16), vc, preferred_element_type=F32))
        psum = p if psum is None else psum + p
    o_ref[...] = jnp.concatenate(outs, axis=1)
    imp = jnp.dot(psum, ov_ref[...], preferred_element_type=F32, precision=lax.Precision.HIGHEST)
    shape = imp.shape
    jb = lax.broadcasted_iota(jnp.int32, shape, 1)
    qpos = q_off + qi * tq + lax.broadcasted_iota(jnp.int32, shape, 0)
    cur = _div_pow2(qpos, SEL_BLOCK)
    forced = (jb == 0) | (jb == cur) | (jb == cur - 1)
    score = jnp.where(jb <= cur, imp + jnp.where(forced, FORCE, 0.0), NEG)
    score = jnp.where(jb < n_sel, score, BELOW_NEG)
    sel, _ = _take_top(score, jb, top)
    sel_ref[...] = jnp.where((jb <= cur) & (sel > 0.5), 0.0, NEG)


def _nsa_cmp(q, u_q, cmp_kv, bias, overlap, *, hkv, G, tq, q_off, n_sel):
    B, Tq, _ = q.shape
    n = cmp_kv.shape[3]
    nsp = overlap.shape[1]
    top = min(SEL_TOP, n_sel)
    body = functools.partial(_cmp_body, G=G, tq=tq, q_off=q_off, n_sel=n_sel, top=top)
    return pl.pallas_call(
        body, grid=(B, hkv, Tq // tq),
        in_specs=[
            pl.BlockSpec((None, tq, G * DH), lambda b, h, i: (b, i, u_q // G + h)),
            pl.BlockSpec((None, None, None, n, DH), lambda b, h, i: (b, 0, h, 0, 0)),
            pl.BlockSpec((None, None, None, n, DH), lambda b, h, i: (b, 1, h, 0, 0)),
            pl.BlockSpec((G, tq, n), lambda b, h, i: (h, i, 0)),
            pl.BlockSpec((n, nsp), lambda b, h, i: (0, 0)),
        ],
        out_specs=[pl.BlockSpec((None, tq, G * DH), lambda b, h, i: (b, i, h)),
                   pl.BlockSpec((None, None, tq, nsp), lambda b, h, i: (b, h, i, 0))],
        out_shape=[jax.ShapeDtypeStruct((B, Tq, hkv * G * DH), F32),
                   jax.ShapeDtypeStruct((B, hkv, Tq, nsp), F32)],
        compiler_params=_cparams("parallel", "parallel", "arbitrary"),
    )(q, cmp_kv, cmp_kv, bias, overlap)


def _block_means_body(pt_ref, *refs, per, page, n_pieces, k_pieces):
    page_refs, o_ref = refs[:per], refs[per]
    for h, piece in enumerate(k_pieces):
        k = jnp.concatenate([ref[pl.ds(piece, page, stride=n_pieces), :] for ref in page_refs], axis=0)
        o_ref[h] = jnp.sum(k.reshape(per * page // MOBA_BLOCK, MOBA_BLOCK, DH), axis=1) * (1.0 / MOBA_BLOCK)


def _block_means_paged(cache, n_pieces, layer, page_table, k_pieces):
    B, n_pages = page_table.shape
    page = cache.shape[2] // n_pieces
    per = math.gcd(MEANS_PAGES_PER_STEP, n_pages)
    blocks = per * page // MOBA_BLOCK
    assert (per * page) % MOBA_BLOCK == 0 and blocks % 8 == 0

    def page_spec(p):
        return pl.BlockSpec((None, None, page * n_pieces, LANE), lambda b, j, pt: (layer, pt[b, j * per + p], 0, 0))

    grid_spec = pltpu.PrefetchScalarGridSpec(
        num_scalar_prefetch=1, grid=(B, n_pages // per), in_specs=[page_spec(p) for p in range(per)],
        out_specs=pl.BlockSpec((None, len(k_pieces), blocks, DH), lambda b, j, pt: (b, 0, j, 0)))
    body = functools.partial(_block_means_body, per=per, page=page, n_pieces=n_pieces, k_pieces=k_pieces)
    return pl.pallas_call(
        body, grid_spec=grid_spec,
        out_shape=jax.ShapeDtypeStruct((B, len(k_pieces), n_pages * page // MOBA_BLOCK, DH), F32),
        compiler_params=_cparams("parallel", "arbitrary"),
    )(page_table, *([cache] * per))


MEANS_PAGES_PER_STEP = 16


def _moba_gate_body(q_ref, k_ref, m_ref, *, G, n_full, nbp, q_off, top, is_means):
    Tq = q_ref.shape[0]
    shape = (Tq, nbp)
    jb = lax.broadcasted_iota(jnp.int32, shape, 1)
    own = _div_pow2(q_off + lax.broadcasted_iota(jnp.int32, shape, 0), MOBA_BLOCK)
    if n_full > 0:
        if is_means:
            kmean = k_ref[...]
        else:
            kmean = jnp.sum(k_ref[...].reshape(n_full, MOBA_BLOCK, DH), axis=1) * (1.0 / MOBA_BLOCK)
        if nbp > n_full:
            kmean = jnp.concatenate([kmean, jnp.zeros((nbp - n_full, DH), F32)], axis=0)
    for g in range(G):
        mask = jnp.where(jb == own, 0.0, NEG)
        if n_full > 0:
            gate = lax.dot_general(q_ref[:, g * DH:(g + 1) * DH], kmean, (((1,), (1,)), ((), ())),
                                   preferred_element_type=F32, precision=lax.Precision.HIGHEST)
            score = jnp.where((jb < own) & (jb < n_full), gate, NEG)
            score = jnp.where(jb < n_full, score, BELOW_NEG)
            _, taken = _take_top(score, jb, top)
            for hit, val in taken:
                mask = jnp.where(hit & (val > NEG / 2), 0.0, mask)
        m_ref[g] = mask


def _moba_gate(q, u_q, ksrc, *, hkv, G, Tk, q_off, nbp, means=None):
    B, Tq, _ = q.shape
    n_full = Tk // MOBA_BLOCK
    top = min(MOBA_TOP, n_full)
    if means is None:
        k, k_spec = ksrc
        k_in = k_spec(max(n_full, 1) * MOBA_BLOCK, DH, lambda b, h: (b, 0, h, 0))
    else:
        assert means.shape[2] == n_full
        k, k_in = means, pl.BlockSpec((None, None, n_full, DH), lambda b, h: (b, h, 0, 0))
    body = functools.partial(_moba_gate_body, G=G, n_full=n_full, nbp=nbp, q_off=q_off, top=top,
                             is_means=means is not None)
    return pl.pallas_call(
        body, grid=(B, hkv),
        in_specs=[pl.BlockSpec((None, Tq, G * DH), lambda b, h: (b, 0, u_q // G + h)), k_in],
        out_specs=pl.BlockSpec((None, None, G, Tq, nbp), lambda b, h: (b, h, 0, 0, 0)),
        out_shape=jax.ShapeDtypeStruct((B, hkv, G, Tq, nbp), F32),
        compiler_params=_cparams("parallel", "arbitrary"),
    )(q, k)


PAIR_Q, PAIR_K, PAIR_VARIANT, PAIR_FLAGS = 0, 1, 2, 3
PAIR_FIRST, PAIR_LAST = 1, 2


def _flash_body(pairs_ref, *refs, G, Gm, tq, tk, dv, fox, mask_blk):
    it = iter(refs)
    q_ref, k_ref, v_ref, bias_ref = next(it), next(it), next(it), next(it)
    cq_ref, ck_ref = (next(it), next(it)) if fox else (None, None)
    mask_ref = next(it) if mask_blk else None
    o_ref, m_sc, acc_sc = next(it), next(it), next(it)
    pair = pl.program_id(2)
    ki = pairs_ref[PAIR_K, pair]
    flags = pairs_ref[PAIR_FLAGS, pair]

    @pl.when(flags % 2 == PAIR_FIRST)
    def _():
        m_sc[...] = jnp.full(m_sc.shape, NEG, F32)
        acc_sc[...] = jnp.zeros_like(acc_sc)

    def update():
        k = k_ref[...].astype(BF16)
        v = jnp.concatenate([v_ref[...].astype(BF16), jnp.ones((tk, LANE), BF16)], axis=1)
        q = jnp.concatenate([q_ref[:, g * DH:(g + 1) * DH] for g in range(G)], axis=0)
        q = (q * (DH ** -0.5)).astype(BF16)
        s = lax.dot_general(q, k, (((1,), (1,)), ((), ())), preferred_element_type=F32)
        s = s.reshape(G, tq, tk) + bias_ref[...]
        if fox:
            s = s + (cq_ref[...] - ck_ref[...])
        if mask_blk:
            nbp = mask_ref.shape[-1]
            kpos = ki * tk + lax.broadcasted_iota(jnp.int32, (nbp, tk), 1)
            lo = lax.broadcasted_iota(jnp.int32, (nbp, tk), 0) * mask_blk
            expand = jnp.where((kpos >= lo) & (kpos < lo + mask_blk), 1.0, 0.0).astype(BF16)
            picked = jnp.dot(mask_ref[...].reshape(Gm * tq, nbp).astype(BF16), expand, preferred_element_type=F32)
            s = s + picked.reshape(Gm, tq, tk)
        m_old = m_sc[...]
        m_new = jnp.maximum(m_old, jnp.max(s, axis=-1, keepdims=True))
        m_use = jnp.where(m_new < NEG / 2, 0.0, m_new)
        p = jnp.exp((s - m_use).astype(BF16))
        alpha = jnp.exp(m_old - m_use)
        pv = jnp.dot(p.reshape(G * tq, tk), v, preferred_element_type=F32)
        acc_sc[...] = alpha * acc_sc[...] + pv.reshape(G, tq, dv + LANE)
        m_sc[...] = m_new

    update()

    @pl.when(flags >= PAIR_LAST)
    def _():
        acc = acc_sc[...]
        o = acc[:, :, :dv] / jnp.maximum(acc[:, :, dv:dv + 1], 1e-30)
        for g in range(G):
            o_ref[:, g * dv:(g + 1) * dv] = o[g]


def _flash(q, u_q, ksrc, vsrc, bias, *, hkv, G, tq, tk, dv, mode, nsteps, fox=None, mask=None, mask_blk=0,
           q_stride=1):
    B, Tq, _ = q.shape
    nq = Tq // tq
    bias, bias_h0, bias_shared = bias
    NB = bias.shape[1]
    Gb = 1 if bias_shared else G
    assert bias_h0 % Gb == 0

    if mode == "causal":
        pairs = [(i, s, min(i - s, NB - 1)) for i in range(nq) for s in range(min(i + 1, nsteps))]
    elif mode == "band":
        pairs = [(i, i - d, d) for i in range(nq) for d in range(min(nsteps, NB) - 1, -1, -1) if i - d >= 0]
    else:
        pairs = [(i, s, s) for i in range(nq) for s in range(nsteps)]
    table = np.zeros((4, len(pairs)), np.int32)
    for n, (i, s, var) in enumerate(pairs):
        first = n == 0 or pairs[n - 1][0] != i
        last = n == len(pairs) - 1 or pairs[n + 1][0] != i
        table[:, n] = (i, s, var, PAIR_FIRST * first + PAIR_LAST * last)
    qt = lambda t, p: t[PAIR_Q, p]
    kt = lambda t, p: t[PAIR_K, p]

    kv_index = lambda b, h, p, t: (b, kt(t, p), h, 0)
    in_specs = [
        pl.BlockSpec((None, tq, G * DH), lambda b, h, p, t: (b, qt(t, p), u_q // G + h * q_stride)),
        ksrc[1](tk, DH, kv_index),
        vsrc[1](tk, dv, kv_index),
        pl.BlockSpec((Gb, None, tq, tk),
                     lambda b, h, p, t: (bias_h0 // Gb + (0 if bias_shared else h), t[PAIR_VARIANT, p], 0, 0)),
    ]
    args = [q, ksrc[0], vsrc[0], bias]
    if fox is not None:
        in_specs += [pl.BlockSpec((None, G, tq, 1), lambda b, h, p, t: (b, h, qt(t, p), 0)),
                     pl.BlockSpec((None, G, 1, tk), lambda b, h, p, t: (b, h, 0, kt(t, p)))]
        args += list(fox)
    Gm = 0
    if mask is not None:
        Gm, nbp = mask.shape[2], mask.shape[4]
        in_specs.append(pl.BlockSpec((None, None, Gm, tq, nbp), lambda b, h, p, t: (b, h, 0, qt(t, p), 0)))
        args.append(mask)
    body = functools.partial(_flash_body, G=G, Gm=Gm, tq=tq, tk=tk, dv=dv,
                             fox=fox is not None, mask_blk=mask_blk if mask is not None else 0)
    grid_spec = pltpu.PrefetchScalarGridSpec(
        num_scalar_prefetch=1, grid=(B, hkv, len(pairs)), in_specs=in_specs,
        out_specs=pl.BlockSpec((None, tq, G * dv), lambda b, h, p, t: (b, qt(t, p), h)),
        scratch_shapes=[pltpu.VMEM((G, tq, 1), F32), pltpu.VMEM((G, tq, dv + LANE), F32)])
    return pl.pallas_call(
        body, grid_spec=grid_spec,
        out_shape=jax.ShapeDtypeStruct((B, Tq, hkv * G * dv), F32),
        compiler_params=_cparams("parallel", "parallel", "arbitrary"),
    )(jnp.asarray(table), *args)


PAGED_PER_STEP = 32


def _flash_paged_body(pt_ref, *refs, groups, n_out, per, page, n_pieces, n_groups, dv, fox, has_mask, mask_blk):
    it = iter(refs)
    q_ref = next(it)
    page_refs = [next(it) for _ in range(per)]
    new_ref, bias_ref = next(it), next(it)
    cq_ref, ck_ref = (next(it), next(it)) if fox else (None, None)
    mask_ref = next(it) if has_mask else None
    o_refs = [next(it) for _ in range(n_out)]
    m_sc, acc_sc = next(it), next(it)
    j = pl.program_id(1)
    last = j == n_groups
    tk = per * page
    tq = q_ref.shape[0]

    @pl.when(j == 0)
    def _():
        m_sc[...] = jnp.full(m_sc.shape, NEG, F32)
        acc_sc[...] = jnp.zeros_like(acc_sc)

    def attend(sources):
        nk = len(sources) * page
        loaded = {}

        def piece(slot):
            if slot not in loaded:
                parts = [ref[pl.ds(slot, page, stride=n_pieces), :] for ref in sources]
                loaded[slot] = jnp.concatenate(parts, axis=0).astype(BF16)
            return loaded[slot]

        if has_mask:
            nbp = mask_ref.shape[-1]
            kpos = j * tk + lax.broadcasted_iota(jnp.int32, (nbp, nk), 1)
            lo = lax.broadcasted_iota(jnp.int32, (nbp, nk), 0) * mask_blk
            expand = jnp.where((kpos >= lo) & (kpos < lo + mask_blk), 1.0, 0.0).astype(BF16)
        ones = jnp.ones((nk, LANE), BF16)
        row = 0
        for grp in groups:
            G = len(grp["q_units"])
            k = piece(grp["k_piece"])
            v = jnp.concatenate([piece(s) for s in grp["v_pieces"]] + [ones], axis=1)
            q = jnp.concatenate([q_ref[:, u * DH:(u + 1) * DH] for u in grp["q_units"]], axis=0)
            q = (q * (DH ** -0.5)).astype(BF16)
            s = lax.dot_general(q, k, (((1,), (1,)), ((), ())), preferred_element_type=F32).reshape(G, tq, nk)
            b0 = grp["bias_head"]
            s = s + (bias_ref[:, :, :nk] if b0 is None else bias_ref[b0:b0 + G, :, :nk])
            if fox:
                f0 = grp["fox_head"]
                s = s + (cq_ref[f0:f0 + G] - ck_ref[f0:f0 + G, :, :nk])
            if has_mask:
                msk = mask_ref[grp["mask_head"]]
                gm = msk.shape[0]
                picked = jnp.dot(msk.reshape(gm * tq, nbp).astype(BF16), expand, preferred_element_type=F32)
                s = s + picked.reshape(gm, tq, nk)
            rows = slice(row, row + G)
            m_old = m_sc[rows]
            m_new = jnp.maximum(m_old, jnp.max(s, axis=-1, keepdims=True))
            m_use = jnp.where(m_new < NEG / 2, 0.0, m_new)
            p = jnp.exp((s - m_use).astype(BF16))
            alpha = jnp.exp(m_old - m_use)
            pv = jnp.dot(p.reshape(G * tq, nk), v, preferred_element_type=F32)
            acc_sc[rows] = alpha * acc_sc[rows] + pv.reshape(G, tq, dv + LANE)
            m_sc[rows] = m_new
            row += G

    @pl.when(j < n_groups)
    def _():
        attend(page_refs)

    @pl.when(last)
    def _():
        attend([new_ref])
        row = 0
        for grp in groups:
            G = len(grp["q_units"])
            acc = acc_sc[row:row + G]
            o = acc[:, :, :dv] / jnp.maximum(acc[:, :, dv:dv + 1], 1e-30)
            out_i, unit0 = grp["out"]
            for g in range(G):
                o_refs[out_i][:, (unit0 + g) * dv:(unit0 + g + 1) * dv] = o[g]
            row += G


def _flash_paged(q, q_block, groups, out_heads, cache, n_pieces, layer, page_table, new_page, bias, *, dv,
                 fox=None, mask=None, mask_blk=0):
    B, Tq, _ = q.shape
    n_pages = page_table.shape[1]
    page = cache.shape[2] // n_pieces
    per = math.gcd(PAGED_PER_STEP, n_pages)
    n_groups = n_pages // per
    tk = per * page
    assert bias.shape[1:] == (n_groups + 1, Tq, tk)
    n_rows = sum(len(g["q_units"]) for g in groups)
    q_w = 8 * DH

    def page_spec(p):
        return pl.BlockSpec((None, None, page * n_pieces, LANE),
                            lambda b, j, pt: (layer, pt[b, jnp.minimum(j, n_groups - 1) * per + p], 0, 0))

    in_specs = [pl.BlockSpec((None, Tq, q_w), lambda b, j, pt: (b, 0, q_block))]
    in_specs += [page_spec(p) for p in range(per)]
    in_specs += [pl.BlockSpec((None, page * n_pieces, LANE), lambda b, j, pt: (b, 0, 0)),
                 pl.BlockSpec((bias.shape[0], None, Tq, tk), lambda b, j, pt: (0, j, 0, 0))]
    args = [q] + [cache] * per + [new_page, bias]
    if fox is not None:
        H = fox[0].shape[1]
        in_specs += [pl.BlockSpec((None, H, Tq, 1), lambda b, j, pt: (b, 0, 0, 0)),
                     pl.BlockSpec((None, H, 1, tk), lambda b, j, pt: (b, 0, 0, j))]
        args += list(fox)
    if mask is not None:
        in_specs.append(pl.BlockSpec((None,) + mask.shape[1:], lambda b, j, pt: (b, 0, 0, 0, 0)))
        args.append(mask)
    body = functools.partial(_flash_paged_body, groups=groups, n_out=len(out_heads), per=per, page=page,
                             n_pieces=n_pieces, n_groups=n_groups, dv=dv, fox=fox is not None,
                             has_mask=mask is not None, mask_blk=mask_blk)
    grid_spec = pltpu.PrefetchScalarGridSpec(
        num_scalar_prefetch=1, grid=(B, n_groups + 1), in_specs=in_specs,
        out_specs=[pl.BlockSpec((None, Tq, nh * dv), lambda b, j, pt: (b, 0, 0)) for nh in out_heads],
        scratch_shapes=[pltpu.VMEM((n_rows, Tq, 1), F32), pltpu.VMEM((n_rows, Tq, dv + LANE), F32)],
    )
    return pl.pallas_call(
        body, grid_spec=grid_spec,
        out_shape=[jax.ShapeDtypeStruct((B, Tq, nh * dv), F32) for nh in out_heads],
        compiler_params=_cparams("parallel", "arbitrary"),
    )(page_table, *args)


def _head_rms(x, gain):
    return x * lax.rsqrt(jnp.mean(x * x, axis=-1, keepdims=True) + RMS_EPS) * gain


def _combine_body(cmp_ref, sel_ref, win_ref, ag_ref, b_ref, c1_ref, c2_ref, d_ref, mg_ref, lp_ref, o_ref, *,
                  h_a, h_b, h_c, h_d, lam_init):
    gates = jax.nn.sigmoid(ag_ref[...])
    col = 0
    for h in range(h_a):
        sl = slice(h * DH, (h + 1) * DH)
        o = (gates[:, 3 * h:3 * h + 1] * cmp_ref[:, sl] + gates[:, 3 * h + 1:3 * h + 2] * sel_ref[:, sl]
             + gates[:, 3 * h + 2:3 * h + 3] * win_ref[:, sl])
        o_ref[:, col:col + DH] = _head_rms(o, mg_ref[:, col:col + DH]).astype(o_ref.dtype)
        col += DH
    for h in range(h_b):
        o_ref[:, col:col + DH] = _head_rms(b_ref[:, h * DH:(h + 1) * DH], mg_ref[:, col:col + DH]).astype(o_ref.dtype)
        col += DH
    for h in range(h_c):
        lp = lp_ref[h]
        lam = (jnp.exp(jnp.sum(lp[0:1] * lp[1:2], axis=-1, keepdims=True))
               - jnp.exp(jnp.sum(lp[2:3] * lp[3:4], axis=-1, keepdims=True)) + lam_init)
        sl = slice(h * 2 * DH, (h + 1) * 2 * DH)
        o = c1_ref[:, sl] - lam * c2_ref[:, sl]
        o_ref[:, col:col + 2 * DH] = (_head_rms(o, mg_ref[:, col:col + 2 * DH]) * (1.0 - lam_init)).astype(o_ref.dtype)
        col += 2 * DH
    for h in range(h_d):
        o_ref[:, col:col + DH] = _head_rms(d_ref[:, h * DH:(h + 1) * DH], mg_ref[:, col:col + DH]).astype(o_ref.dtype)
        col += DH


def _combine(o_cmp, o_sel, o_win, proj, o_b, o_c1, o_c2, o_d, mix_gain, diff_lambda, layer, *, tm, lam_init):
    M = o_cmp.shape[0]
    D = mix_gain.shape[-1]
    h_a, h_b, h_d = o_cmp.shape[1] // DH, o_b.shape[1] // DH, o_d.shape[1] // DH
    h_c = o_c1.shape[1] // (2 * DH)
    row = lambda w: pl.BlockSpec((tm, w), lambda m: (m, 0))
    body = functools.partial(_combine_body, h_a=h_a, h_b=h_b, h_c=h_c, h_d=h_d, lam_init=lam_init)
    return pl.pallas_call(
        body, grid=(M // tm,),
        in_specs=[row(o_cmp.shape[1]), row(o_sel.shape[1]), row(o_win.shape[1]),
                  pl.BlockSpec((tm, LANE), lambda m: (m, U_AG)),
                  row(o_b.shape[1]), row(o_c1.shape[1]), row(o_c2.shape[1]), row(o_d.shape[1]),
                  pl.BlockSpec((None, 1, D), lambda m: (layer, 0, 0)),
                  pl.BlockSpec((None,) + diff_lambda.shape[1:], lambda m: (layer, 0, 0, 0))],
        out_specs=pl.BlockSpec((tm, D), lambda m: (m, 0)),
        out_shape=jax.ShapeDtypeStruct((M, D), BF16),
        compiler_params=_cparams("parallel"),
    )(o_cmp, o_sel, o_win, proj, o_b, o_c1, o_c2, o_d, mix_gain, diff_lambda)


def _t5_bucket_np(dist):
    n = np.maximum(dist, 0)
    exact = N_BUCKETS // 2
    nf = np.maximum(n, 1).astype(np.float32)
    big = exact + (np.log(nf / np.float32(exact)) / np.float32(math.log(MAX_DIST / exact))
                   * np.float32(N_BUCKETS - exact)).astype(np.int32)
    return np.where(n < exact, n, np.minimum(big, N_BUCKETS - 1)).astype(np.int32)


def _bias_from_dist(tab_t, dist, valid):
    if tab_t is None:
        return jnp.asarray(np.where(valid, 0.0, NEG).astype(np.float32))[None]
    idx = _t5_bucket_np(dist).reshape(-1)
    b = jnp.take(tab_t, jnp.asarray(idx), axis=1).reshape((tab_t.shape[0],) + dist.shape)
    return jnp.where(jnp.asarray(valid)[None], b, NEG)


def _dist_table(tab_t):
    if tab_t is None:
        return jnp.zeros((1, MAX_DIST + 1), F32)
    return jnp.take(tab_t, jnp.asarray(_t5_bucket_np(np.arange(MAX_DIST + 1))), axis=1)


def _bias_range(bd, lo, hi, window=None, descending=False):
    H, nd = bd.shape
    far = 10 ** 9
    w = far if window is None else window
    assert w >= nd - 1
    parts = []
    for seg_lo, seg_hi, kind in ((-far, 0, "neg"), (0, nd - 1, "tab"), (nd - 1, w, "far"), (w, far, "neg")):
        a, b = max(lo, seg_lo), min(hi, seg_hi)
        if b <= a:
            continue
        if kind == "neg":
            parts.append(jnp.full((H, b - a), NEG, F32))
        elif kind == "tab":
            parts.append(bd[:, ::-1][:, nd - b:nd - a] if descending else bd[:, a:b])
        else:
            parts.append(jnp.broadcast_to(bd[:, nd - 1:], (H, b - a)))
    return jnp.concatenate(parts[::-1] if descending else parts, axis=1)


def _toeplitz_body(vec_ref, o_ref, *, t, stride, transpose, mask_last):
    rows = o_ref.shape[1] if transpose else o_ref.shape[0]
    x = jnp.broadcast_to(vec_ref[...], (rows, vec_ref.shape[-1]))
    x = pltpu.roll(x, 0, 1, stride=stride, stride_axis=0)[:, :t]
    if mask_last:
        x = jnp.where(lax.broadcasted_iota(jnp.int32, x.shape, 0) < rows - 1, x, NEG)
    o_ref[...] = x.T if transpose else x


def _toeplitz(vec, rows, t, stride=1, transpose=False, mask_last=False):
    H, NV, _, L = vec.shape
    out = (t, rows) if transpose else (rows, t)
    body = functools.partial(_toeplitz_body, t=t, stride=stride, transpose=transpose, mask_last=mask_last)
    return pl.pallas_call(
        body, grid=(H, NV),
        in_specs=[pl.BlockSpec((None, None, 1, L), lambda h, v: (h, v, 0, 0))],
        out_specs=pl.BlockSpec((None, None) + out, lambda h, v: (h, v, 0, 0)),
        out_shape=jax.ShapeDtypeStruct((H, NV) + out, F32),
        compiler_params=_cparams("parallel", "parallel"),
    )(vec)


def _prompt_bias(bd, t, n_var, window=None):
    vec = jnp.stack([jnp.concatenate([_bias_range(bd, v * t - t, v * t + 1, window, descending=True),
                                      _bias_range(bd, v * t + 1, v * t + t, window, descending=True)], axis=1)
                     for v in range(n_var)], axis=1)
    return _toeplitz(vec[:, :, None, :], t, t)


def _decode_bias(bd, tq, q_rel, n_keys, tk, window=None):
    H = bd.shape[0]
    vec = _bias_range(bd, q_rel - n_keys + 1, q_rel + tq, window, descending=True)
    rows = jnp.stack([vec[:, tq - 1 - i:tq - 1 - i + n_keys] for i in range(tq)], axis=1)
    return rows.reshape(H, tq, n_keys // tk, tk).transpose(0, 2, 1, 3)


def _cmp_bias_prompt(bd, T, n):
    H = bd.shape[0]
    L = T + CMP_STRIDE * n
    vec = jnp.concatenate([_bias_range(bd, 1 - CMP_LEN, T + 1 - CMP_LEN), jnp.full((H, L - T), NEG, F32)], axis=1)
    return _toeplitz(vec[:, None, None, :], n, T, stride=CMP_STRIDE, transpose=True, mask_last=True)[:, 0]


def _cmp_bias(tab_t, q_pos, n):
    end = np.arange(n) * CMP_STRIDE + CMP_LEN - 1
    dist = q_pos[:, None] - end[None, :]
    valid = (dist >= 0) & (np.arange(n)[None, :] < n - 1)
    return _bias_from_dist(tab_t, dist, valid)


def _overlap_np(n, n_sel, n_sel_pad):
    cs = np.arange(n)[:, None] * CMP_STRIDE
    js = np.arange(n_sel_pad)[None, :] * SEL_BLOCK
    ov = (cs < js + SEL_BLOCK) & (cs + CMP_LEN > js) & (np.arange(n)[:, None] < n - 1) & (np.arange(n_sel_pad)[None, :] < n_sel)
    return ov.astype(np.float32)


def _round_up(x, m):
    return -(-x // m) * m


def _reorder_w_in(w, dims):
    h_a, hkv_a, h_b, hkv_b, h_c, hkv_c, h_d, hkv_d = dims
    widths = [h_a * DH] + [hkv_a * DH] * 6 + [h_a * 3, h_b * DH, hkv_b * DH, hkv_b * DH, h_b,
                                             h_c * 2 * DH, hkv_c * 2 * DH, hkv_c * 2 * DH, h_d * DH, hkv_d * DH, hkv_d * DH]
    offs = np.concatenate([[0], np.cumsum(widths)])
    seg = lambda i: w[:, offs[i]:offs[i + 1]]
    D = w.shape[0]
    g_c = h_c // hkv_c
    c_q = seg(12).reshape(D, hkv_c, g_c, 2, DH).transpose(0, 1, 3, 2, 4).reshape(D, -1)
    pad = lambda a: jnp.pad(a, ((0, 0), (0, LANE - a.shape[1])))
    return jnp.concatenate([seg(0), seg(8), seg(15), c_q, seg(1), seg(2), seg(3), seg(4), seg(5), seg(6),
                            seg(9), seg(10), seg(13), seg(14), seg(16), seg(17), pad(seg(7)), pad(seg(11))], axis=1)


def _layer(x, B, T, mod, layer, past, prm, tabs, dims, tiles):
    h_a, hkv_a, h_b, hkv_b, h_c, hkv_c, h_d, hkv_d = dims
    g_a, g_b, g_c, g_d = h_a // hkv_a, h_b // hkv_b, h_c // hkv_c, h_d // hkv_d
    D = x.shape[1]
    M = B * T
    sh1, sc1, gt1, sh2, sc2, gt2 = [m[:, None, :] for m in jnp.split(mod, 6, axis=-1)]
    tm_n = tiles["norm_tm"]
    h = _norm(x, prm["norm_attn"], layer, tm=tm_n, sc=sc1, sh=sh1, rows_per_batch=T, out_dtype=BF16)
    proj = _matmul(h, prm["w_in_r"], None, **tiles["w_in"]).reshape(B, T, U_TOTAL * LANE)
    new = {
        "nsa": proj[:, :, U_NSA * LANE:U_WIN * LANE], "fox": proj[:, :, U_FOX * LANE:U_DIFF * LANE],
        "diff": proj[:, :, U_DIFF * LANE:U_MOBA * LANE], "moba": proj[:, :, U_MOBA * LANE:U_AG * LANE],
    }
    kw_new = proj[:, :, U_WIN * LANE:U_FOX * LANE]
    tq, tk = tiles["tq"], tiles["tk"]

    if past is None:
        q_off, Tk = 0, T
        src = dict(
            cmp=_lane_src(proj, U_NSA, jstride=hkv_a),
            sel_k=_lane_src(proj, U_NSA + 2 * hkv_a), sel_v=_lane_src(proj, U_NSA + 3 * hkv_a),
            win_k=_lane_src(proj, U_WIN), win_v=_lane_src(proj, U_WIN + hkv_a),
            fox_k=_lane_src(proj, U_FOX), fox_v=_lane_src(proj, U_FOX + hkv_b),
            diff_k=[_lane_src(proj, U_DIFF + half, stride=2) for half in range(2)],
            diff_v=_lane_src(proj, U_DIFF + 2 * hkv_c),
            moba_k=_lane_src(proj, U_MOBA), moba_v=_lane_src(proj, U_MOBA + hkv_d))
        assert T >= tiles["wbuf"]
        new["win"] = kw_new[:, T - tiles["wbuf"]:]
        lf_new, cum = _fox_cum(proj[:, :, U_BF * LANE:], prm["fox_fbias"], None)
        mode, nsteps = "causal", T // tk
        band_steps = tabs["win"].shape[1]
        n_cmp_pad = T // CMP_STRIDE
        tk_pad = T
    else:
        q_off = past["len"]
        Tk = q_off + T
        page = past["page"]

        pieces = past["pieces"]

        def contiguous(name, n_lanes, slots):
            new_page = jnp.pad(new[name][:, :, :n_lanes], ((0, 0), (0, page - T), (0, 0)))
            return _gather_pages(past[name], layer, past["pt"], new_page, slots=slots,
                                 n_pieces=pieces[name]["n"])

        pc = pieces["nsa"]["of"]
        nsa_cmp = contiguous("nsa", 2 * hkv_a * DH, [pc(t, h) for t in range(2) for h in range(hkv_a)])
        pc = pieces["moba"]["of"]
        moba_means = _block_means_paged(past["moba"], pieces["moba"]["n"], layer, past["pt"],
                                        [pc(0, h) for h in range(hkv_d)])
        past_lf = _gather_pages(past["logf"], layer, past["pt"], jnp.zeros((B, page, h_b), F32))
        win_kv = jnp.concatenate([past["win"][layer], kw_new], axis=1)
        wbuf = past["win"].shape[2]
        new["win"] = win_kv[:, -wbuf:]
        src = dict(cmp=_lane_src(nsa_cmp, 0, jstride=hkv_a), moba_k=None,
                   win_k=_lane_src(win_kv, 0), win_v=_lane_src(win_kv, hkv_a))
        lf_new, cum = _fox_cum(proj[:, :, U_BF * LANE:], prm["fox_fbias"], past_lf, n_past=q_off)
        n_cmp_pad = q_off // CMP_STRIDE
        tk_pad = tabs["causal"].shape[1] * tabs["causal"].shape[3]

        def paged(name, q_block, groups, out_heads, bias, dv=DH, **kw):
            info = pieces[name]
            rows = new[name].reshape(B, T, info["n_t"], info["hkv"], info["halves"], LANE)
            rows = rows.transpose(0, 1, 2, 4, 3, 5).reshape(B, T * info["n"], LANE)
            new_page = jnp.pad(rows, ((0, 0), (0, (page - T) * info["n"]), (0, 0)))
            return _flash_paged(proj, q_block, groups, out_heads, past[name], info["n"], layer, past["pt"],
                                new_page, bias, dv=dv, **kw)
    new["logf"] = lf_new

    cmp_kv = _compress(src["cmp"], hkv_a, n_cmp_pad, layer, prm["cmp_pos"], prm["cmp_w1"], prm["cmp_w2"])
    n_sel = -(-Tk // SEL_BLOCK)
    o_cmp, sel_mask = _nsa_cmp(proj, U_QA, cmp_kv, tabs["cmp"], tabs["overlap"], hkv=hkv_a, G=g_a,
                               tq=min(tiles["cmp_tq"], T), q_off=q_off, n_sel=n_sel)
    sel_mask = sel_mask[:, :, None]
    cum_t = cum.transpose(0, 2, 1)
    cq = cum_t[:, :, Tk - T:, None]
    ck = jnp.pad(cum_t, ((0, 0), (0, 0), (0, tk_pad - Tk)))[:, :, None, :]
    nb = -(-Tk // MOBA_BLOCK)
    moba_mask = _moba_gate(proj, U_QD, src["moba_k"], hkv=hkv_d, G=g_d, Tk=Tk, q_off=q_off,
                           nbp=_round_up(nb, LANE), means=None if past is None else moba_means)
    if past is None:
        o_sel = _flash(proj, U_QA, src["sel_k"], src["sel_v"], tabs["a"], hkv=hkv_a, G=g_a,
                       tq=tq, tk=tk, dv=DH, mode=mode, nsteps=nsteps, mask=sel_mask, mask_blk=SEL_BLOCK)
        o_win = _flash(proj, U_QA, src["win_k"], src["win_v"], (tabs["win"], 0, False), hkv=hkv_a, G=g_a,
                       tq=tabs["win"].shape[2], tk=tabs["win"].shape[3], dv=DH, mode="band", nsteps=band_steps)
        o_b = _flash(proj, U_QB, src["fox_k"], src["fox_v"], tabs["mask"], hkv=hkv_b, G=g_b,
                     tq=tq, tk=tk, dv=DH, mode=mode, nsteps=nsteps, fox=(cq, ck))
        o_c = [_flash(proj, U_QC + half * g_c, src["diff_k"][half], src["diff_v"], tabs["c"],
                      hkv=hkv_c, G=g_c, tq=tq, tk=tk, dv=2 * DH, mode=mode, nsteps=nsteps, q_stride=2)
               for half in range(2)]
        o_d = _flash(proj, U_QD, src["moba_k"], src["moba_v"], tabs["d"], hkv=hkv_d, G=g_d,
                     tq=tq, tk=tk, dv=DH, mode=mode, nsteps=nsteps, mask=moba_mask, mask_blk=MOBA_BLOCK)
    else:
        o_win = _flash(proj, U_QA, src["win_k"], src["win_v"], (tabs["win"], 0, False), hkv=hkv_a, G=g_a,
                       tq=tq, tk=tabs["win"].shape[3], dv=DH, mode="full", nsteps=1)
        causal = tabs["causal"]
        pc = pieces["nsa"]["of"]
        o_sel, = paged("nsa", U_QA // 8, [
            dict(q_units=[h * g_a + g for g in range(g_a)], k_piece=pc(2, h), v_pieces=[pc(3, h)],
                 bias_head=h * g_a, mask_head=h, out=(0, h * g_a)) for h in range(hkv_a)],
            [h_a], causal[:h_a], mask=sel_mask, mask_blk=SEL_BLOCK)
        pc = pieces["fox"]["of"]
        o_b, = paged("fox", U_QB // 8, [
            dict(q_units=[h * g_b + g for g in range(g_b)], k_piece=pc(0, h), v_pieces=[pc(1, h)],
                 bias_head=None, fox_head=h * g_b, out=(0, h * g_b)) for h in range(hkv_b)],
            [h_b], causal[h_a + h_c + h_d:], fox=(cq, ck))
        pc = pieces["diff"]["of"]
        o_c = paged("diff", U_QC // 8, [
            dict(q_units=[(h * 2 + half) * g_c + g for g in range(g_c)], k_piece=pc(0, h, half),
                 v_pieces=[pc(1, h, 0), pc(1, h, 1)], bias_head=h * g_c, out=(half, h * g_c))
            for half in range(2) for h in range(hkv_c)],
            [h_c, h_c], causal[h_a:h_a + h_c], dv=2 * DH)
        pc = pieces["moba"]["of"]
        o_d, = paged("moba", U_QD // 8, [
            dict(q_units=[h * g_d + g for g in range(g_d)], k_piece=pc(0, h), v_pieces=[pc(1, h)],
                 bias_head=h * g_d, mask_head=h, out=(0, h * g_d)) for h in range(hkv_d)],
            [h_d], causal[h_a + h_c:h_a + h_c + h_d], mask=moba_mask, mask_blk=MOBA_BLOCK)

    lam_init = 0.8 - 0.6 * math.exp(-0.3 * layer)
    flat = lambda a: a.reshape(M, a.shape[-1])
    o = _combine(flat(o_cmp), flat(o_sel), flat(o_win), flat(proj), flat(o_b), flat(o_c[0]), flat(o_c[1]),
                 flat(o_d), prm["mix_gain"], prm["diff_lambda"], layer, tm=tiles["comb_tm"], lam_init=lam_init)

    if T % tiles["w_out"]["tm"] == 0:
        gates = dict(gate1=gt1, gate2=gt2, rows_per_batch=T)
    else:
        gates = dict(gate1=jnp.repeat(gt1[:, 0], T, axis=0), gate2=jnp.repeat(gt2[:, 0], T, axis=0), rows_per_batch=None)
    x = _matmul(o, prm["w_out"], layer, res=x, gate=gates["gate1"], rows_per_batch=gates["rows_per_batch"],
                **tiles["w_out"])
    h2 = _norm(x, prm["norm_ffn"], layer, tm=tm_n, sc=sc2, sh=sh2, rows_per_batch=T, out_dtype=BF16)
    act = _matmul(h2, prm["w_gate"], layer, w2=prm["w_up"], out_dtype=BF16, **tiles["w_ff"])
    x = _matmul(act, prm["w_down"], layer, res=x, gate=gates["gate2"], rows_per_batch=gates["rows_per_batch"],
                **tiles["w_down"])
    return x, new


def _group_tables(t5_table, dims, T, q_off, tiles, tk_pad, win_len):
    h_a, hkv_a, h_b, hkv_b, h_c, hkv_c, h_d, hkv_d = dims
    tab_t = t5_table.astype(F32).T
    tab_a = tab_t[:h_a]
    tq, tk = tiles["tq"], tiles["tk"]
    Tk = q_off + T
    n = (Tk - CMP_LEN) // CMP_STRIDE + 2
    n_sel = -(-Tk // SEL_BLOCK)
    tabs = {"overlap": jnp.asarray(_overlap_np(n, n_sel, _round_up(n_sel, LANE)))}
    bd = jnp.concatenate([_dist_table(tab_t), _dist_table(None)], axis=0)
    n_bias = tab_t.shape[0]
    if q_off == 0:
        assert tq == tk
        tw = tiles["win_t"]
        nband = (WINDOW - 1 + tw - 1) // tw + 1
        tabs.update(cmp=_cmp_bias_prompt(bd[:h_a], T, n), causal=_prompt_bias(bd, tq, 3),
                    win=_prompt_bias(bd[:h_a], tw, min(nband, T // tw), WINDOW))
    else:
        assert win_len % 8 == 0 and q_off >= win_len - T
        tabs.update(cmp=_cmp_bias(tab_a, q_off + np.arange(T), n), causal=_decode_bias(bd, T, q_off, tk_pad, tk),
                    win=_decode_bias(bd[:h_a], T, win_len - T, win_len, win_len, WINDOW))
    tabs.update(a=(tabs["causal"], 0, False), c=(tabs["causal"], h_a, False), d=(tabs["causal"], h_a + h_c, False),
                mask=(tabs["causal"], n_bias, True))
    return tabs


PROMPT_TILES = dict(
    tq=512, tk=512, win_t=512, cmp_tq=512, norm_tm=256, comb_tm=256,
    w_in=dict(tm=1024, tn=512, tk=4096), w_out=dict(tm=1024, tn=512, tk=4096),
    w_ff=dict(tm=1024, tn=256, tk=4096), w_down=dict(tm=2048, tn=512, tk=1024),
)


def _sample_tiles(M, T, tk):
    return dict(
        tq=T, tk=tk, cmp_tq=T, norm_tm=T, comb_tm=M,
        w_in=dict(tm=M, tn=512, tk=4096), w_out=dict(tm=M, tn=512, tk=4096),
        w_ff=dict(tm=M, tn=256, tk=4096), w_down=dict(tm=M, tn=1024, tk=1024),
    )


def kernel(x_prompt, x_sample, cache_nsa, state_nsa_win, cache_fox, cache_fox_logf, cache_diff, cache_moba,
           page_table, c_prompt, c_sample, t5_table, ada_w, ada_b, norm_attn, norm_ffn, w_in, w_out, mix_gain,
           nsa_cmp_pos, nsa_cmp_w1, nsa_cmp_w2, fox_fbias, diff_lambda, w_gate, w_up, w_down, final_norm):
    depth = w_in.shape[0]
    B, T, D = x_prompt.shape
    Bs, Ts, _ = x_sample.shape
    hkv_a, hkv_b, hkv_c, hkv_d = cache_nsa.shape[4], cache_fox.shape[4], cache_diff.shape[4], cache_moba.shape[4]
    h_b, h_c = fox_fbias.shape[1], diff_lambda.shape[1]
    h_a = (D // DH - h_b - 2 * h_c) // 2
    h_d = h_a
    dims = (h_a, hkv_a, h_b, hkv_b, h_c, hkv_c, h_d, hkv_d)
    page = cache_nsa.shape[2]
    n_pages = page_table.shape[1]
    past_len = n_pages * page
    pages_per_step = math.gcd(PAGED_PER_STEP, n_pages)
    tk_s = pages_per_step * page
    tk_pad = (n_pages // pages_per_step + 1) * tk_s
    win_len = state_nsa_win.shape[2] + Ts

    p_tiles = dict(PROMPT_TILES, wbuf=state_nsa_win.shape[2])
    s_tiles = _sample_tiles(Bs * Ts, Ts, tk_s)
    tabs_p = _group_tables(t5_table, dims, T, 0, p_tiles, T, 0)
    tabs_s = _group_tables(t5_table, dims, Ts, past_len, s_tiles, tk_pad, win_len)

    def piece_rows(c):
        d, n_phys, pg, n_t, hkv, w = c.shape
        halves = w // LANE
        c = c.reshape(d, n_phys, pg, n_t, hkv, halves, LANE).transpose(0, 1, 2, 3, 5, 4, 6)
        info = dict(n=n_t * hkv * halves, n_t=n_t, hkv=hkv, halves=halves,
                    of=lambda t, h, half=0: (t * halves + half) * hkv + h)
        return c.reshape(d, n_phys, pg * info["n"], LANE), info

    rows = {name: piece_rows(c) for name, c in
            (("nsa", cache_nsa), ("fox", cache_fox), ("diff", cache_diff), ("moba", cache_moba))}
    past = dict({name: r[0] for name, r in rows.items()}, pieces={name: r[1] for name, r in rows.items()},
                logf=cache_fox_logf,
                win=state_nsa_win.reshape(depth, Bs, state_nsa_win.shape[2], -1),
                pt=page_table, len=past_len, page=page)

    n_c = _round_up(B + Bs, 8)
    c_all = jnp.pad(jnp.concatenate([c_prompt, c_sample], axis=0), ((0, n_c - B - Bs), (0, 0)))
    xp, xs = x_prompt.reshape(B * T, D), x_sample.reshape(Bs * Ts, D)
    names = ("nsa", "win", "fox", "logf", "diff", "moba")
    st_p = {n: [] for n in names}
    st_s = {n: [] for n in names}
    r3 = lambda a: a.reshape(a.shape[0], 1, a.shape[1])
    for l in range(depth):
        mod = _matmul(c_all, ada_w, l, bias=ada_b[l][None], silu_in=True, tm=n_c, tn=512, tk=D)
        prm = dict(norm_attn=r3(norm_attn), norm_ffn=r3(norm_ffn), w_in_r=_reorder_w_in(w_in[l], dims), w_out=w_out,
                   mix_gain=r3(mix_gain), cmp_pos=nsa_cmp_pos, cmp_w1=nsa_cmp_w1, cmp_w2=nsa_cmp_w2,
                   fox_fbias=fox_fbias[l][None], diff_lambda=diff_lambda, w_gate=w_gate, w_up=w_up, w_down=w_down)
        xp, new_p = _layer(xp, B, T, mod[:B], l, None, prm, tabs_p, dims, p_tiles)
        xs, new_s = _layer(xs, Bs, Ts, mod[B:B + Bs], l, past, prm, tabs_s, dims, s_tiles)
        for n in names:
            st_p[n].append(new_p[n])
            st_s[n].append(new_s[n])
    y_p = _norm(xp, final_norm[None], None, tm=p_tiles["norm_tm"]).reshape(B, T, D)
    y_s = _norm(xs, final_norm[None], None, tm=s_tiles["norm_tm"]).reshape(Bs, Ts, D)

    def stack(st, name, tail):
        a = jnp.stack(st[name])
        return a.reshape(a.shape[:3] + tail)

    kv = lambda n, hk, w: (n, hk, w)
    shapes = {"nsa": kv(4, hkv_a, DH), "win": kv(2, hkv_a, DH), "fox": kv(2, hkv_b, DH), "logf": (h_b,),
              "diff": kv(2, hkv_c, 2 * DH), "moba": kv(2, hkv_d, DH)}
    out = [y_p, y_s]
    for n in names:
        out += [stack(st_p, n, shapes[n]), stack(st_s, n, shapes[n])]
    return tuple(out)
```

```python
import functools
import math

import jax
import jax.numpy as jnp
import numpy as np
from jax import lax
from jax.experimental import pallas as pl
from jax.experimental.pallas import tpu as pltpu

DH = 128
CMP_LEN = 32
CMP_STRIDE = 16
SEL_BLOCK = 64
SEL_TOP = 16
WINDOW = 512
MOBA_BLOCK = 256
MOBA_TOP = 3
N_BUCKETS = 32
MAX_DIST = 128
NEG = -1e30
FORCE = 1e4
RMS_EPS = 1e-6
BELOW_NEG = -3e38

LANE = 128
V7X_VMEM_LIMIT_BYTES = 60000 * 1024

U_QA, U_QB, U_QD, U_QC = 0, 8, 16, 24
U_NSA, U_WIN, U_FOX, U_DIFF, U_MOBA, U_AG, U_BF, U_TOTAL = 32, 40, 44, 48, 56, 60, 61, 62

BF16 = jnp.bfloat16
F32 = jnp.float32


def _cparams(*sem):
    return pltpu.CompilerParams(dimension_semantics=sem, vmem_limit_bytes=V7X_VMEM_LIMIT_BYTES)


def _silu(x):
    return x * jax.nn.sigmoid(x)


def _div_pow2(x, d):
    assert d & (d - 1) == 0
    return lax.shift_right_logical(x, jnp.int32(d.bit_length() - 1))


def _lane_src(arr, unit0, stride=1, jstride=0):
    def make_spec(rows, width, to_brhj):
        def index_map(*g):
            b, r, h, j = to_brhj(*g)
            return (b, r, (unit0 * DH) // width + h * stride + j * jstride)
        return pl.BlockSpec((None, rows, width), index_map)
    return arr, make_spec


def _mm_body(*refs, nk, tk, k_rem, silu_in, dual, has_bias, has_res):
    it = iter(refs)
    x_ref, w_ref = next(it), next(it)
    w2_ref = next(it) if dual else None
    b_ref = next(it) if has_bias else None
    r_ref, g_ref = (next(it), next(it)) if has_res else (None, None)
    o_ref, acc = next(it), next(it)
    acc2 = next(it) if dual else None
    k = pl.program_id(2)

    @pl.when(k == 0)
    def _():
        acc[...] = jnp.zeros_like(acc)
        if dual:
            acc2[...] = jnp.zeros_like(acc2)

    def accumulate(overhang):
        xv = x_ref[...]
        if silu_in:
            xv = _silu(xv.astype(F32))
        xv = xv.astype(BF16)
        wv = w_ref[...].astype(BF16)
        w2v = w2_ref[...].astype(BF16) if dual else None
        if overhang:
            xv = jnp.where(lax.broadcasted_iota(jnp.int32, xv.shape, 1) < k_rem, xv, jnp.zeros_like(xv))
            rows = lax.broadcasted_iota(jnp.int32, wv.shape, 0) < k_rem
            wv = jnp.where(rows, wv, jnp.zeros_like(wv))
            if dual:
                w2v = jnp.where(rows, w2v, jnp.zeros_like(w2v))
        acc[...] += jnp.dot(xv, wv, preferred_element_type=F32)
        if dual:
            acc2[...] += jnp.dot(xv, w2v, preferred_element_type=F32)

    if k_rem:
        pl.when(k < nk - 1)(lambda: accumulate(False))
        pl.when(k == nk - 1)(lambda: accumulate(True))
    else:
        accumulate(False)

    @pl.when(k == nk - 1)
    def _():
        r = acc[...]
        if dual:
            r = _silu(r) * acc2[...]
        if has_bias:
            r = r + b_ref[...]
        if has_res:
            r = r_ref[...] + g_ref[...] * r
        o_ref[...] = r.astype(o_ref.dtype)


def _matmul(x, w, layer, *, tm, tn, tk, w2=None, bias=None, res=None, gate=None, rows_per_batch=None,
            silu_in=False, out_dtype=F32):
    M, K = x.shape
    N = w.shape[-1]
    assert M % tm == 0
    nk = pl.cdiv(K, tk)
    k_rem = K % tk
    grid = (M // tm, pl.cdiv(N, tn), nk)
    if layer is None:
        w_spec = pl.BlockSpec((tk, tn), lambda m, n, k: (k, n))
    else:
        w_spec = pl.BlockSpec((None, tk, tn), lambda m, n, k: (layer, k, n))
    in_specs = [pl.BlockSpec((tm, tk), lambda m, n, k: (m, k)), w_spec]
    args = [x, w]
    if w2 is not None:
        in_specs.append(w_spec)
        args.append(w2)
    if bias is not None:
        in_specs.append(pl.BlockSpec((1, tn), lambda m, n, k: (0, n)))
        args.append(bias)
    if res is not None:
        in_specs.append(pl.BlockSpec((tm, tn), lambda m, n, k: (m, n)))
        args.append(res)
        if gate.ndim == 2:
            in_specs.append(pl.BlockSpec((tm, tn), lambda m, n, k: (m, n)))
        else:
            assert rows_per_batch % tm == 0
            per = rows_per_batch // tm
            in_specs.append(pl.BlockSpec((None, 1, tn), lambda m, n, k: (m // per, 0, n)))
        args.append(gate)
    scratch = [pltpu.VMEM((tm, tn), F32)] * (2 if w2 is not None else 1)
    body = functools.partial(_mm_body, nk=nk, tk=tk, k_rem=k_rem, silu_in=silu_in, dual=w2 is not None,
                             has_bias=bias is not None, has_res=res is not None)
    return pl.pallas_call(
        body, grid=grid, in_specs=in_specs,
        out_specs=pl.BlockSpec((tm, tn), lambda m, n, k: (m, n)),
        out_shape=jax.ShapeDtypeStruct((M, N), out_dtype),
        scratch_shapes=scratch,
        compiler_params=_cparams("parallel", "parallel", "arbitrary"),
    )(*args)


def _norm_body(*refs, modulated):
    if modulated:
        x_ref, g_ref, sc_ref, sh_ref, o_ref = refs
    else:
        x_ref, g_ref, o_ref = refs
    x = x_ref[...]
    y = x * lax.rsqrt(jnp.mean(x * x, axis=-1, keepdims=True) + RMS_EPS) * g_ref[...]
    if modulated:
        y = y * (1.0 + sc_ref[...]) + sh_ref[...]
    o_ref[...] = y.astype(o_ref.dtype)


def _norm(x, g, layer, *, tm, sc=None, sh=None, rows_per_batch=None, out_dtype=F32):
    M, D = x.shape
    modulated = sc is not None
    if layer is None:
        g_spec = pl.BlockSpec((1, D), lambda m: (0, 0))
    else:
        g_spec = pl.BlockSpec((None, 1, D), lambda m: (layer, 0, 0))
    in_specs = [pl.BlockSpec((tm, D), lambda m: (m, 0)), g_spec]
    args = [x, g]
    if modulated:
        assert rows_per_batch % tm == 0
        per = rows_per_batch // tm
        mod_spec = pl.BlockSpec((None, 1, D), lambda m: (m // per, 0, 0))
        in_specs += [mod_spec, mod_spec]
        args += [sc, sh]
    return pl.pallas_call(
        functools.partial(_norm_body, modulated=modulated), grid=(M // tm,), in_specs=in_specs,
        out_specs=pl.BlockSpec((tm, D), lambda m: (m, 0)),
        out_shape=jax.ShapeDtypeStruct((M, D), out_dtype),
        compiler_params=_cparams("parallel"),
    )(*args)


GATHER_PAGES_PER_STEP = 16


def _gather_body(pt_ref, *refs, n_groups, per, page, slots, n_pieces):
    cache_refs, new_ref, o_ref = refs[:per], refs[per], refs[per + 1]
    j = pl.program_id(1)

    @pl.when(j < n_groups)
    def _():
        for p in range(per):
            if slots is None:
                o_ref[p * page:(p + 1) * page] = cache_refs[p][...]
            else:
                for s, src_slot in enumerate(slots):
                    o_ref[p * page:(p + 1) * page, s * LANE:(s + 1) * LANE] = (
                        cache_refs[p][pl.ds(src_slot, page, stride=n_pieces), :])

    @pl.when(j == n_groups)
    def _():
        o_ref[0:page] = new_ref[...]
        if per > 1:
            o_ref[page:per * page] = jnp.zeros(((per - 1) * page,) + o_ref.shape[1:], o_ref.dtype)


def _gather_pages(cache, layer, page_table, new_page, slots=None, n_pieces=1):
    page, W = new_page.shape[1:]
    B, n_pages = page_table.shape
    per = math.gcd(GATHER_PAGES_PER_STEP, n_pages)
    n_groups = n_pages // per
    assert cache.shape[2:] == ((page, W) if slots is None else (page * n_pieces, LANE))
    assert slots is None or W == len(slots) * LANE

    def page_spec(p):
        return pl.BlockSpec((None, None) + cache.shape[2:],
                            lambda b, j, pt: (layer, pt[b, jnp.minimum(j, n_groups - 1) * per + p], 0, 0))

    grid_spec = pltpu.PrefetchScalarGridSpec(
        num_scalar_prefetch=1, grid=(B, n_groups + 1),
        in_specs=[page_spec(p) for p in range(per)] + [pl.BlockSpec((None, page, W), lambda b, j, pt: (b, 0, 0))],
        out_specs=pl.BlockSpec((None, per * page, W), lambda b, j, pt: (b, j, 0)),
    )
    return pl.pallas_call(
        functools.partial(_gather_body, n_groups=n_groups, per=per, page=page, slots=slots, n_pieces=n_pieces),
        grid_spec=grid_spec,
        out_shape=jax.ShapeDtypeStruct((B, (n_pages + 1) * page, W), cache.dtype),
        compiler_params=_cparams("parallel", "arbitrary"),
    )(page_table, *([cache] * per), new_page)


CUM_CHUNK = 256


def _cum_body(*refs, n_past, n_new, nh):
    if n_past:
        past_ref, raw_ref, fb_ref, lf_ref, cum_ref = refs
    else:
        raw_ref, fb_ref, lf_ref, cum_ref = refs
    z = raw_ref[...][:, :nh] + fb_ref[...]
    lf = -(jnp.maximum(-z, 0.0) + jnp.log1p(jnp.exp(-jnp.abs(z))))
    lf_ref[...] = lf

    def tri(n):
        return (lax.broadcasted_iota(jnp.int32, (n, n), 0) >= lax.broadcasted_iota(jnp.int32, (n, n), 1)).astype(F32)

    def scan_rows(src_ref, dst_off, n, carry):
        c = min(CUM_CHUNK, n)
        assert n % c == 0
        t = tri(c)
        for i in range(n // c):
            cum_ref[dst_off + i * c:dst_off + (i + 1) * c, :] = jnp.dot(
                t, src_ref[i * c:(i + 1) * c, :], preferred_element_type=F32, precision=lax.Precision.HIGHEST)
        for i in range(n // c):
            rows = slice(dst_off + i * c, dst_off + (i + 1) * c)
            total = cum_ref[dst_off + (i + 1) * c - 1:dst_off + (i + 1) * c, :]
            cum_ref[rows, :] = cum_ref[rows, :] + carry
            carry = carry + total
        return carry

    carry = jnp.zeros((1, nh), F32)
    if n_past:
        carry = scan_rows(past_ref, 0, n_past, carry)
    scan_rows(lf_ref, n_past, n_new, carry)


def _fox_cum(raw_f, fbias, past_lf, n_past=0):
    B, Tn, _ = raw_f.shape
    nh = fbias.shape[-1]
    in_specs, args = [], []
    if n_past:
        in_specs.append(pl.BlockSpec((None, n_past, nh), lambda b: (b, 0, 0)))
        args.append(past_lf)
    in_specs += [pl.BlockSpec((None, Tn, LANE), lambda b: (b, 0, 0)), pl.BlockSpec((1, nh), lambda b: (0, 0))]
    args += [raw_f, fbias]
    return pl.pallas_call(
        functools.partial(_cum_body, n_past=n_past, n_new=Tn, nh=nh), grid=(B,), in_specs=in_specs,
        out_specs=[pl.BlockSpec((None, Tn, nh), lambda b: (b, 0, 0)),
                   pl.BlockSpec((None, n_past + Tn, nh), lambda b: (b, 0, 0))],
        out_shape=[jax.ShapeDtypeStruct((B, Tn, nh), F32), jax.ShapeDtypeStruct((B, n_past + Tn, nh), F32)],
        compiler_params=_cparams("parallel"),
    )(*args)


def _gelu_tanh(x):
    return 0.5 * x * (1.0 + jnp.tanh(math.sqrt(2.0 / math.pi) * (x + 0.044715 * (x * x * x))))


def _compress_body(x_ref, pos_ref, w1_ref, w2_ref, o_ref, *, n):
    hidden = w1_ref.shape[-1]
    acc_lo = jnp.zeros((n, hidden), F32)
    acc_hi = jnp.zeros((n, hidden), F32)
    for rho in range(CMP_STRIDE):
        xr = x_ref[pl.ds(rho, n, stride=CMP_STRIDE), :]
        lo = (xr + pos_ref[rho:rho + 1, :]).astype(BF16)
        hi = (xr + pos_ref[rho + CMP_STRIDE:rho + CMP_STRIDE + 1, :]).astype(BF16)
        w_lo = w1_ref[rho * DH:(rho + 1) * DH, :].astype(BF16)
        w_hi = w1_ref[(rho + CMP_STRIDE) * DH:(rho + CMP_STRIDE + 1) * DH, :].astype(BF16)
        acc_lo += jnp.dot(lo, w_lo, preferred_element_type=F32)
        acc_hi += jnp.dot(hi, w_hi, preferred_element_type=F32)
    hid = acc_lo + pltpu.roll(acc_hi, n - 1, 0)
    o_ref[...] = jnp.dot(_gelu_tanh(hid).astype(BF16), w2_ref[...].astype(BF16), preferred_element_type=F32)


def _compress(src, hkv, n, layer, pos, w1, w2):
    kv, make_spec = src
    B = kv.shape[0]
    hidden = w1.shape[-1]
    return pl.pallas_call(
        functools.partial(_compress_body, n=n), grid=(B, 2, hkv),
        in_specs=[
            make_spec(n * CMP_STRIDE, DH, lambda b, j, h: (b, 0, h, j)),
            pl.BlockSpec((None, None, CMP_LEN, DH), lambda b, j, h: (layer, j, 0, 0)),
            pl.BlockSpec((None, None, CMP_LEN * DH, hidden), lambda b, j, h: (layer, j, 0, 0)),
            pl.BlockSpec((None, None, hidden, DH), lambda b, j, h: (layer, j, 0, 0)),
        ],
        out_specs=pl.BlockSpec((None, None, None, n, DH), lambda b, j, h: (b, j, h, 0, 0)),
        out_shape=jax.ShapeDtypeStruct((B, 2, hkv, n, DH), F32),
        compiler_params=_cparams("parallel", "arbitrary", "arbitrary"),
    )(kv, pos, w1, w2)


def _take_top(score, lane_ids, count):
    sel = jnp.zeros(score.shape, F32)
    taken = []
    big = jnp.int32(score.shape[-1])
    for _ in range(count):
        m = jnp.max(score, axis=-1, keepdims=True)
        idx = jnp.min(jnp.where(score == m, lane_ids, big), axis=-1, keepdims=True)
        hit = lane_ids == idx
        sel = jnp.where(hit, 1.0, sel)
        score = jnp.where(hit, BELOW_NEG, score)
        taken.append((hit, m))
    return sel, taken


def _cmp_body(q_ref, kc_ref, vc_ref, bias_ref, ov_ref, o_ref, sel_ref, *, G, tq, q_off, n_sel, top):
    qi = pl.program_id(2)
    kc = kc_ref[...]
    vc = vc_ref[...].astype(BF16)
    scale = DH ** -0.5
    psum = None
    outs = []
    for g in range(G):
        qg = q_ref[:, g * DH:(g + 1) * DH]
        s = lax.dot_general(qg, kc, (((1,), (1,)), ((), ())), preferred_element_type=F32,
                            precision=lax.Precision.HIGHEST) * scale + bias_ref[g]
        m = jnp.max(s, axis=-1, keepdims=True)
        e = jnp.where(s > NEG / 2, jnp.exp(s - m), 0.0)
        p = e / jnp.maximum(jnp.sum(e, axis=-1, keepdims=True), 1e-30)
        outs.append(jnp.dot(p.astype(BF16), vc, preferred_element_type=F32))
        psum = p if psum is None else psum + p
    o_ref[...] = jnp.concatenate(outs, axis=1)
    imp = jnp.dot(psum, ov_ref[...], preferred_element_type=F32, precision=lax.Precision.HIGHEST)
    shape = imp.shape
    jb = lax.broadcasted_iota(jnp.int32, shape, 1)
    qpos = q_off + qi * tq + lax.broadcasted_iota(jnp.int32, shape, 0)
    cur = _div_pow2(qpos, SEL_BLOCK)
    forced = (jb == 0) | (jb == cur) | (jb == cur - 1)
    score = jnp.where(jb <= cur, imp + jnp.where(forced, FORCE, 0.0), NEG)
    score = jnp.where(jb < n_sel, score, BELOW_NEG)
    sel, _ = _take_top(score, jb, top)
    sel_ref[...] = jnp.where((jb <= cur) & (sel > 0.5), 0.0, NEG)


def _nsa_cmp(q, u_q, cmp_kv, bias, overlap, *, hkv, G, tq, q_off, n_sel):
    B, Tq, _ = q.shape
    n = cmp_kv.shape[3]
    nsp = overlap.shape[1]
    top = min(SEL_TOP, n_sel)
    body = functools.partial(_cmp_body, G=G, tq=tq, q_off=q_off, n_sel=n_sel, top=top)
    return pl.pallas_call(
        body, grid=(B, hkv, Tq // tq),
        in_specs=[
            pl.BlockSpec((None, tq, G * DH), lambda b, h, i: (b, i, u_q // G + h)),
            pl.BlockSpec((None, None, None, n, DH), lambda b, h, i: (b, 0, h, 0, 0)),
            pl.BlockSpec((None, None, None, n, DH), lambda b, h, i: (b, 1, h, 0, 0)),
            pl.BlockSpec((G, tq, n), lambda b, h, i: (h, i, 0)),
            pl.BlockSpec((n, nsp), lambda b, h, i: (0, 0)),
        ],
        out_specs=[pl.BlockSpec((None, tq, G * DH), lambda b, h, i: (b, i, h)),
                   pl.BlockSpec((None, None, tq, nsp), lambda b, h, i: (b, h, i, 0))],
        out_shape=[jax.ShapeDtypeStruct((B, Tq, hkv * G * DH), F32),
                   jax.ShapeDtypeStruct((B, hkv, Tq, nsp), F32)],
        compiler_params=_cparams("parallel", "parallel", "arbitrary"),
    )(q, cmp_kv, cmp_kv, bias, overlap)


def _block_means_body(pt_ref, *refs, per, page, n_pieces, k_pieces):
    page_refs, o_ref = refs[:per], refs[per]
    for h, piece in enumerate(k_pieces):
        k = jnp.concatenate([ref[pl.ds(piece, page, stride=n_pieces), :] for ref in page_refs], axis=0)
        o_ref[h] = jnp.sum(k.reshape(per * page // MOBA_BLOCK, MOBA_BLOCK, DH), axis=1) * (1.0 / MOBA_BLOCK)


def _block_means_paged(cache, n_pieces, layer, page_table, k_pieces):
    B, n_pages = page_table.shape
    page = cache.shape[2] // n_pieces
    per = math.gcd(MEANS_PAGES_PER_STEP, n_pages)
    blocks = per * page // MOBA_BLOCK
    assert (per * page) % MOBA_BLOCK == 0 and blocks % 8 == 0

    def page_spec(p):
        return pl.BlockSpec((None, None, page * n_pieces, LANE), lambda b, j, pt: (layer, pt[b, j * per + p], 0, 0))

    grid_spec = pltpu.PrefetchScalarGridSpec(
        num_scalar_prefetch=1, grid=(B, n_pages // per), in_specs=[page_spec(p) for p in range(per)],
        out_specs=pl.BlockSpec((None, len(k_pieces), blocks, DH), lambda b, j, pt: (b, 0, j, 0)))
    body = functools.partial(_block_means_body, per=per, page=page, n_pieces=n_pieces, k_pieces=k_pieces)
    return pl.pallas_call(
        body, grid_spec=grid_spec,
        out_shape=jax.ShapeDtypeStruct((B, len(k_pieces), n_pages * page // MOBA_BLOCK, DH), F32),
        compiler_params=_cparams("parallel", "arbitrary"),
    )(page_table, *([cache] * per))


MEANS_PAGES_PER_STEP = 16


def _moba_gate_body(q_ref, k_ref, m_ref, *, G, n_full, nbp, q_off, top, is_means):
    Tq = q_ref.shape[0]
    shape = (Tq, nbp)
    jb = lax.broadcasted_iota(jnp.int32, shape, 1)
    own = _div_pow2(q_off + lax.broadcasted_iota(jnp.int32, shape, 0), MOBA_BLOCK)
    if n_full > 0:
        if is_means:
            kmean = k_ref[...]
        else:
            kmean = jnp.sum(k_ref[...].reshape(n_full, MOBA_BLOCK, DH), axis=1) * (1.0 / MOBA_BLOCK)
        if nbp > n_full:
            kmean = jnp.concatenate([kmean, jnp.zeros((nbp - n_full, DH), F32)], axis=0)
    for g in range(G):
        mask = jnp.where(jb == own, 0.0, NEG)
        if n_full > 0:
            gate = lax.dot_general(q_ref[:, g * DH:(g + 1) * DH], kmean, (((1,), (1,)), ((), ())),
                                   preferred_element_type=F32, precision=lax.Precision.HIGHEST)
            score = jnp.where((jb < own) & (jb < n_full), gate, NEG)
            score = jnp.where(jb < n_full, score, BELOW_NEG)
            _, taken = _take_top(score, jb, top)
            for hit, val in taken:
                mask = jnp.where(hit & (val > NEG / 2), 0.0, mask)
        m_ref[g] = mask


def _moba_gate(q, u_q, ksrc, *, hkv, G, Tk, q_off, nbp, means=None):
    B, Tq, _ = q.shape
    n_full = Tk // MOBA_BLOCK
    top = min(MOBA_TOP, n_full)
    if means is None:
        k, k_spec = ksrc
        k_in = k_spec(max(n_full, 1) * MOBA_BLOCK, DH, lambda b, h: (b, 0, h, 0))
    else:
        assert means.shape[2] == n_full
        k, k_in = means, pl.BlockSpec((None, None, n_full, DH), lambda b, h: (b, h, 0, 0))
    body = functools.partial(_moba_gate_body, G=G, n_full=n_full, nbp=nbp, q_off=q_off, top=top,
                             is_means=means is not None)
    return pl.pallas_call(
        body, grid=(B, hkv),
        in_specs=[pl.BlockSpec((None, Tq, G * DH), lambda b, h: (b, 0, u_q // G + h)), k_in],
        out_specs=pl.BlockSpec((None, None, G, Tq, nbp), lambda b, h: (b, h, 0, 0, 0)),
        out_shape=jax.ShapeDtypeStruct((B, hkv, G, Tq, nbp), F32),
        compiler_params=_cparams("parallel", "arbitrary"),
    )(q, k)


PAIR_Q, PAIR_K, PAIR_VARIANT, PAIR_FLAGS = 0, 1, 2, 3
PAIR_FIRST, PAIR_LAST = 1, 2


def _flash_body(pairs_ref, *refs, G, Gm, tq, tk, dv, fox, mask_blk):
    it = iter(refs)
    q_ref, k_ref, v_ref, bias_ref = next(it), next(it), next(it), next(it)
    cq_ref, ck_ref = (next(it), next(it)) if fox else (None, None)
    mask_ref = next(it) if mask_blk else None
    o_ref, m_sc, acc_sc = next(it), next(it), next(it)
    pair = pl.program_id(2)
    ki = pairs_ref[PAIR_K, pair]
    flags = pairs_ref[PAIR_FLAGS, pair]

    @pl.when(flags % 2 == PAIR_FIRST)
    def _():
        m_sc[...] = jnp.full(m_sc.shape, NEG, F32)
        acc_sc[...] = jnp.zeros_like(acc_sc)

    def update():
        k = k_ref[...].astype(BF16)
        v = jnp.concatenate([v_ref[...].astype(BF16), jnp.ones((tk, LANE), BF16)], axis=1)
        q = jnp.concatenate([q_ref[:, g * DH:(g + 1) * DH] for g in range(G)], axis=0)
        q = (q * (DH ** -0.5)).astype(BF16)
        s = lax.dot_general(q, k, (((1,), (1,)), ((), ())), preferred_element_type=F32)
        s = s.reshape(G, tq, tk) + bias_ref[...]
        if fox:
            s = s + (cq_ref[...] - ck_ref[...])
        if mask_blk:
            nbp = mask_ref.shape[-1]
            kpos = ki * tk + lax.broadcasted_iota(jnp.int32, (nbp, tk), 1)
            lo = lax.broadcasted_iota(jnp.int32, (nbp, tk), 0) * mask_blk
            expand = jnp.where((kpos >= lo) & (kpos < lo + mask_blk), 1.0, 0.0).astype(BF16)
            picked = jnp.dot(mask_ref[...].reshape(Gm * tq, nbp).astype(BF16), expand, preferred_element_type=F32)
            s = s + picked.reshape(Gm, tq, tk)
        m_old = m_sc[...]
        m_new = jnp.maximum(m_old, jnp.max(s, axis=-1, keepdims=True))
        m_use = jnp.where(m_new < NEG / 2, 0.0, m_new)
        p = jnp.exp((s - m_use).astype(BF16))
        alpha = jnp.exp(m_old - m_use)
        pv = jnp.dot(p.reshape(G * tq, tk), v, preferred_element_type=F32)
        acc_sc[...] = alpha * acc_sc[...] + pv.reshape(G, tq, dv + LANE)
        m_sc[...] = m_new

    update()

    @pl.when(flags >= PAIR_LAST)
    def _():
        acc = acc_sc[...]
        o = acc[:, :, :dv] / jnp.maximum(acc[:, :, dv:dv + 1], 1e-30)
        for g in range(G):
            o_ref[:, g * dv:(g + 1) * dv] = o[g]


def _flash(q, u_q, ksrc, vsrc, bias, *, hkv, G, tq, tk, dv, mode, nsteps, fox=None, mask=None, mask_blk=0,
           q_stride=1):
    B, Tq, _ = q.shape
    nq = Tq // tq
    bias, bias_h0, bias_shared = bias
    NB = bias.shape[1]
    Gb = 1 if bias_shared else G
    assert bias_h0 % Gb == 0

    if mode == "causal":
        pairs = [(i, s, min(i - s, NB - 1)) for i in range(nq) for s in range(min(i + 1, nsteps))]
    elif mode == "band":
        pairs = [(i, i - d, d) for i in range(nq) for d in range(min(nsteps, NB) - 1, -1, -1) if i - d >= 0]
    else:
        pairs = [(i, s, s) for i in range(nq) for s in range(nsteps)]
    table = np.zeros((4, len(pairs)), np.int32)
    for n, (i, s, var) in enumerate(pairs):
        first = n == 0 or pairs[n - 1][0] != i
        last = n == len(pairs) - 1 or pairs[n + 1][0] != i
        table[:, n] = (i, s, var, PAIR_FIRST * first + PAIR_LAST * last)
    qt = lambda t, p: t[PAIR_Q, p]
    kt = lambda t, p: t[PAIR_K, p]

    kv_index = lambda b, h, p, t: (b, kt(t, p), h, 0)
    in_specs = [
        pl.BlockSpec((None, tq, G * DH), lambda b, h, p, t: (b, qt(t, p), u_q // G + h * q_stride)),
        ksrc[1](tk, DH, kv_index),
        vsrc[1](tk, dv, kv_index),
        pl.BlockSpec((Gb, None, tq, tk),
                     lambda b, h, p, t: (bias_h0 // Gb + (0 if bias_shared else h), t[PAIR_VARIANT, p], 0, 0)),
    ]
    args = [q, ksrc[0], vsrc[0], bias]
    if fox is not None:
        in_specs += [pl.BlockSpec((None, G, tq, 1), lambda b, h, p, t: (b, h, qt(t, p), 0)),
                     pl.BlockSpec((None, G, 1, tk), lambda b, h, p, t: (b, h, 0, kt(t, p)))]
        args += list(fox)
    Gm = 0
    if mask is not None:
        Gm, nbp = mask.shape[2], mask.shape[4]
        in_specs.append(pl.BlockSpec((None, None, Gm, tq, nbp), lambda b, h, p, t: (b, h, 0, qt(t, p), 0)))
        args.append(mask)
    body = functools.partial(_flash_body, G=G, Gm=Gm, tq=tq, tk=tk, dv=dv,
                             fox=fox is not None, mask_blk=mask_blk if mask is not None else 0)
    grid_spec = pltpu.PrefetchScalarGridSpec(
        num_scalar_prefetch=1, grid=(B, hkv, len(pairs)), in_specs=in_specs,
        out_specs=pl.BlockSpec((None, tq, G * dv), lambda b, h, p, t: (b, qt(t, p), h)),
        scratch_shapes=[pltpu.VMEM((G, tq, 1), F32), pltpu.VMEM((G, tq, dv + LANE), F32)])
    return pl.pallas_call(
        body, grid_spec=grid_spec,
        out_shape=jax.ShapeDtypeStruct((B, Tq, hkv * G * dv), F32),
        compiler_params=_cparams("parallel", "parallel", "arbitrary"),
    )(jnp.asarray(table), *args)


PAGED_PER_STEP = 32


def _flash_paged_body(pt_ref, *refs, groups, n_out, per, page, n_pieces, n_groups, dv, fox, has_mask, mask_blk):
    it = iter(refs)
    q_ref = next(it)
    page_refs = [next(it) for _ in range(per)]
    new_ref, bias_ref = next(it), next(it)
    cq_ref, ck_ref = (next(it), next(it)) if fox else (None, None)
    mask_ref = next(it) if has_mask else None
    o_refs = [next(it) for _ in range(n_out)]
    m_sc, acc_sc = next(it), next(it)
    j = pl.program_id(1)
    last = j == n_groups
    tk = per * page
    tq = q_ref.shape[0]

    @pl.when(j == 0)
    def _():
        m_sc[...] = jnp.full(m_sc.shape, NEG, F32)
        acc_sc[...] = jnp.zeros_like(acc_sc)

    def attend(sources):
        nk = len(sources) * page
        loaded = {}

        def piece(slot):
            if slot not in loaded:
                parts = [ref[pl.ds(slot, page, stride=n_pieces), :] for ref in sources]
                loaded[slot] = jnp.concatenate(parts, axis=0).astype(BF16)
            return loaded[slot]

        if has_mask:
            nbp = mask_ref.shape[-1]
            kpos = j * tk + lax.broadcasted_iota(jnp.int32, (nbp, nk), 1)
            lo = lax.broadcasted_iota(jnp.int32, (nbp, nk), 0) * mask_blk
            expand = jnp.where((kpos >= lo) & (kpos < lo + mask_blk), 1.0, 0.0).astype(BF16)
        ones = jnp.ones((nk, LANE), BF16)
        row = 0
        for grp in groups:
            G = len(grp["q_units"])
            k = piece(grp["k_piece"])
            v = jnp.concatenate([piece(s) for s in grp["v_pieces"]] + [ones], axis=1)
            q = jnp.concatenate([q_ref[:, u * DH:(u + 1) * DH] for u in grp["q_units"]], axis=0)
            q = (q * (DH ** -0.5)).astype(BF16)
            s = lax.dot_general(q, k, (((1,), (1,)), ((), ())), preferred_element_type=F32).reshape(G, tq, nk)
            b0 = grp["bias_head"]
            s = s + (bias_ref[:, :, :nk] if b0 is None else bias_ref[b0:b0 + G, :, :nk])
            if fox:
                f0 = grp["fox_head"]
                s = s + (cq_ref[f0:f0 + G] - ck_ref[f0:f0 + G, :, :nk])
            if has_mask:
                msk = mask_ref[grp["mask_head"]]
                gm = msk.shape[0]
                picked = jnp.dot(msk.reshape(gm * tq, nbp).astype(BF16), expand, preferred_element_type=F32)
                s = s + picked.reshape(gm, tq, nk)
            rows = slice(row, row + G)
            m_old = m_sc[rows]
            m_new = jnp.maximum(m_old, jnp.max(s, axis=-1, keepdims=True))
            m_use = jnp.where(m_new < NEG / 2, 0.0, m_new)
            p = jnp.exp((s - m_use).astype(BF16))
            alpha = jnp.exp(m_old - m_use)
            pv = jnp.dot(p.reshape(G * tq, nk), v, preferred_element_type=F32)
            acc_sc[rows] = alpha * acc_sc[rows] + pv.reshape(G, tq, dv + LANE)
            m_sc[rows] = m_new
            row += G

    @pl.when(j < n_groups)
    def _():
        attend(page_refs)

    @pl.when(last)
    def _():
        attend([new_ref])
        row = 0
        for grp in groups:
            G = len(grp["q_units"])
            acc = acc_sc[row:row + G]
            o = acc[:, :, :dv] / jnp.maximum(acc[:, :, dv:dv + 1], 1e-30)
            out_i, unit0 = grp["out"]
            for g in range(G):
                o_refs[out_i][:, (unit0 + g) * dv:(unit0 + g + 1) * dv] = o[g]
            row += G


def _flash_paged(q, q_block, groups, out_heads, cache, n_pieces, layer, page_table, new_page, bias, *, dv,
                 fox=None, mask=None, mask_blk=0):
    B, Tq, _ = q.shape
    n_pages = page_table.shape[1]
    page = cache.shape[2] // n_pieces
    per = math.gcd(PAGED_PER_STEP, n_pages)
    n_groups = n_pages // per
    tk = per * page
    assert bias.shape[1:] == (n_groups + 1, Tq, tk)
    n_rows = sum(len(g["q_units"]) for g in groups)
    q_w = 8 * DH

    def page_spec(p):
        return pl.BlockSpec((None, None, page * n_pieces, LANE),
                            lambda b, j, pt: (layer, pt[b, jnp.minimum(j, n_groups - 1) * per + p], 0, 0))

    in_specs = [pl.BlockSpec((None, Tq, q_w), lambda b, j, pt: (b, 0, q_block))]
    in_specs += [page_spec(p) for p in range(per)]
    in_specs += [pl.BlockSpec((None, page * n_pieces, LANE), lambda b, j, pt: (b, 0, 0)),
                 pl.BlockSpec((bias.shape[0], None, Tq, tk), lambda b, j, pt: (0, j, 0, 0))]
    args = [q] + [cache] * per + [new_page, bias]
    if fox is not None:
        H = fox[0].shape[1]
        in_specs += [pl.BlockSpec((None, H, Tq, 1), lambda b, j, pt: (b, 0, 0, 0)),
                     pl.BlockSpec((None, H, 1, tk), lambda b, j, pt: (b, 0, 0, j))]
        args += list(fox)
    if mask is not None:
        in_specs.append(pl.BlockSpec((None,) + mask.shape[1:], lambda b, j, pt: (b, 0, 0, 0, 0)))
        args.append(mask)
    body = functools.partial(_flash_paged_body, groups=groups, n_out=len(out_heads), per=per, page=page,
                             n_pieces=n_pieces, n_groups=n_groups, dv=dv, fox=fox is not None,
                             has_mask=mask is not None, mask_blk=mask_blk)
    grid_spec = pltpu.PrefetchScalarGridSpec(
        num_scalar_prefetch=1, grid=(B, n_groups + 1), in_specs=in_specs,
        out_specs=[pl.BlockSpec((None, Tq, nh * dv), lambda b, j, pt: (b, 0, 0)) for nh in out_heads],
        scratch_shapes=[pltpu.VMEM((n_rows, Tq, 1), F32), pltpu.VMEM((n_rows, Tq, dv + LANE), F32)],
    )
    return pl.pallas_call(
        body, grid_spec=grid_spec,
        out_shape=[jax.ShapeDtypeStruct((B, Tq, nh * dv), F32) for nh in out_heads],
        compiler_params=_cparams("parallel", "arbitrary"),
    )(page_table, *args)


def _head_rms(x, gain):
    return x * lax.rsqrt(jnp.mean(x * x, axis=-1, keepdims=True) + RMS_EPS) * gain


def _combine_body(cmp_ref, sel_ref, win_ref, ag_ref, b_ref, c1_ref, c2_ref, d_ref, mg_ref, lp_ref, o_ref, *,
                  h_a, h_b, h_c, h_d, lam_init):
    gates = jax.nn.sigmoid(ag_ref[...])
    col = 0
    for h in range(h_a):
        sl = slice(h * DH, (h + 1) * DH)
        o = (gates[:, 3 * h:3 * h + 1] * cmp_ref[:, sl] + gates[:, 3 * h + 1:3 * h + 2] * sel_ref[:, sl]
             + gates[:, 3 * h + 2:3 * h + 3] * win_ref[:, sl])
        o_ref[:, col:col + DH] = _head_rms(o, mg_ref[:, col:col + DH]).astype(o_ref.dtype)
        col += DH
    for h in range(h_b):
        o_ref[:, col:col + DH] = _head_rms(b_ref[:, h * DH:(h + 1) * DH], mg_ref[:, col:col + DH]).astype(o_ref.dtype)
        col += DH
    for h in range(h_c):
        lp = lp_ref[h]
        lam = (jnp.exp(jnp.sum(lp[0:1] * lp[1:2], axis=-1, keepdims=True))
               - jnp.exp(jnp.sum(lp[2:3] * lp[3:4], axis=-1, keepdims=True)) + lam_init)
        sl = slice(h * 2 * DH, (h + 1) * 2 * DH)
        o = c1_ref[:, sl] - lam * c2_ref[:, sl]
        o_ref[:, col:col + 2 * DH] = (_head_rms(o, mg_ref[:, col:col + 2 * DH]) * (1.0 - lam_init)).astype(o_ref.dtype)
        col += 2 * DH
    for h in range(h_d):
        o_ref[:, col:col + DH] = _head_rms(d_ref[:, h * DH:(h + 1) * DH], mg_ref[:, col:col + DH]).astype(o_ref.dtype)
        col += DH


def _combine(o_cmp, o_sel, o_win, proj, o_b, o_c1, o_c2, o_d, mix_gain, diff_lambda, layer, *, tm, lam_init):
    M = o_cmp.shape[0]
    D = mix_gain.shape[-1]
    h_a, h_b, h_d = o_cmp.shape[1] // DH, o_b.shape[1] // DH, o_d.shape[1] // DH
    h_c = o_c1.shape[1] // (2 * DH)
    row = lambda w: pl.BlockSpec((tm, w), lambda m: (m, 0))
    body = functools.partial(_combine_body, h_a=h_a, h_b=h_b, h_c=h_c, h_d=h_d, lam_init=lam_init)
    return pl.pallas_call(
        body, grid=(M // tm,),
        in_specs=[row(o_cmp.shape[1]), row(o_sel.shape[1]), row(o_win.shape[1]),
                  pl.BlockSpec((tm, LANE), lambda m: (m, U_AG)),
                  row(o_b.shape[1]), row(o_c1.shape[1]), row(o_c2.shape[1]), row(o_d.shape[1]),
                  pl.BlockSpec((None, 1, D), lambda m: (layer, 0, 0)),
                  pl.BlockSpec((None,) + diff_lambda.shape[1:], lambda m: (layer, 0, 0, 0))],
        out_specs=pl.BlockSpec((tm, D), lambda m: (m, 0)),
        out_shape=jax.ShapeDtypeStruct((M, D), BF16),
        compiler_params=_cparams("parallel"),
    )(o_cmp, o_sel, o_win, proj, o_b, o_c1, o_c2, o_d, mix_gain, diff_lambda)


def _t5_bucket_np(dist):
    n = np.maximum(dist, 0)
    exact = N_BUCKETS // 2
    nf = np.maximum(n, 1).astype(np.float32)
    big = exact + (np.log(nf / np.float32(exact)) / np.float32(math.log(MAX_DIST / exact))
                   * np.float32(N_BUCKETS - exact)).astype(np.int32)
    return np.where(n < exact, n, np.minimum(big, N_BUCKETS - 1)).astype(np.int32)


def _bias_from_dist(tab_t, dist, valid):
    if tab_t is None:
        return jnp.asarray(np.where(valid, 0.0, NEG).astype(np.float32))[None]
    idx = _t5_bucket_np(dist).reshape(-1)
    b = jnp.take(tab_t, jnp.asarray(idx), axis=1).reshape((tab_t.shape[0],) + dist.shape)
    return jnp.where(jnp.asarray(valid)[None], b, NEG)


def _dist_table(tab_t):
    if tab_t is None:
        return jnp.zeros((1, MAX_DIST + 1), F32)
    return jnp.take(tab_t, jnp.asarray(_t5_bucket_np(np.arange(MAX_DIST + 1))), axis=1)


def _bias_range(bd, lo, hi, window=None, descending=False):
    H, nd = bd.shape
    far = 10 ** 9
    w = far if window is None else window
    assert w >= nd - 1
    parts = []
    for seg_lo, seg_hi, kind in ((-far, 0, "neg"), (0, nd - 1, "tab"), (nd - 1, w, "far"), (w, far, "neg")):
        a, b = max(lo, seg_lo), min(hi, seg_hi)
        if b <= a:
            continue
        if kind == "neg":
            parts.append(jnp.full((H, b - a), NEG, F32))
        elif kind == "tab":
            parts.append(bd[:, ::-1][:, nd - b:nd - a] if descending else bd[:, a:b])
        else:
            parts.append(jnp.broadcast_to(bd[:, nd - 1:], (H, b - a)))
    return jnp.concatenate(parts[::-1] if descending else parts, axis=1)


def _toeplitz_body(vec_ref, o_ref, *, t, stride, transpose, mask_last):
    rows = o_ref.shape[1] if transpose else o_ref.shape[0]
    x = jnp.broadcast_to(vec_ref[...], (rows, vec_ref.shape[-1]))
    x = pltpu.roll(x, 0, 1, stride=stride, stride_axis=0)[:, :t]
    if mask_last:
        x = jnp.where(lax.broadcasted_iota(jnp.int32, x.shape, 0) < rows - 1, x, NEG)
    o_ref[...] = x.T if transpose else x


def _toeplitz(vec, rows, t, stride=1, transpose=False, mask_last=False):
    H, NV, _, L = vec.shape
    out = (t, rows) if transpose else (rows, t)
    body = functools.partial(_toeplitz_body, t=t, stride=stride, transpose=transpose, mask_last=mask_last)
    return pl.pallas_call(
        body, grid=(H, NV),
        in_specs=[pl.BlockSpec((None, None, 1, L), lambda h, v: (h, v, 0, 0))],
        out_specs=pl.BlockSpec((None, None) + out, lambda h, v: (h, v, 0, 0)),
        out_shape=jax.ShapeDtypeStruct((H, NV) + out, F32),
        compiler_params=_cparams("parallel", "parallel"),
    )(vec)


def _prompt_bias(bd, t, n_var, window=None):
    vec = jnp.stack([jnp.concatenate([_bias_range(bd, v * t - t, v * t + 1, window, descending=True),
                                      _bias_range(bd, v * t + 1, v * t + t, window, descending=True)], axis=1)
                     for v in range(n_var)], axis=1)
    return _toeplitz(vec[:, :, None, :], t, t)


def _decode_bias(bd, tq, q_rel, n_keys, tk, window=None):
    H = bd.shape[0]
    vec = _bias_range(bd, q_rel - n_keys + 1, q_rel + tq, window, descending=True)
    rows = jnp.stack([vec[:, tq - 1 - i:tq - 1 - i + n_keys] for i in range(tq)], axis=1)
    return rows.reshape(H, tq, n_keys // tk, tk).transpose(0, 2, 1, 3)


def _cmp_bias_prompt(bd, T, n):
    H = bd.shape[0]
    L = T + CMP_STRIDE * n
    vec = jnp.concatenate([_bias_range(bd, 1 - CMP_LEN, T + 1 - CMP_LEN), jnp.full((H, L - T), NEG, F32)], axis=1)
    return _toeplitz(vec[:, None, None, :], n, T, stride=CMP_STRIDE, transpose=True, mask_last=True)[:, 0]


def _cmp_bias(tab_t, q_pos, n):
    end = np.arange(n) * CMP_STRIDE + CMP_LEN - 1
    dist = q_pos[:, None] - end[None, :]
    valid = (dist >= 0) & (np.arange(n)[None, :] < n - 1)
    return _bias_from_dist(tab_t, dist, valid)


def _overlap_np(n, n_sel, n_sel_pad):
    cs = np.arange(n)[:, None] * CMP_STRIDE
    js = np.arange(n_sel_pad)[None, :] * SEL_BLOCK
    ov = (cs < js + SEL_BLOCK) & (cs + CMP_LEN > js) & (np.arange(n)[:, None] < n - 1) & (np.arange(n_sel_pad)[None, :] < n_sel)
    return ov.astype(np.float32)


def _round_up(x, m):
    return -(-x // m) * m


def _reorder_w_in(w, dims):
    h_a, hkv_a, h_b, hkv_b, h_c, hkv_c, h_d, hkv_d = dims
    widths = [h_a * DH] + [hkv_a * DH] * 6 + [h_a * 3, h_b * DH, hkv_b * DH, hkv_b * DH, h_b,
                                             h_c * 2 * DH, hkv_c * 2 * DH, hkv_c * 2 * DH, h_d * DH, hkv_d * DH, hkv_d * DH]
    offs = np.concatenate([[0], np.cumsum(widths)])
    seg = lambda i: w[:, offs[i]:offs[i + 1]]
    D = w.shape[0]
    g_c = h_c // hkv_c
    c_q = seg(12).reshape(D, hkv_c, g_c, 2, DH).transpose(0, 1, 3, 2, 4).reshape(D, -1)
    pad = lambda a: jnp.pad(a, ((0, 0), (0, LANE - a.shape[1])))
    return jnp.concatenate([seg(0), seg(8), seg(15), c_q, seg(1), seg(2), seg(3), seg(4), seg(5), seg(6),
                            seg(9), seg(10), seg(13), seg(14), seg(16), seg(17), pad(seg(7)), pad(seg(11))], axis=1)


def _layer(x, B, T, mod, layer, past, prm, tabs, dims, tiles):
    h_a, hkv_a, h_b, hkv_b, h_c, hkv_c, h_d, hkv_d = dims
    g_a, g_b, g_c, g_d = h_a // hkv_a, h_b // hkv_b, h_c // hkv_c, h_d // hkv_d
    D = x.shape[1]
    M = B * T
    sh1, sc1, gt1, sh2, sc2, gt2 = [m[:, None, :] for m in jnp.split(mod, 6, axis=-1)]
    tm_n = tiles["norm_tm"]
    h = _norm(x, prm["norm_attn"], layer, tm=tm_n, sc=sc1, sh=sh1, rows_per_batch=T, out_dtype=BF16)
    proj = _matmul(h, prm["w_in_r"], None, **tiles["w_in"]).reshape(B, T, U_TOTAL * LANE)
    new = {
        "nsa": proj[:, :, U_NSA * LANE:U_WIN * LANE], "fox": proj[:, :, U_FOX * LANE:U_DIFF * LANE],
        "diff": proj[:, :, U_DIFF * LANE:U_MOBA * LANE], "moba": proj[:, :, U_MOBA * LANE:U_AG * LANE],
    }
    kw_new = proj[:, :, U_WIN * LANE:U_FOX * LANE]
    tq, tk = tiles["tq"], tiles["tk"]

    if past is None:
        q_off, Tk = 0, T
        src = dict(
            cmp=_lane_src(proj, U_NSA, jstride=hkv_a),
            sel_k=_lane_src(proj, U_NSA + 2 * hkv_a), sel_v=_lane_src(proj, U_NSA + 3 * hkv_a),
            win_k=_lane_src(proj, U_WIN), win_v=_lane_src(proj, U_WIN + hkv_a),
            fox_k=_lane_src(proj, U_FOX), fox_v=_lane_src(proj, U_FOX + hkv_b),
            diff_k=[_lane_src(proj, U_DIFF + half, stride=2) for half in range(2)],
            diff_v=_lane_src(proj, U_DIFF + 2 * hkv_c),
            moba_k=_lane_src(proj, U_MOBA), moba_v=_lane_src(proj, U_MOBA + hkv_d))
        assert T >= tiles["wbuf"]
        new["win"] = kw_new[:, T - tiles["wbuf"]:]
        lf_new, cum = _fox_cum(proj[:, :, U_BF * LANE:], prm["fox_fbias"], None)
        mode, nsteps = "causal", T // tk
        band_steps = tabs["win"].shape[1]
        n_cmp_pad = T // CMP_STRIDE
        tk_pad = T
    else:
        q_off = past["len"]
        Tk = q_off + T
        page = past["page"]

        pieces = past["pieces"]

        def contiguous(name, n_lanes, slots):
            new_page = jnp.pad(new[name][:, :, :n_lanes], ((0, 0), (0, page - T), (0, 0)))
            return _gather_pages(past[name], layer, past["pt"], new_page, slots=slots,
                                 n_pieces=pieces[name]["n"])

        pc = pieces["nsa"]["of"]
        nsa_cmp = contiguous("nsa", 2 * hkv_a * DH, [pc(t, h) for t in range(2) for h in range(hkv_a)])
        pc = pieces["moba"]["of"]
        moba_means = _block_means_paged(past["moba"], pieces["moba"]["n"], layer, past["pt"],
                                        [pc(0, h) for h in range(hkv_d)])
        past_lf = _gather_pages(past["logf"], layer, past["pt"], jnp.zeros((B, page, h_b), F32))
        win_kv = jnp.concatenate([past["win"][layer], kw_new], axis=1)
        wbuf = past["win"].shape[2]
        new["win"] = win_kv[:, -wbuf:]
        src = dict(cmp=_lane_src(nsa_cmp, 0, jstride=hkv_a), moba_k=None,
                   win_k=_lane_src(win_kv, 0), win_v=_lane_src(win_kv, hkv_a))
        lf_new, cum = _fox_cum(proj[:, :, U_BF * LANE:], prm["fox_fbias"], past_lf, n_past=q_off)
        n_cmp_pad = q_off // CMP_STRIDE
        tk_pad = tabs["causal"].shape[1] * tabs["causal"].shape[3]

        def paged(name, q_block, groups, out_heads, bias, dv=DH, **kw):
            info = pieces[name]
            rows = new[name].reshape(B, T, info["n_t"], info["hkv"], info["halves"], LANE)
            rows = rows.transpose(0, 1, 2, 4, 3, 5).reshape(B, T * info["n"], LANE)
            new_page = jnp.pad(rows, ((0, 0), (0, (page - T) * info["n"]), (0, 0)))
            return _flash_paged(proj, q_block, groups, out_heads, past[name], info["n"], layer, past["pt"],
                                new_page, bias, dv=dv, **kw)
    new["logf"] = lf_new

    cmp_kv = _compress(src["cmp"], hkv_a, n_cmp_pad, layer, prm["cmp_pos"], prm["cmp_w1"], prm["cmp_w2"])
    n_sel = -(-Tk // SEL_BLOCK)
    o_cmp, sel_mask = _nsa_cmp(proj, U_QA, cmp_kv, tabs["cmp"], tabs["overlap"], hkv=hkv_a, G=g_a,
                               tq=min(tiles["cmp_tq"], T), q_off=q_off, n_sel=n_sel)
    sel_mask = sel_mask[:, :, None]
    cum_t = cum.transpose(0, 2, 1)
    cq = cum_t[:, :, Tk - T:, None]
    ck = jnp.pad(cum_t, ((0, 0), (0, 0), (0, tk_pad - Tk)))[:, :, None, :]
    nb = -(-Tk // MOBA_BLOCK)
    moba_mask = _moba_gate(proj, U_QD, src["moba_k"], hkv=hkv_d, G=g_d, Tk=Tk, q_off=q_off,
                           nbp=_round_up(nb, LANE), means=None if past is None else moba_means)
    if past is None:
        o_sel = _flash(proj, U_QA, src["sel_k"], src["sel_v"], tabs["a"], hkv=hkv_a, G=g_a,
                       tq=tq, tk=tk, dv=DH, mode=mode, nsteps=nsteps, mask=sel_mask, mask_blk=SEL_BLOCK)
        o_win = _flash(proj, U_QA, src["win_k"], src["win_v"], (tabs["win"], 0, False), hkv=hkv_a, G=g_a,
                       tq=tabs["win"].shape[2], tk=tabs["win"].shape[3], dv=DH, mode="band", nsteps=band_steps)
        o_b = _flash(proj, U_QB, src["fox_k"], src["fox_v"], tabs["mask"], hkv=hkv_b, G=g_b,
                     tq=tq, tk=tk, dv=DH, mode=mode, nsteps=nsteps, fox=(cq, ck))
        o_c = [_flash(proj, U_QC + half * g_c, src["diff_k"][half], src["diff_v"], tabs["c"],
                      hkv=hkv_c, G=g_c, tq=tq, tk=tk, dv=2 * DH, mode=mode, nsteps=nsteps, q_stride=2)
               for half in range(2)]
        o_d = _flash(proj, U_QD, src["moba_k"], src["moba_v"], tabs["d"], hkv=hkv_d, G=g_d,
                     tq=tq, tk=tk, dv=DH, mode=mode, nsteps=nsteps, mask=moba_mask, mask_blk=MOBA_BLOCK)
    else:
        o_win = _flash(proj, U_QA, src["win_k"], src["win_v"], (tabs["win"], 0, False), hkv=hkv_a, G=g_a,
                       tq=tq, tk=tabs["win"].shape[3], dv=DH, mode="full", nsteps=1)
        causal = tabs["causal"]
        pc = pieces["nsa"]["of"]
        o_sel, = paged("nsa", U_QA // 8, [
            dict(q_units=[h * g_a + g for g in range(g_a)], k_piece=pc(2, h), v_pieces=[pc(3, h)],
                 bias_head=h * g_a, mask_head=h, out=(0, h * g_a)) for h in range(hkv_a)],
            [h_a], causal[:h_a], mask=sel_mask, mask_blk=SEL_BLOCK)
        pc = pieces["fox"]["of"]
        o_b, = paged("fox", U_QB // 8, [
            dict(q_units=[h * g_b + g for g in range(g_b)], k_piece=pc(0, h), v_pieces=[pc(1, h)],
                 bias_head=None, fox_head=h * g_b, out=(0, h * g_b)) for h in range(hkv_b)],
            [h_b], causal[h_a + h_c + h_d:], fox=(cq, ck))
        pc = pieces["diff"]["of"]
        o_c = paged("diff", U_QC // 8, [
            dict(q_units=[(h * 2 + half) * g_c + g for g in range(g_c)], k_piece=pc(0, h, half),
                 v_pieces=[pc(1, h, 0), pc(1, h, 1)], bias_head=h * g_c, out=(half, h * g_c))
            for half in range(2) for h in range(hkv_c)],
            [h_c, h_c], causal[h_a:h_a + h_c], dv=2 * DH)
        pc = pieces["moba"]["of"]
        o_d, = paged("moba", U_QD // 8, [
            dict(q_units=[h * g_d + g for g in range(g_d)], k_piece=pc(0, h), v_pieces=[pc(1, h)],
                 bias_head=h * g_d, mask_head=h, out=(0, h * g_d)) for h in range(hkv_d)],
            [h_d], causal[h_a + h_c:h_a + h_c + h_d], mask=moba_mask, mask_blk=MOBA_BLOCK)

    lam_init = 0.8 - 0.6 * math.exp(-0.3 * layer)
    flat = lambda a: a.reshape(M, a.shape[-1])
    o = _combine(flat(o_cmp), flat(o_sel), flat(o_win), flat(proj), flat(o_b), flat(o_c[0]), flat(o_c[1]),
                 flat(o_d), prm["mix_gain"], prm["diff_lambda"], layer, tm=tiles["comb_tm"], lam_init=lam_init)

    if T % tiles["w_out"]["tm"] == 0:
        gates = dict(gate1=gt1, gate2=gt2, rows_per_batch=T)
    else:
        gates = dict(gate1=jnp.repeat(gt1[:, 0], T, axis=0), gate2=jnp.repeat(gt2[:, 0], T, axis=0), rows_per_batch=None)
    x = _matmul(o, prm["w_out"], layer, res=x, gate=gates["gate1"], rows_per_batch=gates["rows_per_batch"],
                **tiles["w_out"])
    h2 = _norm(x, prm["norm_ffn"], layer, tm=tm_n, sc=sc2, sh=sh2, rows_per_batch=T, out_dtype=BF16)
    act = _matmul(h2, prm["w_gate"], layer, w2=prm["w_up"], out_dtype=BF16, **tiles["w_ff"])
    x = _matmul(act, prm["w_down"], layer, res=x, gate=gates["gate2"], rows_per_batch=gates["rows_per_batch"],
                **tiles["w_down"])
    return x, new


def _group_tables(t5_table, dims, T, q_off, tiles, tk_pad, win_len):
    h_a, hkv_a, h_b, hkv_b, h_c, hkv_c, h_d, hkv_d = dims
    tab_t = t5_table.astype(F32).T
    tab_a = tab_t[:h_a]
    tq, tk = tiles["tq"], tiles["tk"]
    Tk = q_off + T
    n = (Tk - CMP_LEN) // CMP_STRIDE + 2
    n_sel = -(-Tk // SEL_BLOCK)
    tabs = {"overlap": jnp.asarray(_overlap_np(n, n_sel, _round_up(n_sel, LANE)))}
    bd = jnp.concatenate([_dist_table(tab_t), _dist_table(None)], axis=0)
    n_bias = tab_t.shape[0]
    if q_off == 0:
        assert tq == tk
        tw = tiles["win_t"]
        nband = (WINDOW - 1 + tw - 1) // tw + 1
        tabs.update(cmp=_cmp_bias_prompt(bd[:h_a], T, n), causal=_prompt_bias(bd, tq, 3),
                    win=_prompt_bias(bd[:h_a], tw, min(nband, T // tw), WINDOW))
    else:
        assert win_len % 8 == 0 and q_off >= win_len - T
        tabs.update(cmp=_cmp_bias(tab_a, q_off + np.arange(T), n), causal=_decode_bias(bd, T, q_off, tk_pad, tk),
                    win=_decode_bias(bd[:h_a], T, win_len - T, win_len, win_len, WINDOW))
    tabs.update(a=(tabs["causal"], 0, False), c=(tabs["causal"], h_a, False), d=(tabs["causal"], h_a + h_c, False),
                mask=(tabs["causal"], n_bias, True))
    return tabs


PROMPT_TILES = dict(
    tq=512, tk=512, win_t=512, cmp_tq=1024, norm_tm=256, comb_tm=256,
    w_in=dict(tm=1024, tn=512, tk=4096), w_out=dict(tm=1024, tn=512, tk=4096),
    w_ff=dict(tm=1024, tn=256, tk=4096), w_down=dict(tm=2048, tn=512, tk=1024),
)


def _sample_tiles(M, T, tk):
    return dict(
        tq=T, tk=tk, cmp_tq=T, norm_tm=T, comb_tm=M,
        w_in=dict(tm=M, tn=512, tk=4096), w_out=dict(tm=M, tn=512, tk=4096),
        w_ff=dict(tm=M, tn=256, tk=4096), w_down=dict(tm=M, tn=1024, tk=1024),
    )


def kernel(x_prompt, x_sample, cache_nsa, state_nsa_win, cache_fox, cache_fox_logf, cache_diff, cache_moba,
           page_table, c_prompt, c_sample, t5_table, ada_w, ada_b, norm_attn, norm_ffn, w_in, w_out, mix_gain,
           nsa_cmp_pos, nsa_cmp_w1, nsa_cmp_w2, fox_fbias, diff_lambda, w_gate, w_up, w_down, final_norm):
    depth = w_in.shape[0]
    B, T, D = x_prompt.shape
    Bs, Ts, _ = x_sample.shape
    hkv_a, hkv_b, hkv_c, hkv_d = cache_nsa.shape[4], cache_fox.shape[4], cache_diff.shape[4], cache_moba.shape[4]
    h_b, h_c = fox_fbias.shape[1], diff_lambda.shape[1]
    h_a = (D // DH - h_b - 2 * h_c) // 2
    h_d = h_a
    dims = (h_a, hkv_a, h_b, hkv_b, h_c, hkv_c, h_d, hkv_d)
    page = cache_nsa.shape[2]
    n_pages = page_table.shape[1]
    past_len = n_pages * page
    pages_per_step = math.gcd(PAGED_PER_STEP, n_pages)
    tk_s = pages_per_step * page
    tk_pad = (n_pages // pages_per_step + 1) * tk_s
    win_len = state_nsa_win.shape[2] + Ts

    p_tiles = dict(PROMPT_TILES, wbuf=state_nsa_win.shape[2])
    s_tiles = _sample_tiles(Bs * Ts, Ts, tk_s)
    tabs_p = _group_tables(t5_table, dims, T, 0, p_tiles, T, 0)
    tabs_s = _group_tables(t5_table, dims, Ts, past_len, s_tiles, tk_pad, win_len)

    def piece_rows(c):
        d, n_phys, pg, n_t, hkv, w = c.shape
        halves = w // LANE
        c = c.reshape(d, n_phys, pg, n_t, hkv, halves, LANE).transpose(0, 1, 2, 3, 5, 4, 6)
        info = dict(n=n_t * hkv * halves, n_t=n_t, hkv=hkv, halves=halves,
                    of=lambda t, h, half=0: (t * halves + half) * hkv + h)
        return c.reshape(d, n_phys, pg * info["n"], LANE), info

    rows = {name: piece_rows(c) for name, c in
            (("nsa", cache_nsa), ("fox", cache_fox), ("diff", cache_diff), ("moba", cache_moba))}
    past = dict({name: r[0] for name, r in rows.items()}, pieces={name: r[1] for name, r in rows.items()},
                logf=cache_fox_logf,
                win=state_nsa_win.reshape(depth, Bs, state_nsa_win.shape[2], -1),
                pt=page_table, len=past_len, page=page)

    n_c = _round_up(B + Bs, 8)
    c_all = jnp.pad(jnp.concatenate([c_prompt, c_sample], axis=0), ((0, n_c - B - Bs), (0, 0)))
    xp, xs = x_prompt.reshape(B * T, D), x_sample.reshape(Bs * Ts, D)
    names = ("nsa", "win", "fox", "logf", "diff", "moba")
    st_p = {n: [] for n in names}
    st_s = {n: [] for n in names}
    r3 = lambda a: a.reshape(a.shape[0], 1, a.shape[1])
    for l in range(depth):
        mod = _matmul(c_all, ada_w, l, bias=ada_b[l][None], silu_in=True, tm=n_c, tn=512, tk=D)
        prm = dict(norm_attn=r3(norm_attn), norm_ffn=r3(norm_ffn), w_in_r=_reorder_w_in(w_in[l], dims), w_out=w_out,
                   mix_gain=r3(mix_gain), cmp_pos=nsa_cmp_pos, cmp_w1=nsa_cmp_w1, cmp_w2=nsa_cmp_w2,
                   fox_fbias=fox_fbias[l][None], diff_lambda=diff_lambda, w_gate=w_gate, w_up=w_up, w_down=w_down)
        xp, new_p = _layer(xp, B, T, mod[:B], l, None, prm, tabs_p, dims, p_tiles)
        xs, new_s = _layer(xs, Bs, Ts, mod[B:B + Bs], l, past, prm, tabs_s, dims, s_tiles)
        for n in names:
            st_p[n].append(new_p[n])
            st_s[n].append(new_s[n])
    y_p = _norm(xp, final_norm[None], None, tm=p_tiles["norm_tm"]).reshape(B, T, D)
    y_s = _norm(xs, final_norm[None], None, tm=s_tiles["norm_tm"]).reshape(Bs, Ts, D)

    def stack(st, name, tail):
        a = jnp.stack(st[name])
        return a.reshape(a.shape[:3] + tail)

    kv = lambda n, hk, w: (n, hk, w)
    shapes = {"nsa": kv(4, hkv_a, DH), "win": kv(2, hkv_a, DH), "fox": kv(2, hkv_b, DH), "logf": (h_b,),
              "diff": kv(2, hkv_c, 2 * DH), "moba": kv(2, hkv_d, DH)}
    out = [y_p, y_s]
    for n in names:
        out += [stack(st_p, n, shapes[n]), stack(st_s, n, shapes[n])]
    return tuple(out)
```

```python
import functools
import math

import jax
import jax.numpy as jnp
import numpy as np
from jax import lax
from jax.experimental import pallas as pl
from jax.experimental.pallas import tpu as pltpu

DH = 128
CMP_LEN = 32
CMP_STRIDE = 16
SEL_BLOCK = 64
SEL_TOP = 16
WINDOW = 512
MOBA_BLOCK = 256
MOBA_TOP = 3
N_BUCKETS = 32
MAX_DIST = 128
NEG = -1e30
FORCE = 1e4
RMS_EPS = 1e-6
BELOW_NEG = -3e38

LANE = 128
V7X_VMEM_LIMIT_BYTES = 60000 * 1024

U_QA, U_QB, U_QD, U_QC = 0, 8, 16, 24
U_NSA, U_WIN, U_FOX, U_DIFF, U_MOBA, U_AG, U_BF, U_TOTAL = 32, 40, 44, 48, 56, 60, 61, 62

BF16 = jnp.bfloat16
F32 = jnp.float32


def _cparams(*sem):
    return pltpu.CompilerParams(dimension_semantics=sem, vmem_limit_bytes=V7X_VMEM_LIMIT_BYTES)


def _silu(x):
    return x * jax.nn.sigmoid(x)


def _div_pow2(x, d):
    assert d & (d - 1) == 0
    return lax.shift_right_logical(x, jnp.int32(d.bit_length() - 1))


def _lane_src(arr, unit0, stride=1, jstride=0):
    def make_spec(rows, width, to_brhj):
        def index_map(*g):
            b, r, h, j = to_brhj(*g)
            return (b, r, (unit0 * DH) // width + h * stride + j * jstride)
        return pl.BlockSpec((None, rows, width), index_map)
    return arr, make_spec


def _mm_body(*refs, nk, tk, k_rem, silu_in, dual, has_bias, has_res):
    it = iter(refs)
    x_ref, w_ref = next(it), next(it)
    w2_ref = next(it) if dual else None
    b_ref = next(it) if has_bias else None
    r_ref, g_ref = (next(it), next(it)) if has_res else (None, None)
    o_ref, acc = next(it), next(it)
    acc2 = next(it) if dual else None
    k = pl.program_id(2)

    @pl.when(k == 0)
    def _():
        acc[...] = jnp.zeros_like(acc)
        if dual:
            acc2[...] = jnp.zeros_like(acc2)

    def accumulate(overhang):
        xv = x_ref[...]
        if silu_in:
            xv = _silu(xv.astype(F32))
        xv = xv.astype(BF16)
        wv = w_ref[...].astype(BF16)
        w2v = w2_ref[...].astype(BF16) if dual else None
        if overhang:
            xv = jnp.where(lax.broadcasted_iota(jnp.int32, xv.shape, 1) < k_rem, xv, jnp.zeros_like(xv))
            rows = lax.broadcasted_iota(jnp.int32, wv.shape, 0) < k_rem
            wv = jnp.where(rows, wv, jnp.zeros_like(wv))
            if dual:
                w2v = jnp.where(rows, w2v, jnp.zeros_like(w2v))
        acc[...] += jnp.dot(xv, wv, preferred_element_type=F32)
        if dual:
            acc2[...] += jnp.dot(xv, w2v, preferred_element_type=F32)

    if k_rem:
        pl.when(k < nk - 1)(lambda: accumulate(False))
        pl.when(k == nk - 1)(lambda: accumulate(True))
    else:
        accumulate(False)

    @pl.when(k == nk - 1)
    def _():
        r = acc[...]
        if dual:
            r = _silu(r) * acc2[...]
        if has_bias:
            r = r + b_ref[...]
        if has_res:
            r = r_ref[...] + g_ref[...] * r
        o_ref[...] = r.astype(o_ref.dtype)


def _matmul(x, w, layer, *, tm, tn, tk, w2=None, bias=None, res=None, gate=None, rows_per_batch=None,
            silu_in=False, out_dtype=F32):
    M, K = x.shape
    N = w.shape[-1]
    assert M % tm == 0
    nk = pl.cdiv(K, tk)
    k_rem = K % tk
    grid = (M // tm, pl.cdiv(N, tn), nk)
    if layer is None:
        w_spec = pl.BlockSpec((tk, tn), lambda m, n, k: (k, n))
    else:
        w_spec = pl.BlockSpec((None, tk, tn), lambda m, n, k: (layer, k, n))
    in_specs = [pl.BlockSpec((tm, tk), lambda m, n, k: (m, k)), w_spec]
    args = [x, w]
    if w2 is not None:
        in_specs.append(w_spec)
        args.append(w2)
    if bias is not None:
        in_specs.append(pl.BlockSpec((1, tn), lambda m, n, k: (0, n)))
        args.append(bias)
    if res is not None:
        in_specs.append(pl.BlockSpec((tm, tn), lambda m, n, k: (m, n)))
        args.append(res)
        if gate.ndim == 2:
            in_specs.append(pl.BlockSpec((tm, tn), lambda m, n, k: (m, n)))
        else:
            assert rows_per_batch % tm == 0
            per = rows_per_batch // tm
            in_specs.append(pl.BlockSpec((None, 1, tn), lambda m, n, k: (m // per, 0, n)))
        args.append(gate)
    scratch = [pltpu.VMEM((tm, tn), F32)] * (2 if w2 is not None else 1)
    body = functools.partial(_mm_body, nk=nk, tk=tk, k_rem=k_rem, silu_in=silu_in, dual=w2 is not None,
                             has_bias=bias is not None, has_res=res is not None)
    return pl.pallas_call(
        body, grid=grid, in_specs=in_specs,
        out_specs=pl.BlockSpec((tm, tn), lambda m, n, k: (m, n)),
        out_shape=jax.ShapeDtypeStruct((M, N), out_dtype),
        scratch_shapes=scratch,
        compiler_params=_cparams("parallel", "parallel", "arbitrary"),
    )(*args)


def _norm_body(*refs, modulated):
    if modulated:
        x_ref, g_ref, sc_ref, sh_ref, o_ref = refs
    else:
        x_ref, g_ref, o_ref = refs
    x = x_ref[...]
    y = x * lax.rsqrt(jnp.mean(x * x, axis=-1, keepdims=True) + RMS_EPS) * g_ref[...]
    if modulated:
        y = y * (1.0 + sc_ref[...]) + sh_ref[...]
    o_ref[...] = y.astype(o_ref.dtype)


def _norm(x, g, layer, *, tm, sc=None, sh=None, rows_per_batch=None, out_dtype=F32):
    M, D = x.shape
    modulated = sc is not None
    if layer is None:
        g_spec = pl.BlockSpec((1, D), lambda m: (0, 0))
    else:
        g_spec = pl.BlockSpec((None, 1, D), lambda m: (layer, 0, 0))
    in_specs = [pl.BlockSpec((tm, D), lambda m: (m, 0)), g_spec]
    args = [x, g]
    if modulated:
        assert rows_per_batch % tm == 0
        per = rows_per_batch // tm
        mod_spec = pl.BlockSpec((None, 1, D), lambda m: (m // per, 0, 0))
        in_specs += [mod_spec, mod_spec]
        args += [sc, sh]
    return pl.pallas_call(
        functools.partial(_norm_body, modulated=modulated), grid=(M // tm,), in_specs=in_specs,
        out_specs=pl.BlockSpec((tm, D), lambda m: (m, 0)),
        out_shape=jax.ShapeDtypeStruct((M, D), out_dtype),
        compiler_params=_cparams("parallel"),
    )(*args)


GATHER_PAGES_PER_STEP = 16


def _gather_body(pt_ref, *refs, n_groups, per, page, slots, n_pieces):
    cache_refs, new_ref, o_ref = refs[:per], refs[per], refs[per + 1]
    j = pl.program_id(1)

    @pl.when(j < n_groups)
    def _():
        for p in range(per):
            if slots is None:
                o_ref[p * page:(p + 1) * page] = cache_refs[p][...]
            else:
                for s, src_slot in enumerate(slots):
                    o_ref[p * page:(p + 1) * page, s * LANE:(s + 1) * LANE] = (
                        cache_refs[p][pl.ds(src_slot, page, stride=n_pieces), :])

    @pl.when(j == n_groups)
    def _():
        o_ref[0:page] = new_ref[...]
        if per > 1:
            o_ref[page:per * page] = jnp.zeros(((per - 1) * page,) + o_ref.shape[1:], o_ref.dtype)


def _gather_pages(cache, layer, page_table, new_page, slots=None, n_pieces=1):
    page, W = new_page.shape[1:]
    B, n_pages = page_table.shape
    per = math.gcd(GATHER_PAGES_PER_STEP, n_pages)
    n_groups = n_pages // per
    assert cache.shape[2:] == ((page, W) if slots is None else (page * n_pieces, LANE))
    assert slots is None or W == len(slots) * LANE

    def page_spec(p):
        return pl.BlockSpec((None, None) + cache.shape[2:],
                            lambda b, j, pt: (layer, pt[b, jnp.minimum(j, n_groups - 1) * per + p], 0, 0))

    grid_spec = pltpu.PrefetchScalarGridSpec(
        num_scalar_prefetch=1, grid=(B, n_groups + 1),
        in_specs=[page_spec(p) for p in range(per)] + [pl.BlockSpec((None, page, W), lambda b, j, pt: (b, 0, 0))],
        out_specs=pl.BlockSpec((None, per * page, W), lambda b, j, pt: (b, j, 0)),
    )
    return pl.pallas_call(
        functools.partial(_gather_body, n_groups=n_groups, per=per, page=page, slots=slots, n_pieces=n_pieces),
        grid_spec=grid_spec,
        out_shape=jax.ShapeDtypeStruct((B, (n_pages + 1) * page, W), cache.dtype),
        compiler_params=_cparams("parallel", "arbitrary"),
    )(page_table, *([cache] * per), new_page)


CUM_CHUNK = 256


def _cum_body(*refs, n_past, n_new, nh):
    if n_past:
        past_ref, raw_ref, fb_ref, lf_ref, cum_ref = refs
    else:
        raw_ref, fb_ref, lf_ref, cum_ref = refs
    z = raw_ref[...][:, :nh] + fb_ref[...]
    lf = -(jnp.maximum(-z, 0.0) + jnp.log1p(jnp.exp(-jnp.abs(z))))
    lf_ref[...] = lf

    def tri(n):
        return (lax.broadcasted_iota(jnp.int32, (n, n), 0) >= lax.broadcasted_iota(jnp.int32, (n, n), 1)).astype(F32)

    def scan_rows(src_ref, dst_off, n, carry):
        c = min(CUM_CHUNK, n)
        assert n % c == 0
        t = tri(c)
        for i in range(n // c):
            cum_ref[dst_off + i * c:dst_off + (i + 1) * c, :] = jnp.dot(
                t, src_ref[i * c:(i + 1) * c, :], preferred_element_type=F32, precision=lax.Precision.HIGHEST)
        for i in range(n // c):
            rows = slice(dst_off + i * c, dst_off + (i + 1) * c)
            total = cum_ref[dst_off + (i + 1) * c - 1:dst_off + (i + 1) * c, :]
            cum_ref[rows, :] = cum_ref[rows, :] + carry
            carry = carry + total
        return carry

    carry = jnp.zeros((1, nh), F32)
    if n_past:
        carry = scan_rows(past_ref, 0, n_past, carry)
    scan_rows(lf_ref, n_past, n_new, carry)


def _fox_cum(raw_f, fbias, past_lf, n_past=0):
    B, Tn, _ = raw_f.shape
    nh = fbias.shape[-1]
    in_specs, args = [], []
    if n_past:
        in_specs.append(pl.BlockSpec((None, n_past, nh), lambda b: (b, 0, 0)))
        args.append(past_lf)
    in_specs += [pl.BlockSpec((None, Tn, LANE), lambda b: (b, 0, 0)), pl.BlockSpec((1, nh), lambda b: (0, 0))]
    args += [raw_f, fbias]
    return pl.pallas_call(
        functools.partial(_cum_body, n_past=n_past, n_new=Tn, nh=nh), grid=(B,), in_specs=in_specs,
        out_specs=[pl.BlockSpec((None, Tn, nh), lambda b: (b, 0, 0)),
                   pl.BlockSpec((None, n_past + Tn, nh), lambda b: (b, 0, 0))],
        out_shape=[jax.ShapeDtypeStruct((B, Tn, nh), F32), jax.ShapeDtypeStruct((B, n_past + Tn, nh), F32)],
        compiler_params=_cparams("parallel"),
    )(*args)


def _gelu_tanh(x):
    return 0.5 * x * (1.0 + jnp.tanh(math.sqrt(2.0 / math.pi) * (x + 0.044715 * (x * x * x))))


def _compress_body(x_ref, pos_ref, w1_ref, w2_ref, o_ref, *, n):
    hidden = w1_ref.shape[-1]
    acc_lo = jnp.zeros((n, hidden), F32)
    acc_hi = jnp.zeros((n, hidden), F32)
    for rho in range(CMP_STRIDE):
        xr = x_ref[pl.ds(rho, n, stride=CMP_STRIDE), :]
        lo = (xr + pos_ref[rho:rho + 1, :]).astype(BF16)
        hi = (xr + pos_ref[rho + CMP_STRIDE:rho + CMP_STRIDE + 1, :]).astype(BF16)
        w_lo = w1_ref[rho * DH:(rho + 1) * DH, :].astype(BF16)
        w_hi = w1_ref[(rho + CMP_STRIDE) * DH:(rho + CMP_STRIDE + 1) * DH, :].astype(BF16)
        acc_lo += jnp.dot(lo, w_lo, preferred_element_type=F32)
        acc_hi += jnp.dot(hi, w_hi, preferred_element_type=F32)
    hid = acc_lo + pltpu.roll(acc_hi, n - 1, 0)
    o_ref[...] = jnp.dot(_gelu_tanh(hid).astype(BF16), w2_ref[...].astype(BF16), preferred_element_type=F32)


def _compress(src, hkv, n, layer, pos, w1, w2):
    kv, make_spec = src
    B = kv.shape[0]
    hidden = w1.shape[-1]
    return pl.pallas_call(
        functools.partial(_compress_body, n=n), grid=(B, 2, hkv),
        in_specs=[
            make_spec(n * CMP_STRIDE, DH, lambda b, j, h: (b, 0, h, j)),
            pl.BlockSpec((None, None, CMP_LEN, DH), lambda b, j, h: (layer, j, 0, 0)),
            pl.BlockSpec((None, None, CMP_LEN * DH, hidden), lambda b, j, h: (layer, j, 0, 0)),
            pl.BlockSpec((None, None, hidden, DH), lambda b, j, h: (layer, j, 0, 0)),
        ],
        out_specs=pl.BlockSpec((None, None, None, n, DH), lambda b, j, h: (b, j, h, 0, 0)),
        out_shape=jax.ShapeDtypeStruct((B, 2, hkv, n, DH), F32),
        compiler_params=_cparams("parallel", "arbitrary", "arbitrary"),
    )(kv, pos, w1, w2)


def _take_top(score, lane_ids, count):
    sel = jnp.zeros(score.shape, F32)
    taken = []
    big = jnp.int32(score.shape[-1])
    for _ in range(count):
        m = jnp.max(score, axis=-1, keepdims=True)
        idx = jnp.min(jnp.where(score == m, lane_ids, big), axis=-1, keepdims=True)
        hit = lane_ids == idx
        sel = jnp.where(hit, 1.0, sel)
        score = jnp.where(hit, BELOW_NEG, score)
        taken.append((hit, m))
    return sel, taken


def _cmp_body(q_ref, kc_ref, vc_ref, bias_ref, ov_ref, o_ref, sel_ref, *, G, tq, q_off, n_sel, top):
    qi = pl.program_id(2)
    kc = kc_ref[...]
    vc = vc_ref[...].astype(BF16)
    scale = DH ** -0.5
    psum = None
    outs = []
    for g in range(G):
        qg = q_ref[:, g * DH:(g + 1) * DH]
        s = lax.dot_general(qg, kc, (((1,), (1,)), ((), ())), preferred_element_type=F32,
                            precision=lax.Precision.HIGHEST) * scale + bias_ref[g]
        m = jnp.max(s, axis=-1, keepdims=True)
        e = jnp.where(s > NEG / 2, jnp.exp(s - m), 0.0)
        p = e / jnp.maximum(jnp.sum(e, axis=-1, keepdims=True), 1e-30)
        outs.append(jnp.dot(p.astype(BF16), vc, preferred_element_type=F32))
        psum = p if psum is None else psum + p
    o_ref[...] = jnp.concatenate(outs, axis=1)
    imp = jnp.dot(psum, ov_ref[...], preferred_element_type=F32, precision=lax.Precision.HIGHEST)
    shape = imp.shape
    jb = lax.broadcasted_iota(jnp.int32, shape, 1)
    qpos = q_off + qi * tq + lax.broadcasted_iota(jnp.int32, shape, 0)
    cur = _div_pow2(qpos, SEL_BLOCK)
    forced = (jb == 0) | (jb == cur) | (jb == cur - 1)
    score = jnp.where(jb <= cur, imp + jnp.where(forced, FORCE, 0.0), NEG)
    score = jnp.where(jb < n_sel, score, BELOW_NEG)
    sel, _ = _take_top(score, jb, top)
    sel_ref[...] = jnp.where((jb <= cur) & (sel > 0.5), 0.0, NEG)


def _nsa_cmp(q, u_q, cmp_kv, bias, overlap, *, hkv, G, tq, q_off, n_sel):
    B, Tq, _ = q.shape
    n = cmp_kv.shape[3]
    nsp = overlap.shape[1]
    top = min(SEL_TOP, n_sel)
    body = functools.partial(_cmp_body, G=G, tq=tq, q_off=q_off, n_sel=n_sel, top=top)
    return pl.pallas_call(
        body, grid=(B, hkv, Tq // tq),
        in_specs=[
            pl.BlockSpec((None, tq, G * DH), lambda b, h, i: (b, i, u_q // G + h)),
            pl.BlockSpec((None, None, None, n, DH), lambda b, h, i: (b, 0, h, 0, 0)),
            pl.BlockSpec((None, None, None, n, DH), lambda b, h, i: (b, 1, h, 0, 0)),
            pl.BlockSpec((G, tq, n), lambda b, h, i: (h, i, 0)),
            pl.BlockSpec((n, nsp), lambda b, h, i: (0, 0)),
        ],
        out_specs=[pl.BlockSpec((None, tq, G * DH), lambda b, h, i: (b, i, h)),
                   pl.BlockSpec((None, None, tq, nsp), lambda b, h, i: (b, h, i, 0))],
        out_shape=[jax.ShapeDtypeStruct((B, Tq, hkv * G * DH), F32),
                   jax.ShapeDtypeStruct((B, hkv, Tq, nsp), F32)],
        compiler_params=_cparams("parallel", "parallel", "arbitrary"),
    )(q, cmp_kv, cmp_kv, bias, overlap)


def _block_means_body(pt_ref, *refs, per, page, n_pieces, k_pieces):
    page_refs, o_ref = refs[:per], refs[per]
    for h, piece in enumerate(k_pieces):
        k = jnp.concatenate([ref[pl.ds(piece, page, stride=n_pieces), :] for ref in page_refs], axis=0)
        o_ref[h] = jnp.sum(k.reshape(per * page // MOBA_BLOCK, MOBA_BLOCK, DH), axis=1) * (1.0 / MOBA_BLOCK)


def _block_means_paged(cache, n_pieces, layer, page_table, k_pieces):
    B, n_pages = page_table.shape
    page = cache.shape[2] // n_pieces
    per = math.gcd(MEANS_PAGES_PER_STEP, n_pages)
    blocks = per * page // MOBA_BLOCK
    assert (per * page) % MOBA_BLOCK == 0 and blocks % 8 == 0

    def page_spec(p):
        return pl.BlockSpec((None, None, page * n_pieces, LANE), lambda b, j, pt: (layer, pt[b, j * per + p], 0, 0))

    grid_spec = pltpu.PrefetchScalarGridSpec(
        num_scalar_prefetch=1, grid=(B, n_pages // per), in_specs=[page_spec(p) for p in range(per)],
        out_specs=pl.BlockSpec((None, len(k_pieces), blocks, DH), lambda b, j, pt: (b, 0, j, 0)))
    body = functools.partial(_block_means_body, per=per, page=page, n_pieces=n_pieces, k_pieces=k_pieces)
    return pl.pallas_call(
        body, grid_spec=grid_spec,
        out_shape=jax.ShapeDtypeStruct((B, len(k_pieces), n_pages * page // MOBA_BLOCK, DH), F32),
        compiler_params=_cparams("parallel", "arbitrary"),
    )(page_table, *([cache] * per))


MEANS_PAGES_PER_STEP = 16


def _moba_gate_body(q_ref, k_ref, m_ref, *, G, n_full, nbp, q_off, top, is_means):
    Tq = q_ref.shape[0]
    shape = (Tq, nbp)
    jb = lax.broadcasted_iota(jnp.int32, shape, 1)
    own = _div_pow2(q_off + lax.broadcasted_iota(jnp.int32, shape, 0), MOBA_BLOCK)
    if n_full > 0:
        if is_means:
            kmean = k_ref[...]
        else:
            kmean = jnp.sum(k_ref[...].reshape(n_full, MOBA_BLOCK, DH), axis=1) * (1.0 / MOBA_BLOCK)
        if nbp > n_full:
            kmean = jnp.concatenate([kmean, jnp.zeros((nbp - n_full, DH), F32)], axis=0)
    for g in range(G):
        mask = jnp.where(jb == own, 0.0, NEG)
        if n_full > 0:
            gate = lax.dot_general(q_ref[:, g * DH:(g + 1) * DH], kmean, (((1,), (1,)), ((), ())),
                                   preferred_element_type=F32, precision=lax.Precision.HIGHEST)
            score = jnp.where((jb < own) & (jb < n_full), gate, NEG)
            score = jnp.where(jb < n_full, score, BELOW_NEG)
            _, taken = _take_top(score, jb, top)
            for hit, val in taken:
                mask = jnp.where(hit & (val > NEG / 2), 0.0, mask)
        m_ref[g] = mask


def _moba_gate(q, u_q, ksrc, *, hkv, G, Tk, q_off, nbp, means=None):
    B, Tq, _ = q.shape
    n_full = Tk // MOBA_BLOCK
    top = min(MOBA_TOP, n_full)
    if means is None:
        k, k_spec = ksrc
        k_in = k_spec(max(n_full, 1) * MOBA_BLOCK, DH, lambda b, h: (b, 0, h, 0))
    else:
        assert means.shape[2] == n_full
        k, k_in = means, pl.BlockSpec((None, None, n_full, DH), lambda b, h: (b, h, 0, 0))
    body = functools.partial(_moba_gate_body, G=G, n_full=n_full, nbp=nbp, q_off=q_off, top=top,
                             is_means=means is not None)
    return pl.pallas_call(
        body, grid=(B, hkv),
        in_specs=[pl.BlockSpec((None, Tq, G * DH), lambda b, h: (b, 0, u_q // G + h)), k_in],
        out_specs=pl.BlockSpec((None, None, G, Tq, nbp), lambda b, h: (b, h, 0, 0, 0)),
        out_shape=jax.ShapeDtypeStruct((B, hkv, G, Tq, nbp), F32),
        compiler_params=_cparams("parallel", "arbitrary"),
    )(q, k)


PAIR_Q, PAIR_K, PAIR_VARIANT, PAIR_FLAGS = 0, 1, 2, 3
PAIR_FIRST, PAIR_LAST = 1, 2


def _flash_body(pairs_ref, *refs, G, Gm, tq, tk, dv, fox, mask_blk):
    it = iter(refs)
    q_ref, k_ref, v_ref, bias_ref = next(it), next(it), next(it), next(it)
    cq_ref, ck_ref = (next(it), next(it)) if fox else (None, None)
    mask_ref = next(it) if mask_blk else None
    o_ref, m_sc, acc_sc = next(it), next(it), next(it)
    pair = pl.program_id(2)
    ki = pairs_ref[PAIR_K, pair]
    flags = pairs_ref[PAIR_FLAGS, pair]

    @pl.when(flags % 2 == PAIR_FIRST)
    def _():
        m_sc[...] = jnp.full(m_sc.shape, NEG, F32)
        acc_sc[...] = jnp.zeros_like(acc_sc)

    def update():
        k = k_ref[...].astype(BF16)
        v = jnp.concatenate([v_ref[...].astype(BF16), jnp.ones((tk, LANE), BF16)], axis=1)
        q = jnp.concatenate([q_ref[:, g * DH:(g + 1) * DH] for g in range(G)], axis=0)
        q = (q * (DH ** -0.5)).astype(BF16)
        s = lax.dot_general(q, k, (((1,), (1,)), ((), ())), preferred_element_type=F32)
        s = s.reshape(G, tq, tk) + bias_ref[...]
        if fox:
            s = s + (cq_ref[...] - ck_ref[...])
        if mask_blk:
            nbp = mask_ref.shape[-1]
            kpos = ki * tk + lax.broadcasted_iota(jnp.int32, (nbp, tk), 1)
            lo = lax.broadcasted_iota(jnp.int32, (nbp, tk), 0) * mask_blk
            expand = jnp.where((kpos >= lo) & (kpos < lo + mask_blk), 1.0, 0.0).astype(BF16)
            picked = jnp.dot(mask_ref[...].reshape(Gm * tq, nbp).astype(BF16), expand, preferred_element_type=F32)
            s = s + picked.reshape(Gm, tq, tk)
        m_old = m_sc[...]
        m_new = jnp.maximum(m_old, jnp.max(s, axis=-1, keepdims=True))
        m_use = jnp.where(m_new < NEG / 2, 0.0, m_new)
        p = jnp.exp((s - m_use).astype(BF16))
        alpha = jnp.exp(m_old - m_use)
        pv = jnp.dot(p.reshape(G * tq, tk), v, preferred_element_type=F32)
        acc_sc[...] = alpha * acc_sc[...] + pv.reshape(G, tq, dv + LANE)
        m_sc[...] = m_new

    update()

    @pl.when(flags >= PAIR_LAST)
    def _():
        acc = acc_sc[...]
        o = acc[:, :, :dv] / jnp.maximum(acc[:, :, dv:dv + 1], 1e-30)
        for g in range(G):
            o_ref[:, g * dv:(g + 1) * dv] = o[g]


def _flash(q, u_q, ksrc, vsrc, bias, *, hkv, G, tq, tk, dv, mode, nsteps, fox=None, mask=None, mask_blk=0,
           q_stride=1):
    B, Tq, _ = q.shape
    nq = Tq // tq
    bias, bias_h0, bias_shared = bias
    NB = bias.shape[1]
    Gb = 1 if bias_shared else G
    assert bias_h0 % Gb == 0

    if mode == "causal":
        pairs = [(i, s, min(i - s, NB - 1)) for i in range(nq) for s in range(min(i + 1, nsteps))]
    elif mode == "band":
        pairs = [(i, i - d, d) for i in range(nq) for d in range(min(nsteps, NB) - 1, -1, -1) if i - d >= 0]
    else:
        pairs = [(i, s, s) for i in range(nq) for s in range(nsteps)]
    table = np.zeros((4, len(pairs)), np.int32)
    for n, (i, s, var) in enumerate(pairs):
        first = n == 0 or pairs[n - 1][0] != i
        last = n == len(pairs) - 1 or pairs[n + 1][0] != i
        table[:, n] = (i, s, var, PAIR_FIRST * first + PAIR_LAST * last)
    qt = lambda t, p: t[PAIR_Q, p]
    kt = lambda t, p: t[PAIR_K, p]

    kv_index = lambda b, h, p, t: (b, kt(t, p), h, 0)
    in_specs = [
        pl.BlockSpec((None, tq, G * DH), lambda b, h, p, t: (b, qt(t, p), u_q // G + h * q_stride)),
        ksrc[1](tk, DH, kv_index),
        vsrc[1](tk, dv, kv_index),
        pl.BlockSpec((Gb, None, tq, tk),
                     lambda b, h, p, t: (bias_h0 // Gb + (0 if bias_shared else h), t[PAIR_VARIANT, p], 0, 0)),
    ]
    args = [q, ksrc[0], vsrc[0], bias]
    if fox is not None:
        in_specs += [pl.BlockSpec((None, G, tq, 1), lambda b, h, p, t: (b, h, qt(t, p), 0)),
                     pl.BlockSpec((None, G, 1, tk), lambda b, h, p, t: (b, h, 0, kt(t, p)))]
        args += list(fox)
    Gm = 0
    if mask is not None:
        Gm, nbp = mask.shape[2], mask.shape[4]
        in_specs.append(pl.BlockSpec((None, None, Gm, tq, nbp), lambda b, h, p, t: (b, h, 0, qt(t, p), 0)))
        args.append(mask)
    body = functools.partial(_flash_body, G=G, Gm=Gm, tq=tq, tk=tk, dv=dv,
                             fox=fox is not None, mask_blk=mask_blk if mask is not None else 0)
    grid_spec = pltpu.PrefetchScalarGridSpec(
        num_scalar_prefetch=1, grid=(B, hkv, len(pairs)), in_specs=in_specs,
        out_specs=pl.BlockSpec((None, tq, G * dv), lambda b, h, p, t: (b, qt(t, p), h)),
        scratch_shapes=[pltpu.VMEM((G, tq, 1), F32), pltpu.VMEM((G, tq, dv + LANE), F32)])
    return pl.pallas_call(
        body, grid_spec=grid_spec,
        out_shape=jax.ShapeDtypeStruct((B, Tq, hkv * G * dv), F32),
        compiler_params=_cparams("parallel", "parallel", "arbitrary"),
    )(jnp.asarray(table), *args)


PAGED_PER_STEP = 32


def _flash_paged_body(pt_ref, *refs, groups, n_out, per, page, n_pieces, n_groups, dv, fox, has_mask, mask_blk):
    it = iter(refs)
    q_ref = next(it)
    page_refs = [next(it) for _ in range(per)]
    new_ref, bias_ref = next(it), next(it)
    cq_ref, ck_ref = (next(it), next(it)) if fox else (None, None)
    mask_ref = next(it) if has_mask else None
    o_refs = [next(it) for _ in range(n_out)]
    m_sc, acc_sc = next(it), next(it)
    j = pl.program_id(1)
    last = j == n_groups
    tk = per * page
    tq = q_ref.shape[0]

    @pl.when(j == 0)
    def _():
        m_sc[...] = jnp.full(m_sc.shape, NEG, F32)
        acc_sc[...] = jnp.zeros_like(acc_sc)

    def attend(sources):
        nk = len(sources) * page
        loaded = {}

        def piece(slot):
            if slot not in loaded:
                parts = [ref[pl.ds(slot, page, stride=n_pieces), :] for ref in sources]
                loaded[slot] = jnp.concatenate(parts, axis=0).astype(BF16)
            return loaded[slot]

        if has_mask:
            nbp = mask_ref.shape[-1]
            kpos = j * tk + lax.broadcasted_iota(jnp.int32, (nbp, nk), 1)
            lo = lax.broadcasted_iota(jnp.int32, (nbp, nk), 0) * mask_blk
            expand = jnp.where((kpos >= lo) & (kpos < lo + mask_blk), 1.0, 0.0).astype(BF16)
        ones = jnp.ones((nk, LANE), BF16)
        row = 0
        for grp in groups:
            G = len(grp["q_units"])
            k = piece(grp["k_piece"])
            v = jnp.concatenate([piece(s) for s in grp["v_pieces"]] + [ones], axis=1)
            q = jnp.concatenate([q_ref[:, u * DH:(u + 1) * DH] for u in grp["q_units"]], axis=0)
            q = (q * (DH ** -0.5)).astype(BF16)
            s = lax.dot_general(q, k, (((1,), (1,)), ((), ())), preferred_element_type=F32).reshape(G, tq, nk)
            b0 = grp["bias_head"]
            s = s + (bias_ref[:, :, :nk] if b0 is None else bias_ref[b0:b0 + G, :, :nk])
            if fox:
                f0 = grp["fox_head"]
                s = s + (cq_ref[f0:f0 + G] - ck_ref[f0:f0 + G, :, :nk])
            if has_mask:
                msk = mask_ref[grp["mask_head"]]
                gm = msk.shape[0]
                picked = jnp.dot(msk.reshape(gm * tq, nbp).astype(BF16), expand, preferred_element_type=F32)
                s = s + picked.reshape(gm, tq, nk)
            rows = slice(row, row + G)
            m_old = m_sc[rows]
            m_new = jnp.maximum(m_old, jnp.max(s, axis=-1, keepdims=True))
            m_use = jnp.where(m_new < NEG / 2, 0.0, m_new)
            p = jnp.exp((s - m_use).astype(BF16))
            alpha = jnp.exp(m_old - m_use)
            pv = jnp.dot(p.reshape(G * tq, nk), v, preferred_element_type=F32)
            acc_sc[rows] = alpha * acc_sc[rows] + pv.reshape(G, tq, dv + LANE)
            m_sc[rows] = m_new
            row += G

    @pl.when(j < n_groups)
    def _():
        attend(page_refs)

    @pl.when(last)
    def _():
        attend([new_ref])
        row = 0
        for grp in groups:
            G = len(grp["q_units"])
            acc = acc_sc[row:row + G]
            o = acc[:, :, :dv] / jnp.maximum(acc[:, :, dv:dv + 1], 1e-30)
            out_i, unit0 = grp["out"]
            for g in range(G):
                o_refs[out_i][:, (unit0 + g) * dv:(unit0 + g + 1) * dv] = o[g]
            row += G


def _flash_paged(q, q_block, groups, out_heads, cache, n_pieces, layer, page_table, new_page, bias, *, dv,
                 fox=None, mask=None, mask_blk=0):
    B, Tq, _ = q.shape
    n_pages = page_table.shape[1]
    page = cache.shape[2] // n_pieces
    per = math.gcd(PAGED_PER_STEP, n_pages)
    n_groups = n_pages // per
    tk = per * page
    assert bias.shape[1:] == (n_groups + 1, Tq, tk)
    n_rows = sum(len(g["q_units"]) for g in groups)
    q_w = 8 * DH

    def page_spec(p):
        return pl.BlockSpec((None, None, page * n_pieces, LANE),
                            lambda b, j, pt: (layer, pt[b, jnp.minimum(j, n_groups - 1) * per + p], 0, 0))

    in_specs = [pl.BlockSpec((None, Tq, q_w), lambda b, j, pt: (b, 0, q_block))]
    in_specs += [page_spec(p) for p in range(per)]
    in_specs += [pl.BlockSpec((None, page * n_pieces, LANE), lambda b, j, pt: (b, 0, 0)),
                 pl.BlockSpec((bias.shape[0], None, Tq, tk), lambda b, j, pt: (0, j, 0, 0))]
    args = [q] + [cache] * per + [new_page, bias]
    if fox is not None:
        H = fox[0].shape[1]
        in_specs += [pl.BlockSpec((None, H, Tq, 1), lambda b, j, pt: (b, 0, 0, 0)),
                     pl.BlockSpec((None, H, 1, tk), lambda b, j, pt: (b, 0, 0, j))]
        args += list(fox)
    if mask is not None:
        in_specs.append(pl.BlockSpec((None,) + mask.shape[1:], lambda b, j, pt: (b, 0, 0, 0, 0)))
        args.append(mask)
    body = functools.partial(_flash_paged_body, groups=groups, n_out=len(out_heads), per=per, page=page,
                             n_pieces=n_pieces, n_groups=n_groups, dv=dv, fox=fox is not None,
                             has_mask=mask is not None, mask_blk=mask_blk)
    grid_spec = pltpu.PrefetchScalarGridSpec(
        num_scalar_prefetch=1, grid=(B, n_groups + 1), in_specs=in_specs,
        out_specs=[pl.BlockSpec((None, Tq, nh * dv), lambda b, j, pt: (b, 0, 0)) for nh in out_heads],
        scratch_shapes=[pltpu.VMEM((n_rows, Tq, 1), F32), pltpu.VMEM((n_rows, Tq, dv + LANE), F32)],
    )
    return pl.pallas_call(
        body, grid_spec=grid_spec,
        out_shape=[jax.ShapeDtypeStruct((B, Tq, nh * dv), F32) for nh in out_heads],
        compiler_params=_cparams("parallel", "arbitrary"),
    )(page_table, *args)


def _head_rms(x, gain):
    return x * lax.rsqrt(jnp.mean(x * x, axis=-1, keepdims=True) + RMS_EPS) * gain


def _combine_body(cmp_ref, sel_ref, win_ref, ag_ref, b_ref, c1_ref, c2_ref, d_ref, mg_ref, lp_ref, o_ref, *,
                  h_a, h_b, h_c, h_d, lam_init):
    gates = jax.nn.sigmoid(ag_ref[...])
    col = 0
    for h in range(h_a):
        sl = slice(h * DH, (h + 1) * DH)
        o = (gates[:, 3 * h:3 * h + 1] * cmp_ref[:, sl] + gates[:, 3 * h + 1:3 * h + 2] * sel_ref[:, sl]
             + gates[:, 3 * h + 2:3 * h + 3] * win_ref[:, sl])
        o_ref[:, col:col + DH] = _head_rms(o, mg_ref[:, col:col + DH]).astype(o_ref.dtype)
        col += DH
    for h in range(h_b):
        o_ref[:, col:col + DH] = _head_rms(b_ref[:, h * DH:(h + 1) * DH], mg_ref[:, col:col + DH]).astype(o_ref.dtype)
        col += DH
    for h in range(h_c):
        lp = lp_ref[h]
        lam = (jnp.exp(jnp.sum(lp[0:1] * lp[1:2], axis=-1, keepdims=True))
               - jnp.exp(jnp.sum(lp[2:3] * lp[3:4], axis=-1, keepdims=True)) + lam_init)
        sl = slice(h * 2 * DH, (h + 1) * 2 * DH)
        o = c1_ref[:, sl] - lam * c2_ref[:, sl]
        o_ref[:, col:col + 2 * DH] = (_head_rms(o, mg_ref[:, col:col + 2 * DH]) * (1.0 - lam_init)).astype(o_ref.dtype)
        col += 2 * DH
    for h in range(h_d):
        o_ref[:, col:col + DH] = _head_rms(d_ref[:, h * DH:(h + 1) * DH], mg_ref[:, col:col + DH]).astype(o_ref.dtype)
        col += DH


def _combine(o_cmp, o_sel, o_win, proj, o_b, o_c1, o_c2, o_d, mix_gain, diff_lambda, layer, *, tm, lam_init):
    M = o_cmp.shape[0]
    D = mix_gain.shape[-1]
    h_a, h_b, h_d = o_cmp.shape[1] // DH, o_b.shape[1] // DH, o_d.shape[1] // DH
    h_c = o_c1.shape[1] // (2 * DH)
    row = lambda w: pl.BlockSpec((tm, w), lambda m: (m, 0))
    body = functools.partial(_combine_body, h_a=h_a, h_b=h_b, h_c=h_c, h_d=h_d, lam_init=lam_init)
    return pl.pallas_call(
        body, grid=(M // tm,),
        in_specs=[row(o_cmp.shape[1]), row(o_sel.shape[1]), row(o_win.shape[1]),
                  pl.BlockSpec((tm, LANE), lambda m: (m, U_AG)),
                  row(o_b.shape[1]), row(o_c1.shape[1]), row(o_c2.shape[1]), row(o_d.shape[1]),
                  pl.BlockSpec((None, 1, D), lambda m: (layer, 0, 0)),
                  pl.BlockSpec((None,) + diff_lambda.shape[1:], lambda m: (layer, 0, 0, 0))],
        out_specs=pl.BlockSpec((tm, D), lambda m: (m, 0)),
        out_shape=jax.ShapeDtypeStruct((M, D), BF16),
        compiler_params=_cparams("parallel"),
    )(o_cmp, o_sel, o_win, proj, o_b, o_c1, o_c2, o_d, mix_gain, diff_lambda)


def _t5_bucket_np(dist):
    n = np.maximum(dist, 0)
    exact = N_BUCKETS // 2
    nf = np.maximum(n, 1).astype(np.float32)
    big = exact + (np.log(nf / np.float32(exact)) / np.float32(math.log(MAX_DIST / exact))
                   * np.float32(N_BUCKETS - exact)).astype(np.int32)
    return np.where(n < exact, n, np.minimum(big, N_BUCKETS - 1)).astype(np.int32)


def _bias_from_dist(tab_t, dist, valid):
    if tab_t is None:
        return jnp.asarray(np.where(valid, 0.0, NEG).astype(np.float32))[None]
    idx = _t5_bucket_np(dist).reshape(-1)
    b = jnp.take(tab_t, jnp.asarray(idx), axis=1).reshape((tab_t.shape[0],) + dist.shape)
    return jnp.where(jnp.asarray(valid)[None], b, NEG)


def _dist_table(tab_t):
    if tab_t is None:
        return jnp.zeros((1, MAX_DIST + 1), F32)
    return jnp.take(tab_t, jnp.asarray(_t5_bucket_np(np.arange(MAX_DIST + 1))), axis=1)


def _bias_range(bd, lo, hi, window=None, descending=False):
    H, nd = bd.shape
    far = 10 ** 9
    w = far if window is None else window
    assert w >= nd - 1
    parts = []
    for seg_lo, seg_hi, kind in ((-far, 0, "neg"), (0, nd - 1, "tab"), (nd - 1, w, "far"), (w, far, "neg")):
        a, b = max(lo, seg_lo), min(hi, seg_hi)
        if b <= a:
            continue
        if kind == "neg":
            parts.append(jnp.full((H, b - a), NEG, F32))
        elif kind == "tab":
            parts.append(bd[:, ::-1][:, nd - b:nd - a] if descending else bd[:, a:b])
        else:
            parts.append(jnp.broadcast_to(bd[:, nd - 1:], (H, b - a)))
    return jnp.concatenate(parts[::-1] if descending else parts, axis=1)


def _toeplitz_body(vec_ref, o_ref, *, t, stride, transpose, mask_last):
    rows = o_ref.shape[1] if transpose else o_ref.shape[0]
    x = jnp.broadcast_to(vec_ref[...], (rows, vec_ref.shape[-1]))
    x = pltpu.roll(x, 0, 1, stride=stride, stride_axis=0)[:, :t]
    if mask_last:
        x = jnp.where(lax.broadcasted_iota(jnp.int32, x.shape, 0) < rows - 1, x, NEG)
    o_ref[...] = x.T if transpose else x


def _toeplitz(vec, rows, t, stride=1, transpose=False, mask_last=False):
    H, NV, _, L = vec.shape
    out = (t, rows) if transpose else (rows, t)
    body = functools.partial(_toeplitz_body, t=t, stride=stride, transpose=transpose, mask_last=mask_last)
    return pl.pallas_call(
        body, grid=(H, NV),
        in_specs=[pl.BlockSpec((None, None, 1, L), lambda h, v: (h, v, 0, 0))],
        out_specs=pl.BlockSpec((None, None) + out, lambda h, v: (h, v, 0, 0)),
        out_shape=jax.ShapeDtypeStruct((H, NV) + out, F32),
        compiler_params=_cparams("parallel", "parallel"),
    )(vec)


def _prompt_bias(bd, t, n_var, window=None):
    vec = jnp.stack([jnp.concatenate([_bias_range(bd, v * t - t, v * t + 1, window, descending=True),
                                      _bias_range(bd, v * t + 1, v * t + t, window, descending=True)], axis=1)
                     for v in range(n_var)], axis=1)
    return _toeplitz(vec[:, :, None, :], t, t)


def _decode_bias(bd, tq, q_rel, n_keys, tk, window=None):
    H = bd.shape[0]
    vec = _bias_range(bd, q_rel - n_keys + 1, q_rel + tq, window, descending=True)
    rows = jnp.stack([vec[:, tq - 1 - i:tq - 1 - i + n_keys] for i in range(tq)], axis=1)
    return rows.reshape(H, tq, n_keys // tk, tk).transpose(0, 2, 1, 3)


def _cmp_bias_prompt(bd, T, n):
    H = bd.shape[0]
    L = T + CMP_STRIDE * n
    vec = jnp.concatenate([_bias_range(bd, 1 - CMP_LEN, T + 1 - CMP_LEN), jnp.full((H, L - T), NEG, F32)], axis=1)
    return _toeplitz(vec[:, None, None, :], n, T, stride=CMP_STRIDE, transpose=True, mask_last=True)[:, 0]


def _cmp_bias(tab_t, q_pos, n):
    end = np.arange(n) * CMP_STRIDE + CMP_LEN - 1
    dist = q_pos[:, None] - end[None, :]
    valid = (dist >= 0) & (np.arange(n)[None, :] < n - 1)
    return _bias_from_dist(tab_t, dist, valid)


def _overlap_np(n, n_sel, n_sel_pad):
    cs = np.arange(n)[:, None] * CMP_STRIDE
    js = np.arange(n_sel_pad)[None, :] * SEL_BLOCK
    ov = (cs < js + SEL_BLOCK) & (cs + CMP_LEN > js) & (np.arange(n)[:, None] < n - 1) & (np.arange(n_sel_pad)[None, :] < n_sel)
    return ov.astype(np.float32)


def _round_up(x, m):
    return -(-x // m) * m


def _reorder_w_in(w, dims):
    h_a, hkv_a, h_b, hkv_b, h_c, hkv_c, h_d, hkv_d = dims
    widths = [h_a * DH] + [hkv_a * DH] * 6 + [h_a * 3, h_b * DH, hkv_b * DH, hkv_b * DH, h_b,
                                             h_c * 2 * DH, hkv_c * 2 * DH, hkv_c * 2 * DH, h_d * DH, hkv_d * DH, hkv_d * DH]
    offs = np.concatenate([[0], np.cumsum(widths)])
    seg = lambda i: w[:, offs[i]:offs[i + 1]]
    D = w.shape[0]
    g_c = h_c // hkv_c
    c_q = seg(12).reshape(D, hkv_c, g_c, 2, DH).transpose(0, 1, 3, 2, 4).reshape(D, -1)
    pad = lambda a: jnp.pad(a, ((0, 0), (0, LANE - a.shape[1])))
    return jnp.concatenate([seg(0), seg(8), seg(15), c_q, seg(1), seg(2), seg(3), seg(4), seg(5), seg(6),
                            seg(9), seg(10), seg(13), seg(14), seg(16), seg(17), pad(seg(7)), pad(seg(11))], axis=1)


def _layer(x, B, T, mod, layer, past, prm, tabs, dims, tiles):
    h_a, hkv_a, h_b, hkv_b, h_c, hkv_c, h_d, hkv_d = dims
    g_a, g_b, g_c, g_d = h_a // hkv_a, h_b // hkv_b, h_c // hkv_c, h_d // hkv_d
    D = x.shape[1]
    M = B * T
    sh1, sc1, gt1, sh2, sc2, gt2 = [m[:, None, :] for m in jnp.split(mod, 6, axis=-1)]
    tm_n = tiles["norm_tm"]
    h = _norm(x, prm["norm_attn"], layer, tm=tm_n, sc=sc1, sh=sh1, rows_per_batch=T, out_dtype=BF16)
    proj = _matmul(h, prm["w_in_r"], None, **tiles["w_in"]).reshape(B, T, U_TOTAL * LANE)
    new = {
        "nsa": proj[:, :, U_NSA * LANE:U_WIN * LANE], "fox": proj[:, :, U_FOX * LANE:U_DIFF * LANE],
        "diff": proj[:, :, U_DIFF * LANE:U_MOBA * LANE], "moba": proj[:, :, U_MOBA * LANE:U_AG * LANE],
    }
    kw_new = proj[:, :, U_WIN * LANE:U_FOX * LANE]
    tq, tk = tiles["tq"], tiles["tk"]

    if past is None:
        q_off, Tk = 0, T
        src = dict(
            cmp=_lane_src(proj, U_NSA, jstride=hkv_a),
            sel_k=_lane_src(proj, U_NSA + 2 * hkv_a), sel_v=_lane_src(proj, U_NSA + 3 * hkv_a),
            win_k=_lane_src(proj, U_WIN), win_v=_lane_src(proj, U_WIN + hkv_a),
            fox_k=_lane_src(proj, U_FOX), fox_v=_lane_src(proj, U_FOX + hkv_b),
            diff_k=[_lane_src(proj, U_DIFF + half, stride=2) for half in range(2)],
            diff_v=_lane_src(proj, U_DIFF + 2 * hkv_c),
            moba_k=_lane_src(proj, U_MOBA), moba_v=_lane_src(proj, U_MOBA + hkv_d))
        assert T >= tiles["wbuf"]
        new["win"] = kw_new[:, T - tiles["wbuf"]:]
        lf_new, cum = _fox_cum(proj[:, :, U_BF * LANE:], prm["fox_fbias"], None)
        mode, nsteps = "causal", T // tk
        band_steps = tabs["win"].shape[1]
        n_cmp_pad = T // CMP_STRIDE
        tk_pad = T
    else:
        q_off = past["len"]
        Tk = q_off + T
        page = past["page"]

        pieces = past["pieces"]

        def contiguous(name, n_lanes, slots):
            new_page = jnp.pad(new[name][:, :, :n_lanes], ((0, 0), (0, page - T), (0, 0)))
            return _gather_pages(past[name], layer, past["pt"], new_page, slots=slots,
                                 n_pieces=pieces[name]["n"])

        pc = pieces["nsa"]["of"]
        nsa_cmp = contiguous("nsa", 2 * hkv_a * DH, [pc(t, h) for t in range(2) for h in range(hkv_a)])
        pc = pieces["moba"]["of"]
        moba_means = _block_means_paged(past["moba"], pieces["moba"]["n"], layer, past["pt"],
                                        [pc(0, h) for h in range(hkv_d)])
        past_lf = _gather_pages(past["logf"], layer, past["pt"], jnp.zeros((B, page, h_b), F32))
        win_kv = jnp.concatenate([past["win"][layer], kw_new], axis=1)
        wbuf = past["win"].shape[2]
        new["win"] = win_kv[:, -wbuf:]
        src = dict(cmp=_lane_src(nsa_cmp, 0, jstride=hkv_a), moba_k=None,
                   win_k=_lane_src(win_kv, 0), win_v=_lane_src(win_kv, hkv_a))
        lf_new, cum = _fox_cum(proj[:, :, U_BF * LANE:], prm["fox_fbias"], past_lf, n_past=q_off)
        n_cmp_pad = q_off // CMP_STRIDE
        tk_pad = tabs["causal"].shape[1] * tabs["causal"].shape[3]

        def paged(name, q_block, groups, out_heads, bias, dv=DH, **kw):
            info = pieces[name]
            rows = new[name].reshape(B, T, info["n_t"], info["hkv"], info["halves"], LANE)
            rows = rows.transpose(0, 1, 2, 4, 3, 5).reshape(B, T * info["n"], LANE)
            new_page = jnp.pad(rows, ((0, 0), (0, (page - T) * info["n"]), (0, 0)))
            return _flash_paged(proj, q_block, groups, out_heads, past[name], info["n"], layer, past["pt"],
                                new_page, bias, dv=dv, **kw)
    new["logf"] = lf_new

    cmp_kv = _compress(src["cmp"], hkv_a, n_cmp_pad, layer, prm["cmp_pos"], prm["cmp_w1"], prm["cmp_w2"])
    n_sel = -(-Tk // SEL_BLOCK)
    o_cmp, sel_mask = _nsa_cmp(proj, U_QA, cmp_kv, tabs["cmp"], tabs["overlap"], hkv=hkv_a, G=g_a,
                               tq=min(tiles["cmp_tq"], T), q_off=q_off, n_sel=n_sel)
    sel_mask = sel_mask[:, :, None]
    cum_t = cum.transpose(0, 2, 1)
    cq = cum_t[:, :, Tk - T:, None]
    ck = jnp.pad(cum_t, ((0, 0), (0, 0), (0, tk_pad - Tk)))[:, :, None, :]
    nb = -(-Tk // MOBA_BLOCK)
    moba_mask = _moba_gate(proj, U_QD, src["moba_k"], hkv=hkv_d, G=g_d, Tk=Tk, q_off=q_off,
                           nbp=_round_up(nb, LANE), means=None if past is None else moba_means)
    if past is None:
        o_sel = _flash(proj, U_QA, src["sel_k"], src["sel_v"], tabs["a"], hkv=hkv_a, G=g_a,
                       tq=tq, tk=tk, dv=DH, mode=mode, nsteps=nsteps, mask=sel_mask, mask_blk=SEL_BLOCK)
        o_win = _flash(proj, U_QA, src["win_k"], src["win_v"], (tabs["win"], 0, False), hkv=hkv_a, G=g_a,
                       tq=tabs["win"].shape[2], tk=tabs["win"].shape[3], dv=DH, mode="band", nsteps=band_steps)
        o_b = _flash(proj, U_QB, src["fox_k"], src["fox_v"], tabs["mask"], hkv=hkv_b, G=g_b,
                     tq=tq, tk=tk, dv=DH, mode=mode, nsteps=nsteps, fox=(cq, ck))
        o_c = [_flash(proj, U_QC + half * g_c, src["diff_k"][half], src["diff_v"], tabs["c"],
                      hkv=hkv_c, G=g_c, tq=tq, tk=tk, dv=2 * DH, mode=mode, nsteps=nsteps, q_stride=2)
               for half in range(2)]
        o_d = _flash(proj, U_QD, src["moba_k"], src["moba_v"], tabs["d"], hkv=hkv_d, G=g_d,
                     tq=tq, tk=tk, dv=DH, mode=mode, nsteps=nsteps, mask=moba_mask, mask_blk=MOBA_BLOCK)
    else:
        o_win = _flash(proj, U_QA, src["win_k"], src["win_v"], (tabs["win"], 0, False), hkv=hkv_a, G=g_a,
                       tq=tq, tk=tabs["win"].shape[3], dv=DH, mode="full", nsteps=1)
        causal = tabs["causal"]
        pc = pieces["nsa"]["of"]
        o_sel, = paged("nsa", U_QA // 8, [
            dict(q_units=[h * g_a + g for g in range(g_a)], k_piece=pc(2, h), v_pieces=[pc(3, h)],
                 bias_head=h * g_a, mask_head=h, out=(0, h * g_a)) for h in range(hkv_a)],
            [h_a], causal[:h_a], mask=sel_mask, mask_blk=SEL_BLOCK)
        pc = pieces["fox"]["of"]
        o_b, = paged("fox", U_QB // 8, [
            dict(q_units=[h * g_b + g for g in range(g_b)], k_piece=pc(0, h), v_pieces=[pc(1, h)],
                 bias_head=None, fox_head=h * g_b, out=(0, h * g_b)) for h in range(hkv_b)],
            [h_b], causal[h_a + h_c + h_d:], fox=(cq, ck))
        pc = pieces["diff"]["of"]
        o_c = paged("diff", U_QC // 8, [
            dict(q_units=[(h * 2 + half) * g_c + g for g in range(g_c)], k_piece=pc(0, h, half),
                 v_pieces=[pc(1, h, 0), pc(1, h, 1)], bias_head=h * g_c, out=(half, h * g_c))
            for half in range(2) for h in range(hkv_c)],
            [h_c, h_c], causal[h_a:h_a + h_c], dv=2 * DH)
        pc = pieces["moba"]["of"]
        o_d, = paged("moba", U_QD // 8, [
            dict(q_units=[h * g_d + g for g in range(g_d)], k_piece=pc(0, h), v_pieces=[pc(1, h)],
                 bias_head=h * g_d, mask_head=h, out=(0, h * g_d)) for h in range(hkv_d)],
            [h_d], causal[h_a + h_c:h_a + h_c + h_d], mask=moba_mask, mask_blk=MOBA_BLOCK)

    lam_init = 0.8 - 0.6 * math.exp(-0.3 * layer)
    flat = lambda a: a.reshape(M, a.shape[-1])
    o = _combine(flat(o_cmp), flat(o_sel), flat(o_win), flat(proj), flat(o_b), flat(o_c[0]), flat(o_c[1]),
                 flat(o_d), prm["mix_gain"], prm["diff_lambda"], layer, tm=tiles["comb_tm"], lam_init=lam_init)

    if T % tiles["w_out"]["tm"] == 0:
        gates = dict(gate1=gt1, gate2=gt2, rows_per_batch=T)
    else:
        gates = dict(gate1=jnp.repeat(gt1[:, 0], T, axis=0), gate2=jnp.repeat(gt2[:, 0], T, axis=0), rows_per_batch=None)
    x = _matmul(o, prm["w_out"], layer, res=x, gate=gates["gate1"], rows_per_batch=gates["rows_per_batch"],
                **tiles["w_out"])
    h2 = _norm(x, prm["norm_ffn"], layer, tm=tm_n, sc=sc2, sh=sh2, rows_per_batch=T, out_dtype=BF16)
    act = _matmul(h2, prm["w_gate"], layer, w2=prm["w_up"], out_dtype=BF16, **tiles["w_ff"])
    x = _matmul(act, prm["w_down"], layer, res=x, gate=gates["gate2"], rows_per_batch=gates["rows_per_batch"],
                **tiles["w_down"])
    return x, new


def _group_tables(t5_table, dims, T, q_off, tiles, tk_pad, win_len):
    h_a, hkv_a, h_b, hkv_b, h_c, hkv_c, h_d, hkv_d = dims
    tab_t = t5_table.astype(F32).T
    tab_a = tab_t[:h_a]
    tq, tk = tiles["tq"], tiles["tk"]
    Tk = q_off + T
    n = (Tk - CMP_LEN) // CMP_STRIDE + 2
    n_sel = -(-Tk // SEL_BLOCK)
    tabs = {"overlap": jnp.asarray(_overlap_np(n, n_sel, _round_up(n_sel, LANE)))}
    bd = jnp.concatenate([_dist_table(tab_t), _dist_table(None)], axis=0)
    n_bias = tab_t.shape[0]
    if q_off == 0:
        assert tq == tk
        tw = tiles["win_t"]
        nband = (WINDOW - 1 + tw - 1) // tw + 1
        tabs.update(cmp=_cmp_bias_prompt(bd[:h_a], T, n), causal=_prompt_bias(bd, tq, 3),
                    win=_prompt_bias(bd[:h_a], tw, min(nband, T // tw), WINDOW))
    else:
        assert win_len % 8 == 0 and q_off >= win_len - T
        tabs.update(cmp=_cmp_bias(tab_a, q_off + np.arange(T), n), causal=_decode_bias(bd, T, q_off, tk_pad, tk),
                    win=_decode_bias(bd[:h_a], T, win_len - T, win_len, win_len, WINDOW))
    tabs.update(a=(tabs["causal"], 0, False), c=(tabs["causal"], h_a, False), d=(tabs["causal"], h_a + h_c, False),
                mask=(tabs["causal"], n_bias, True))
    return tabs


PROMPT_TILES = dict(
    tq=512, tk=512, win_t=512, cmp_tq=1024, norm_tm=256, comb_tm=256,
    w_in=dict(tm=1024, tn=512, tk=4096), w_out=dict(tm=1024, tn=512, tk=4096),
    w_ff=dict(tm=1024, tn=256, tk=4096), w_down=dict(tm=2048, tn=1024, tk=512),
)


def _sample_tiles(M, T, tk):
    return dict(
        tq=T, tk=tk, cmp_tq=T, norm_tm=T, comb_tm=M,
        w_in=dict(tm=M, tn=512, tk=4096), w_out=dict(tm=M, tn=512, tk=4096),
        w_ff=dict(tm=M, tn=256, tk=4096), w_down=dict(tm=M, tn=1024, tk=1024),
    )


def kernel(x_prompt, x_sample, cache_nsa, state_nsa_win, cache_fox, cache_fox_logf, cache_diff, cache_moba,
           page_table, c_prompt, c_sample, t5_table, ada_w, ada_b, norm_attn, norm_ffn, w_in, w_out, mix_gain,
           nsa_cmp_pos, nsa_cmp_w1, nsa_cmp_w2, fox_fbias, diff_lambda, w_gate, w_up, w_down, final_norm):
    depth = w_in.shape[0]
    B, T, D = x_prompt.shape
    Bs, Ts, _ = x_sample.shape
    hkv_a, hkv_b, hkv_c, hkv_d = cache_nsa.shape[4], cache_fox.shape[4], cache_diff.shape[4], cache_moba.shape[4]
    h_b, h_c = fox_fbias.shape[1], diff_lambda.shape[1]
    h_a = (D // DH - h_b - 2 * h_c) // 2
    h_d = h_a
    dims = (h_a, hkv_a, h_b, hkv_b, h_c, hkv_c, h_d, hkv_d)
    page = cache_nsa.shape[2]
    n_pages = page_table.shape[1]
    past_len = n_pages * page
    pages_per_step = math.gcd(PAGED_PER_STEP, n_pages)
    tk_s = pages_per_step * page
    tk_pad = (n_pages // pages_per_step + 1) * tk_s
    win_len = state_nsa_win.shape[2] + Ts

    p_tiles = dict(PROMPT_TILES, wbuf=state_nsa_win.shape[2])
    s_tiles = _sample_tiles(Bs * Ts, Ts, tk_s)
    tabs_p = _group_tables(t5_table, dims, T, 0, p_tiles, T, 0)
    tabs_s = _group_tables(t5_table, dims, Ts, past_len, s_tiles, tk_pad, win_len)

    def piece_rows(c):
        d, n_phys, pg, n_t, hkv, w = c.shape
        halves = w // LANE
        c = c.reshape(d, n_phys, pg, n_t, hkv, halves, LANE).transpose(0, 1, 2, 3, 5, 4, 6)
        info = dict(n=n_t * hkv * halves, n_t=n_t, hkv=hkv, halves=halves,
                    of=lambda t, h, half=0: (t * halves + half) * hkv + h)
        return c.reshape(d, n_phys, pg * info["n"], LANE), info

    rows = {name: piece_rows(c) for name, c in
            (("nsa", cache_nsa), ("fox", cache_fox), ("diff", cache_diff), ("moba", cache_moba))}
    past = dict({name: r[0] for name, r in rows.items()}, pieces={name: r[1] for name, r in rows.items()},
                logf=cache_fox_logf,
                win=state_nsa_win.reshape(depth, Bs, state_nsa_win.shape[2], -1),
                pt=page_table, len=past_len, page=page)

    n_c = _round_up(B + Bs, 8)
    c_all = jnp.pad(jnp.concatenate([c_prompt, c_sample], axis=0), ((0, n_c - B - Bs), (0, 0)))
    xp, xs = x_prompt.reshape(B * T, D), x_sample.reshape(Bs * Ts, D)
    names = ("nsa", "win", "fox", "logf", "diff", "moba")
    st_p = {n: [] for n in names}
    st_s = {n: [] for n in names}
    r3 = lambda a: a.reshape(a.shape[0], 1, a.shape[1])
    for l in range(depth):
        mod = _matmul(c_all, ada_w, l, bias=ada_b[l][None], silu_in=True, tm=n_c, tn=512, tk=D)
        prm = dict(norm_attn=r3(norm_attn), norm_ffn=r3(norm_ffn), w_in_r=_reorder_w_in(w_in[l], dims), w_out=w_out,
                   mix_gain=r3(mix_gain), cmp_pos=nsa_cmp_pos, cmp_w1=nsa_cmp_w1, cmp_w2=nsa_cmp_w2,
                   fox_fbias=fox_fbias[l][None], diff_lambda=diff_lambda, w_gate=w_gate, w_up=w_up, w_down=w_down)
        xp, new_p = _layer(xp, B, T, mod[:B], l, None, prm, tabs_p, dims, p_tiles)
        xs, new_s = _layer(xs, Bs, Ts, mod[B:B + Bs], l, past, prm, tabs_s, dims, s_tiles)
        for n in names:
            st_p[n].append(new_p[n])
            st_s[n].append(new_s[n])
    y_p = _norm(xp, final_norm[None], None, tm=p_tiles["norm_tm"]).reshape(B, T, D)
    y_s = _norm(xs, final_norm[None], None, tm=s_tiles["norm_tm"]).reshape(Bs, Ts, D)

    def stack(st, name, tail):
        a = jnp.stack(st[name])
        return a.reshape(a.shape[:3] + tail)

    kv = lambda n, hk, w: (n, hk, w)
    shapes = {"nsa": kv(4, hkv_a, DH), "win": kv(2, hkv_a, DH), "fox": kv(2, hkv_b, DH), "logf": (h_b,),
              "diff": kv(2, hkv_c, 2 * DH), "moba": kv(2, hkv_d, DH)}
    out = [y_p, y_s]
    for n in names:
        out += [stack(st_p, n, shapes[n]), stack(st_s, n, shapes[n])]
    return tuple(out)
```
